```python
import jax, jax.numpy as jnp
from jax import lax
import numpy as np

D_MODEL = 1024
BATCH = 16
SEQ = 2048
DEPTH = 4

GRID_W = 64
CTX_LEN = 256
BRANCH_W = D_MODEL
N_BRANCH = 3
CHUNK = 64
CONV_W = 4
M_HEADS = 4
M_DQK = 128
M_DV = BRANCH_W // M_HEADS
G_HEADS = 8
G_DK = 128
G_DV = BRANCH_W // G_HEADS
R_BLOCKS = 8
R_BDIM = BRANCH_W // R_BLOCKS
LRU_C = 8.0
D_FF = 2816
N_EXPERTS = 8
TOP_K = 2
E_FF = 2816
MOE_BLOCK = 128
EPS = 1e-6
N_DENSE = (DEPTH + 1) // 2
N_MOE = DEPTH // 2
IN_COLS = (M_HEADS * M_DQK, M_HEADS * M_DQK, BRANCH_W, BRANCH_W, 4 * M_HEADS,
           2 * G_HEADS * G_DK + BRANCH_W, BRANCH_W, 2 * G_HEADS, 2 * G_HEADS,
           BRANCH_W, BRANCH_W, N_BRANCH * D_MODEL)
N_IN = sum(IN_COLS)

kernel_name = 'hybrid_mlstm_gdn_rglru_moe_flow_trunk'


def rmsnorm(x, g):
    xf = x.astype(jnp.float32)
    y = xf * lax.rsqrt(jnp.mean(xf * xf, axis=-1, keepdims=True) + EPS)
    return (y * g.astype(jnp.float32)).astype(x.dtype)


def l2norm(x):
    return x * lax.rsqrt(jnp.sum(x * x, axis=-1, keepdims=True) + EPS)


def centred_dwconv(x, w):
    pad_l = CONV_W // 2
    return lax.conv_general_dilated(x, w[:, None, :].astype(x.dtype), window_strides=(1,),
                                    padding=[(pad_l, CONV_W - 1 - pad_l)],
                                    dimension_numbers=('NWC', 'WIO', 'NWC'),
                                    feature_group_count=x.shape[-1])


def to_col_major(a, rows):
    b = a.shape[0]
    return a.reshape((b, rows, GRID_W) + a.shape[2:]).swapaxes(1, 2).reshape(a.shape)


def to_row_major(a, rows):
    b = a.shape[0]
    return a.reshape((b, GRID_W, rows) + a.shape[2:]).swapaxes(1, 2).reshape(a.shape)


def to_chunks(a):
    b, t = a.shape[:2]
    return jnp.moveaxis(a.reshape((b, t // CHUNK, CHUNK) + a.shape[2:]), 3, 1)


def from_chunks(a):
    a = jnp.moveaxis(a, 1, 3)
    return a.reshape((a.shape[0], a.shape[1] * a.shape[2]) + a.shape[3:])


def bidirectional_prefix(run, ctx_dirs, lat_dirs, zero_state):
    outs_c, outs_l = [], []
    for d in range(2):
        rev = (lambda a: jnp.flip(a, axis=1)) if d == 1 else (lambda a: a)
        out_c, state = run(tuple(rev(a) for a in ctx_dirs[d]), zero_state)
        out_l, _ = run(tuple(rev(a) for a in lat_dirs[d]), state)
        outs_c.append(rev(out_c))
        outs_l.append(rev(out_l))
    return outs_c[0] + outs_c[1], outs_l[0] + outs_l[1]


def mlstm_run(args, state):
    q, k, v, ig, lf = (to_chunks(a) for a in args)
    b = jnp.cumsum(lf, axis=-1)
    b_end = b[..., -1]
    a_end = b_end[..., None] - b + ig
    m_loc = jnp.max(a_end, axis=-1)
    w_end = jnp.exp(a_end - m_loc[..., None])
    c_loc = jnp.einsum('bhnsk,bhnsv->bhnkv', k * w_end[..., None], v)
    n_loc = jnp.einsum('bhnsk,bhns->bhnk', k, w_end)

    def step(carry, xs):
        c_prev, n_prev, m_prev = carry
        c_l, n_l, m_l, b_e = xs
        m_new = jnp.maximum(b_e + m_prev, m_l)
        s_prev = jnp.exp(b_e + m_prev - m_new)
        s_loc = jnp.exp(m_l - m_new)
        c_new = s_prev[..., None, None] * c_prev + s_loc[..., None, None] * c_l
        n_new = s_prev[..., None] * n_prev + s_loc[..., None] * n_l
        return (c_new, n_new, m_new), carry

    xs = tuple(jnp.moveaxis(a, 2, 0) for a in (c_loc, n_loc, m_loc, b_end))
    final, starts = lax.scan(step, state, xs)
    c0, n0, m0 = (jnp.moveaxis(a, 0, 2) for a in starts)
    causal = jnp.tril(jnp.ones((CHUNK, CHUNK), bool))
    d_log = jnp.where(causal, b[..., :, None] - b[..., None, :] + ig[..., None, :], -jnp.inf)
    m_inter = b + m0[..., None]
    m_comb = jnp.maximum(m_inter, jnp.max(d_log, axis=-1))
    s = jnp.einsum('bhntk,bhnsk->bhnts', q, k) * jnp.exp(d_log - m_comb[..., None])
    e_inter = jnp.exp(m_inter - m_comb)
    num = jnp.einsum('bhnts,bhnsv->bhntv', s, v) + e_inter[..., None] * jnp.einsum('bhntk,bhnkv->bhntv', q, c0)
    den = jnp.sum(s, axis=-1) + e_inter * jnp.einsum('bhntk,bhnk->bhnt', q, n0)
    h = num / jnp.maximum(jnp.abs(den), jnp.exp(-m_comb))[..., None]
    return from_chunks(h), final


def gdn_run(args, state):
    q, k, v, beta, g = (to_chunks(a) for a in args)
    gam = jnp.cumsum(g, axis=-1)
    incl = jnp.tril(jnp.ones((CHUNK, CHUNK), bool))
    strict = jnp.tril(jnp.ones((CHUNK, CHUNK), bool), -1)
    decay = jnp.exp(jnp.where(incl, gam[..., :, None] - gam[..., None, :], -jnp.inf))
    kb = k * beta[..., None]
    a_low = jnp.where(strict, jnp.einsum('bhntk,bhnsk->bhnts', kb, k) * decay, 0.0)
    rhs = jnp.concatenate([v * beta[..., None], kb * jnp.exp(gam)[..., None]], axis=-1)
    sol = lax.linalg.triangular_solve(a_low + jnp.eye(CHUNK, dtype=a_low.dtype), rhs,
                                      left_side=True, lower=True)
    dv = v.shape[-1]
    u, w = sol[..., :dv], sol[..., dv:]
    attn = jnp.einsum('bhntk,bhnsk->bhnts', q, k) * decay
    q_dec = q * jnp.exp(gam)[..., None]
    k_tail = k * jnp.exp(gam[..., -1:] - gam)[..., None]
    chunk_dec = jnp.exp(gam[..., -1])

    def step(s_mat, xs):
        q_c, u_c, w_c, a_c, kt_c, cd = xs
        v_new = u_c - jnp.einsum('bhck,bhkv->bhcv', w_c, s_mat)
        o = jnp.einsum('bhck,bhkv->bhcv', q_c, s_mat) + jnp.einsum('bhcs,bhsv->bhcv', a_c, v_new)
        s_mat = cd[..., None, None] * s_mat + jnp.einsum('bhck,bhcv->bhkv', kt_c, v_new)
        return s_mat, o

    xs = tuple(jnp.moveaxis(a, 2, 0) for a in (q_dec, u, w, attn, k_tail, chunk_dec))
    final, o = lax.scan(step, state, xs)
    return from_chunks(jnp.moveaxis(o, 0, 2)), final


def lru_run(args, h0):
    log_a, u = args
    a = jnp.exp(log_a)
    u = u.at[:, 0].add(a[:, 0] * h0)

    def combine(left, right):
        a_l, b_l = left
        a_r, b_r = right
        return a_l * a_r, a_r * b_l + b_r

    _, h = lax.associative_scan(combine, (a, u), axis=1)
    return h, h[:, -1]


def mlstm_inputs(q, k, v, gates, f_bias):
    bsz, t = q.shape[:2]
    f32 = jnp.float32
    q = q.reshape(bsz, t, M_HEADS, M_DQK).astype(f32)
    k = k.reshape(bsz, t, M_HEADS, M_DQK).astype(f32) * (M_DQK ** -0.5)
    v = v.reshape(bsz, t, M_HEADS, M_DV).astype(f32)
    gates = gates.reshape(bsz, t, 2, 2, M_HEADS).astype(f32)
    ig = gates[:, :, :, 0]
    lf = jax.nn.log_sigmoid(gates[:, :, :, 1] + f_bias.astype(f32))
    return tuple((q, k, v, ig[:, :, d], lf[:, :, d]) for d in range(2))


def gdn_inputs(qkv, beta_pre, alpha_pre, conv_w, a_log, dt_bias):
    bsz, t = qkv.shape[:2]
    f32 = jnp.float32
    qkv = jax.nn.silu(centred_dwconv(qkv, conv_w)).astype(f32)
    q, k, v = jnp.split(qkv, [G_HEADS * G_DK, 2 * G_HEADS * G_DK], axis=-1)
    q = l2norm(q.reshape(bsz, t, G_HEADS, G_DK)) * (G_DK ** -0.5)
    k = l2norm(k.reshape(bsz, t, G_HEADS, G_DK))
    v = v.reshape(bsz, t, G_HEADS, G_DV)
    beta = jax.nn.sigmoid(beta_pre.reshape(bsz, t, 2, G_HEADS).astype(f32))
    g = -jnp.exp(a_log.astype(f32)) * jax.nn.softplus(
        alpha_pre.reshape(bsz, t, 2, G_HEADS).astype(f32) + dt_bias.astype(f32))
    return tuple((q, k, v, beta[:, :, d], g[:, :, d]) for d in range(2))


def lru_inputs(xr, conv_w, conv_b, gate_w, gate_b, lam):
    bsz, t = xr.shape[:2]
    f32 = jnp.float32
    xb = (centred_dwconv(xr, conv_w) + conv_b).astype(f32)
    pre = jnp.einsum('btnc,dgncm->btdgnm', xb.reshape(bsz, t, R_BLOCKS, R_BDIM), gate_w.astype(f32))
    pre = pre.reshape(bsz, t, 2, 2, BRANCH_W) + gate_b.astype(f32)
    r = jax.nn.sigmoid(pre[:, :, :, 0])
    i = jax.nn.sigmoid(pre[:, :, :, 1])
    log_a = -LRU_C * jax.nn.softplus(-lam.astype(f32)) * r
    u = jnp.sqrt(-jnp.expm1(2.0 * log_a)) * i * xb[:, :, None]
    return tuple((log_a[:, :, d], u[:, :, d]) for d in range(2))


def mlstm_finish(h, o, norm_g):
    bsz, t = o.shape[:2]
    h = rmsnorm(h, norm_g.reshape(M_HEADS, M_DV)).reshape(bsz, t, BRANCH_W)
    return (h * jax.nn.sigmoid(o.astype(jnp.float32))).astype(o.dtype)


def gdn_finish(h, z, norm_g):
    bsz, t = z.shape[:2]
    h = rmsnorm(h, norm_g).reshape(bsz, t, BRANCH_W)
    return (h * jax.nn.silu(z.astype(jnp.float32))).astype(z.dtype)


def lru_finish(h, y):
    return (h * jax.nn.gelu(y.astype(jnp.float32))).astype(y.dtype)


def merge_branches(ys, gate_pre, w_branch, w_out):
    bsz, t = gate_pre.shape[:2]
    proj = jnp.einsum('nbtw,nwd->nbtd', jnp.stack(ys, 0), w_branch)
    gates = jax.nn.sigmoid(gate_pre.reshape(bsz, t, N_BRANCH, D_MODEL))
    return jnp.einsum('nbtd,btnd->btd', proj, gates) @ w_out


def hybrid_mixer(hc, hl, w_in, b_in, m_fbias, m_norm_g, g_conv, g_a_log, g_dt_bias, g_norm_g,
                 r_conv, r_conv_b, r_gate_w, r_gate_b, r_lambda, w_branch, w_out):
    bsz, seq_len = hl.shape[:2]
    rows = seq_len // GRID_W
    f32 = jnp.float32
    offs = np.cumsum(IN_COLS)[:-1]
    pc = jnp.split(hc @ w_in + b_in, offs, axis=-1)
    pl = jnp.split(hl @ w_in + b_in, offs, axis=-1)
    m_zero = (jnp.zeros((bsz, M_HEADS, M_DQK, M_DV), f32), jnp.zeros((bsz, M_HEADS, M_DQK), f32),
              jnp.zeros((bsz, M_HEADS), f32))
    m_c, m_l = bidirectional_prefix(mlstm_run, mlstm_inputs(pc[0], pc[1], pc[2], pc[4], m_fbias),
                                    mlstm_inputs(pl[0], pl[1], pl[2], pl[4], m_fbias), m_zero)
    g_zero = jnp.zeros((bsz, G_HEADS, G_DK, G_DV), f32)
    d_c, d_l = bidirectional_prefix(gdn_run, gdn_inputs(pc[5], pc[7], pc[8], g_conv, g_a_log, g_dt_bias),
                                    gdn_inputs(pl[5], pl[7], pl[8], g_conv, g_a_log, g_dt_bias), g_zero)
    r_zero = jnp.zeros((bsz, BRANCH_W), f32)
    r_c, r_l = bidirectional_prefix(lru_run, lru_inputs(pc[9], r_conv, r_conv_b, r_gate_w, r_gate_b, r_lambda),
                                    lru_inputs(to_col_major(pl[9], rows), r_conv, r_conv_b, r_gate_w, r_gate_b, r_lambda),
                                    r_zero)
    r_l = to_row_major(r_l, rows)
    ys_c = [mlstm_finish(m_c, pc[3], m_norm_g), gdn_finish(d_c, pc[6], g_norm_g), lru_finish(r_c, pc[10])]
    ys_l = [mlstm_finish(m_l, pl[3], m_norm_g), gdn_finish(d_l, pl[6], g_norm_g), lru_finish(r_l, pl[10])]
    return merge_branches(ys_c, pc[11], w_branch, w_out), merge_branches(ys_l, pl[11], w_branch, w_out)


def swiglu(h, w1, w3, w2):
    return (jax.nn.silu(h @ w1) * (h @ w3)) @ w2


def moe_swiglu(h, router_w, router_b, w1, w3, w2):
    n_tok, dim = h.shape
    logits = (h @ router_w).astype(jnp.float32) + router_b.astype(jnp.float32)
    top_logit, top_e = lax.top_k(logits, TOP_K)
    gate = jax.nn.softmax(top_logit, axis=-1)
    n_assign = n_tok * TOP_K
    flat_e = top_e.reshape(-1)
    flat_tok = jnp.repeat(jnp.arange(n_tok, dtype=jnp.int32), TOP_K)
    flat_gate = gate.reshape(-1)
    order = jnp.argsort(flat_e)
    sorted_e = flat_e[order]
    counts = jnp.bincount(flat_e, length=N_EXPERTS)
    padded = (counts + MOE_BLOCK - 1) // MOE_BLOCK * MOE_BLOCK
    start = jnp.cumsum(counts) - counts
    p_end = jnp.cumsum(padded)
    p_start = p_end - padded
    dest = p_start[sorted_e] + jnp.arange(n_assign, dtype=jnp.int32) - start[sorted_e]
    n_blocks = -(-n_assign // MOE_BLOCK) + N_EXPERTS
    n_rows = n_blocks * MOE_BLOCK
    row_tok = jnp.zeros((n_rows,), jnp.int32).at[dest].set(flat_tok[order])
    row_gate = jnp.zeros((n_rows,), h.dtype).at[dest].set(flat_gate[order].astype(h.dtype))
    block_e = jnp.minimum(jnp.searchsorted(p_end, jnp.arange(n_blocks) * MOE_BLOCK, side='right'), N_EXPERTS - 1)
    xb = h[row_tok].reshape(n_blocks, MOE_BLOCK, dim)

    def expert_block(args):
        x_blk, e = args
        return swiglu(x_blk, w1[e], w3[e], w2[e])

    yb = lax.map(expert_block, (xb, block_e))
    return jnp.zeros_like(h).at[row_tok].add(yb.reshape(n_rows, dim) * row_gate[:, None])


def setup_inputs(seed: int = 0) -> dict:
    key = jax.random.key(seed)
    keys = iter(jax.random.split(key, 40))
    f32 = jnp.float32

    def nrm(shape, scale):
        return jax.random.normal(next(keys), shape, f32) * scale

    def unif(shape, lo, hi):
        return jax.random.uniform(next(keys), shape, f32, lo, hi)

    dim = D_MODEL
    dt = jnp.exp(unif((DEPTH, 2, G_HEADS), float(np.log(1e-3)), float(np.log(0.1))))
    a0 = unif((DEPTH, 2, BRANCH_W), 0.9, 0.999)
    return {
        'x': nrm((BATCH, SEQ, dim), 1.0),
        'c': nrm((BATCH, dim), 1.0),
        'ctx': nrm((BATCH, CTX_LEN, dim), 1.0),
        'c_ctx': nrm((dim,), 1.0),
        'mod_w': nrm((DEPTH, dim, 6 * dim), dim ** -0.5),
        'mod_b': nrm((DEPTH, 6 * dim), 0.02),
        'norm1_g': 1.0 + nrm((DEPTH, dim), 0.02),
        'norm2_g': 1.0 + nrm((DEPTH, dim), 0.02),
        'final_g': 1.0 + nrm((dim,), 0.02),
        'w_in': nrm((DEPTH, dim, N_IN), dim ** -0.5),
        'b_in': nrm((DEPTH, N_IN), 0.02),
        'm_fbias': unif((DEPTH, 2, M_HEADS), 3.0, 6.0),
        'm_norm_g': 1.0 + nrm((DEPTH, BRANCH_W), 0.02),
        'g_conv': nrm((DEPTH, CONV_W, 2 * G_HEADS * G_DK + BRANCH_W), CONV_W ** -0.5),
        'g_a_log': jnp.log(unif((DEPTH, 2, G_HEADS), 1.0, 16.0)),
        'g_dt_bias': dt + jnp.log(-jnp.expm1(-dt)),
        'g_norm_g': 1.0 + nrm((DEPTH, G_DV), 0.02),
        'r_conv': nrm((DEPTH, CONV_W, BRANCH_W), CONV_W ** -0.5),
        'r_conv_b': nrm((DEPTH, BRANCH_W), 0.02),
        'r_gate_w': nrm((DEPTH, 2, 2, R_BLOCKS, R_BDIM, R_BDIM), R_BDIM ** -0.5),
        'r_gate_b': nrm((DEPTH, 2, 2, BRANCH_W), 0.02),
        'r_lambda': jnp.log(a0) - jnp.log1p(-a0),
        'w_branch': nrm((DEPTH, N_BRANCH, BRANCH_W, dim), BRANCH_W ** -0.5),
        'w_out': nrm((DEPTH, dim, dim), dim ** -0.5),
        'ffn_w1': nrm((N_DENSE, dim, D_FF), dim ** -0.5),
        'ffn_w3': nrm((N_DENSE, dim, D_FF), dim ** -0.5),
        'ffn_w2': nrm((N_DENSE, D_FF, dim), D_FF ** -0.5),
        'router_w': nrm((N_MOE, dim, N_EXPERTS), dim ** -0.5),
        'router_b': nrm((N_MOE, N_EXPERTS), 0.01),
        'moe_w1': nrm((N_MOE, N_EXPERTS, dim, E_FF), dim ** -0.5),
        'moe_w3': nrm((N_MOE, N_EXPERTS, dim, E_FF), dim ** -0.5),
        'moe_w2': nrm((N_MOE, N_EXPERTS, E_FF, dim), E_FF ** -0.5),
    }


def reference(x, c, ctx, c_ctx, mod_w, mod_b, norm1_g, norm2_g, final_g, w_in, b_in, m_fbias, m_norm_g,
              g_conv, g_a_log, g_dt_bias, g_norm_g, r_conv, r_conv_b, r_gate_w, r_gate_b, r_lambda,
              w_branch, w_out, ffn_w1, ffn_w3, ffn_w2, router_w, router_b, moe_w1, moe_w3, moe_w2):
    dim = x.shape[-1]
    xc = ctx
    src_l = jax.nn.silu(c)
    src_c = jax.nn.silu(c_ctx)[None]
    for layer in range(DEPTH):
        last = layer == DEPTH - 1
        mod_l = (src_l @ mod_w[layer] + mod_b[layer])[:, None, :]
        mod_c = (src_c @ mod_w[layer] + mod_b[layer])[:, None, :]
        sh1_l, sc1_l, gt1_l, sh2_l, sc2_l, gt2_l = jnp.split(mod_l, 6, axis=-1)
        sh1_c, sc1_c, gt1_c, sh2_c, sc2_c, gt2_c = jnp.split(mod_c, 6, axis=-1)
        hl = rmsnorm(x, norm1_g[layer]) * (1.0 + sc1_l) + sh1_l
        hc = rmsnorm(xc, norm1_g[layer]) * (1.0 + sc1_c) + sh1_c
        out_c, out_l = hybrid_mixer(hc, hl, w_in[layer], b_in[layer], m_fbias[layer], m_norm_g[layer],
                                    g_conv[layer], g_a_log[layer], g_dt_bias[layer], g_norm_g[layer],
                                    r_conv[layer], r_conv_b[layer], r_gate_w[layer], r_gate_b[layer],
                                    r_lambda[layer], w_branch[layer], w_out[layer])
        x = x + gt1_l * out_l
        if not last:
            xc = xc + gt1_c * out_c
        hl = rmsnorm(x, norm2_g[layer]) * (1.0 + sc2_l) + sh2_l
        j = layer // 2
        if layer % 2 == 0:
            x = x + gt2_l * swiglu(hl, ffn_w1[j], ffn_w3[j], ffn_w2[j])
            if not last:
                hc = rmsnorm(xc, norm2_g[layer]) * (1.0 + sc2_c) + sh2_c
                xc = xc + gt2_c * swiglu(hc, ffn_w1[j], ffn_w3[j], ffn_w2[j])
        elif last:
            y = moe_swiglu(hl.reshape(-1, dim), router_w[j], router_b[j], moe_w1[j], moe_w3[j], moe_w2[j])
            x = x + gt2_l * y.reshape(hl.shape)
        else:
            hc = rmsnorm(xc, norm2_g[layer]) * (1.0 + sc2_c) + sh2_c
            n_ctx_tok = hc.shape[0] * hc.shape[1]
            tokens = jnp.concatenate([hc.reshape(-1, dim), hl.reshape(-1, dim)], axis=0)
            y = moe_swiglu(tokens, router_w[j], router_b[j], moe_w1[j], moe_w3[j], moe_w2[j])
            xc = xc + gt2_c * y[:n_ctx_tok].reshape(hc.shape)
            x = x + gt2_l * y[n_ctx_tok:].reshape(hl.shape)
    return rmsnorm(x, final_g)
```

```python
import functools

import jax
import jax.numpy as jnp
import numpy as np
from jax import lax
from jax.experimental import pallas as pl
from jax.experimental.pallas import tpu as pltpu

D_MODEL = 1024
SEQ = 2048
DEPTH = 4
GRID_W = 64
CTX_LEN = 256
S_ALL = CTX_LEN + SEQ
BRANCH_W = D_MODEL
N_BRANCH = 3
CHUNK = 64
CONV_W = 4
M_HEADS = 4
M_DQK = 128
M_DV = BRANCH_W // M_HEADS
G_HEADS = 8
G_DK = 128
G_DV = BRANCH_W // G_HEADS
R_BLOCKS = 8
R_BDIM = BRANCH_W // R_BLOCKS
LRU_C = 8.0
D_FF = 2816
N_EXPERTS = 8
TOP_K = 2
MOE_BLOCK = 128
EPS = 1e-6
IN_COLS = (M_HEADS * M_DQK, M_HEADS * M_DQK, BRANCH_W, BRANCH_W, 4 * M_HEADS,
           2 * G_HEADS * G_DK + BRANCH_W, BRANCH_W, 2 * G_HEADS, 2 * G_HEADS,
           BRANCH_W, BRANCH_W, N_BRANCH * D_MODEL)
IN_OFFS = tuple(int(v) for v in np.cumsum((0,) + IN_COLS))
SMALL_SEGS = (4, 7, 8)
BIG_SEGS = tuple(i for i in range(len(IN_COLS)) if i not in SMALL_SEGS)
N_BIG = sum(IN_COLS[i] for i in BIG_SEGS)
N_SMALL = 128
C_MQ, C_MK, C_MV, C_MO = 0, 512, 1024, 2048
C_GQ, C_GK, C_GV, C_GZ = 3072, 4096, 5120, 6144
C_RX, C_RY, C_MG = 7168, 8192, 9216

LANE = 128
VMEM_LIMIT = 56 * 1024 * 1024
F32 = jnp.float32
BF16 = jnp.bfloat16


def _cparams(sem):
    return pltpu.CompilerParams(dimension_semantics=sem, vmem_limit_bytes=VMEM_LIMIT)


def _sigmoid(v):
    return 1.0 / (1.0 + jnp.exp(-v))


def _ada_norm(x, g, mod_l, mod_c, first_row, which):
    y = x * lax.rsqrt(jnp.mean(x * x, axis=-1, keepdims=True) + EPS) * g
    sh_l, sc_l = _mod_slice(mod_l, 3 * which), _mod_slice(mod_l, 3 * which + 1)
    sh_c, sc_c = _mod_slice(mod_c, 3 * which), _mod_slice(mod_c, 3 * which + 1)
    is_ctx = _is_ctx(x.shape[0], first_row)
    return y * (1.0 + jnp.where(is_ctx, sc_c, sc_l)) + jnp.where(is_ctx, sh_c, sh_l)


def _mod_slice(mod, k):
    return mod[:, k * D_MODEL:(k + 1) * D_MODEL]


def _is_ctx(rows, first_row):
    return (lax.broadcasted_iota(jnp.int32, (rows, 1), 0) + first_row) < CTX_LEN


def _mod_kernel(src_ref, w_ref, b_ref, o_ref):
    s = src_ref[...]
    s = (s * _sigmoid(s)).astype(BF16)
    o_ref[0] = jnp.dot(s, w_ref[0].astype(BF16), preferred_element_type=F32) + b_ref[0]


def modulation(src, mod_w, mod_b):
    rows = src.shape[0]
    tn = 1536
    return pl.pallas_call(
        _mod_kernel,
        out_shape=jax.ShapeDtypeStruct((DEPTH, rows, 6 * D_MODEL), F32),
        grid=(DEPTH, 6 * D_MODEL // tn),
        in_specs=[pl.BlockSpec((rows, D_MODEL), lambda l, n: (0, 0)),
                  pl.BlockSpec((1, D_MODEL, tn), lambda l, n: (l, 0, n)),
                  pl.BlockSpec((1, 1, tn), lambda l, n: (l, 0, n))],
        out_specs=pl.BlockSpec((1, rows, tn), lambda l, n: (l, 0, n)),
        compiler_params=_cparams(("arbitrary", "arbitrary")),
        name="modulation",
    )(src, mod_w, mod_b.reshape(DEPTH, 1, 6 * D_MODEL))


IN_TM = 1152
IN_TN = 512


def _in_kernel(x_ref, g_ref, ml_ref, mc_ref, w_ref, b_ref, ws_ref, bs_ref, o_ref, os_ref, h_ref):
    n = pl.program_id(2)

    @pl.when(n == 0)
    def _():
        h = _ada_norm(x_ref[0], g_ref[...], ml_ref[0], mc_ref[0], pl.program_id(1) * IN_TM, 0)
        h_ref[...] = h.astype(BF16)
        os_ref[0] = jnp.dot(h_ref[...], ws_ref[...], preferred_element_type=F32) + bs_ref[...]

    o_ref[0] = (jnp.dot(h_ref[...], w_ref[...], preferred_element_type=F32) + b_ref[...]).astype(o_ref.dtype)


def in_projection(xs, g, mod, w_big, b_big, w_small, b_small):
    bsz = xs.shape[0]
    mod3 = mod.reshape(bsz + 1, 1, 6 * D_MODEL)
    return pl.pallas_call(
        _in_kernel,
        out_shape=(jax.ShapeDtypeStruct((bsz, S_ALL, N_BIG), BF16),
                   jax.ShapeDtypeStruct((bsz, S_ALL, N_SMALL), F32)),
        grid=(bsz, S_ALL // IN_TM, N_BIG // IN_TN),
        in_specs=[pl.BlockSpec((1, IN_TM, D_MODEL), lambda b, s, n: (b, s, 0)),
                  pl.BlockSpec((1, D_MODEL), lambda b, s, n: (0, 0)),
                  pl.BlockSpec((1, 1, 6 * D_MODEL), lambda b, s, n: (b, 0, 0)),
                  pl.BlockSpec((1, 1, 6 * D_MODEL), lambda b, s, n: (bsz, 0, 0)),
                  pl.BlockSpec((D_MODEL, IN_TN), lambda b, s, n: (0, n)),
                  pl.BlockSpec((1, IN_TN), lambda b, s, n: (0, n)),
                  pl.BlockSpec((D_MODEL, N_SMALL), lambda b, s, n: (0, 0)),
                  pl.BlockSpec((1, N_SMALL), lambda b, s, n: (0, 0))],
        out_specs=(pl.BlockSpec((1, IN_TM, IN_TN), lambda b, s, n: (b, s, n)),
                   pl.BlockSpec((1, IN_TM, N_SMALL), lambda b, s, n: (b, s, 0))),
        scratch_shapes=[pltpu.VMEM((IN_TM, D_MODEL), BF16)],
        compiler_params=_cparams(("arbitrary", "arbitrary", "arbitrary")),
        name="in_projection",
    )(xs, g.reshape(1, D_MODEL), mod3, mod3, w_big, b_big, w_small, b_small)


MERGE_TM = 768


def _merge_kernel(ym_ref, yg_ref, yr_ref, g0_ref, g1_ref, g2_ref, x_ref, ml_ref, mc_ref, wb_ref, wo_ref, o_ref):
    acc = None
    for n, (y_ref, gp_ref) in enumerate(((ym_ref, g0_ref), (yg_ref, g1_ref), (yr_ref, g2_ref))):
        p = jnp.dot(y_ref[0].astype(BF16), wb_ref[n], preferred_element_type=F32)
        p = p * _sigmoid(gp_ref[0].astype(F32))
        acc = p if acc is None else acc + p
    out = jnp.dot(acc.astype(BF16), wo_ref[...], preferred_element_type=F32)
    is_ctx = _is_ctx(MERGE_TM, pl.program_id(1) * MERGE_TM)
    gt = jnp.where(is_ctx, _mod_slice(mc_ref[0], 2), _mod_slice(ml_ref[0], 2))
    o_ref[0] = x_ref[0] + gt * out


def merge_branches(ym, yg, yr, big, xs, mod, w_branch, w_out):
    bsz = xs.shape[0]
    mod3 = mod.reshape(bsz + 1, 1, 6 * D_MODEL)
    row = lambda b, s: (b, s, 0)
    gate_blk = C_MG // D_MODEL
    const = pl.Buffered(1)
    return pl.pallas_call(
        _merge_kernel,
        out_shape=jax.ShapeDtypeStruct(xs.shape, F32),
        grid=(bsz, S_ALL // MERGE_TM),
        in_specs=[pl.BlockSpec((1, MERGE_TM, D_MODEL), row),
                  pl.BlockSpec((1, MERGE_TM, D_MODEL), row),
                  pl.BlockSpec((1, MERGE_TM, D_MODEL), row),
                  pl.BlockSpec((1, MERGE_TM, D_MODEL), lambda b, s: (b, s, gate_blk)),
                  pl.BlockSpec((1, MERGE_TM, D_MODEL), lambda b, s: (b, s, gate_blk + 1)),
                  pl.BlockSpec((1, MERGE_TM, D_MODEL), lambda b, s: (b, s, gate_blk + 2)),
                  pl.BlockSpec((1, MERGE_TM, D_MODEL), row),
                  pl.BlockSpec((1, 1, 6 * D_MODEL), lambda b, s: (b, 0, 0)),
                  pl.BlockSpec((1, 1, 6 * D_MODEL), lambda b, s: (bsz, 0, 0)),
                  pl.BlockSpec((N_BRANCH, BRANCH_W, D_MODEL), lambda b, s: (0, 0, 0), pipeline_mode=const),
                  pl.BlockSpec((D_MODEL, D_MODEL), lambda b, s: (0, 0), pipeline_mode=const)],
        out_specs=pl.BlockSpec((1, MERGE_TM, D_MODEL), row),
        compiler_params=_cparams(("arbitrary", "arbitrary")),
        name="merge_branches",
    )(ym, yg, yr, big, big, big, xs, mod3, mod3, w_branch, w_out)


FFN_TM = 1152
FFN_TF = 256


def _swiglu_chunk(h, w1, w3, w2):
    a = jnp.dot(h, w1, preferred_element_type=F32)
    b = jnp.dot(h, w3, preferred_element_type=F32)
    return jnp.dot((a * _sigmoid(a) * b).astype(BF16), w2, preferred_element_type=F32)


def _ffn_kernel(x_ref, g_ref, ml_ref, mc_ref, w1_ref, w3_ref, w2_ref, o_ref, h_ref, acc_ref):
    f = pl.program_id(2)

    @pl.when(f == 0)
    def _():
        h = _ada_norm(x_ref[0], g_ref[...], ml_ref[0], mc_ref[0], pl.program_id(1) * FFN_TM, 1)
        h_ref[...] = h.astype(BF16)
        acc_ref[...] = jnp.zeros_like(acc_ref)

    acc_ref[...] += _swiglu_chunk(h_ref[...], w1_ref[...], w3_ref[...], w2_ref[...])

    @pl.when(f == pl.num_programs(2) - 1)
    def _():
        is_ctx = _is_ctx(FFN_TM, pl.program_id(1) * FFN_TM)
        gt = jnp.where(is_ctx, _mod_slice(mc_ref[0], 5), _mod_slice(ml_ref[0], 5))
        o_ref[0] = x_ref[0] + gt * acc_ref[...]


def dense_ffn(xs, g, mod, w1, w3, w2):
    bsz = xs.shape[0]
    mod3 = mod.reshape(bsz + 1, 1, 6 * D_MODEL)
    return pl.pallas_call(
        _ffn_kernel,
        out_shape=jax.ShapeDtypeStruct(xs.shape, F32),
        grid=(bsz, S_ALL // FFN_TM, D_FF // FFN_TF),
        in_specs=[pl.BlockSpec((1, FFN_TM, D_MODEL), lambda b, s, f: (b, s, 0)),
                  pl.BlockSpec((1, D_MODEL), lambda b, s, f: (0, 0)),
                  pl.BlockSpec((1, 1, 6 * D_MODEL), lambda b, s, f: (b, 0, 0)),
                  pl.BlockSpec((1, 1, 6 * D_MODEL), lambda b, s, f: (bsz, 0, 0)),
                  pl.BlockSpec((D_MODEL, FFN_TF), lambda b, s, f: (0, f)),
                  pl.BlockSpec((D_MODEL, FFN_TF), lambda b, s, f: (0, f)),
                  pl.BlockSpec((FFN_TF, D_MODEL), lambda b, s, f: (f, 0))],
        out_specs=pl.BlockSpec((1, FFN_TM, D_MODEL), lambda b, s, f: (b, s, 0)),
        scratch_shapes=[pltpu.VMEM((FFN_TM, D_MODEL), BF16), pltpu.VMEM((FFN_TM, D_MODEL), F32)],
        compiler_params=_cparams(("arbitrary", "arbitrary", "arbitrary")),
        name="dense_ffn",
    )(xs, g.reshape(1, D_MODEL), mod3, mod3, w1, w3, w2)


NORM_TM = 768


def _norm_router_kernel(x_ref, g_ref, ml_ref, mc_ref, rw_ref, rb_ref, h_ref, lg_ref):
    h = _ada_norm(x_ref[0], g_ref[...], ml_ref[0], mc_ref[0], pl.program_id(1) * NORM_TM, 1)
    h_ref[0] = h.astype(h_ref.dtype)
    lg_ref[0] = jnp.dot(h, rw_ref[...], preferred_element_type=F32, precision=lax.Precision.HIGHEST) + rb_ref[...]


def norm_router(xs, g, mod, router_w, router_b):
    bsz = xs.shape[0]
    mod3 = mod.reshape(bsz + 1, 1, 6 * D_MODEL)
    rw = jnp.zeros((D_MODEL, LANE), F32).at[:, :N_EXPERTS].set(router_w)
    rb = jnp.zeros((1, LANE), F32).at[0, :N_EXPERTS].set(router_b)
    return pl.pallas_call(
        _norm_router_kernel,
        out_shape=(jax.ShapeDtypeStruct(xs.shape, BF16), jax.ShapeDtypeStruct((bsz, S_ALL, LANE), F32)),
        grid=(bsz, S_ALL // NORM_TM),
        in_specs=[pl.BlockSpec((1, NORM_TM, D_MODEL), lambda b, s: (b, s, 0)),
                  pl.BlockSpec((1, D_MODEL), lambda b, s: (0, 0)),
                  pl.BlockSpec((1, 1, 6 * D_MODEL), lambda b, s: (b, 0, 0)),
                  pl.BlockSpec((1, 1, 6 * D_MODEL), lambda b, s: (bsz, 0, 0)),
                  pl.BlockSpec((D_MODEL, LANE), lambda b, s: (0, 0)),
                  pl.BlockSpec((1, LANE), lambda b, s: (0, 0))],
        out_specs=(pl.BlockSpec((1, NORM_TM, D_MODEL), lambda b, s: (b, s, 0)),
                   pl.BlockSpec((1, NORM_TM, LANE), lambda b, s: (b, s, 0))),
        compiler_params=_cparams(("arbitrary", "arbitrary")),
        name="norm_router",
    )(xs, g.reshape(1, D_MODEL), mod3, mod3, rw, rb)


def _expert_kernel(be_ref, x_ref, w1_ref, w3_ref, w2_ref, o_ref):
    del be_ref
    x = x_ref[...]
    acc = None
    for f in range(D_FF // FFN_TF):
        cols = slice(f * FFN_TF, (f + 1) * FFN_TF)
        y = _swiglu_chunk(x, w1_ref[0, :, cols], w3_ref[0, :, cols], w2_ref[0, cols, :])
        acc = y if acc is None else acc + y
    o_ref[...] = acc


def grouped_experts(xb, block_e, w1, w3, w2):
    n_rows = xb.shape[0]
    const = pl.Buffered(1)
    grid_spec = pltpu.PrefetchScalarGridSpec(
        num_scalar_prefetch=1,
        grid=(n_rows // MOE_BLOCK,),
        in_specs=[pl.BlockSpec((MOE_BLOCK, D_MODEL), lambda i, be: (i, 0)),
                  pl.BlockSpec((1, D_MODEL, D_FF), lambda i, be: (be[i], 0, 0), pipeline_mode=const),
                  pl.BlockSpec((1, D_MODEL, D_FF), lambda i, be: (be[i], 0, 0), pipeline_mode=const),
                  pl.BlockSpec((1, D_FF, D_MODEL), lambda i, be: (be[i], 0, 0), pipeline_mode=const)],
        out_specs=pl.BlockSpec((MOE_BLOCK, D_MODEL), lambda i, be: (i, 0)),
    )
    return pl.pallas_call(
        _expert_kernel,
        out_shape=jax.ShapeDtypeStruct((n_rows, D_MODEL), F32),
        grid_spec=grid_spec,
        compiler_params=_cparams(("arbitrary",)),
        name="grouped_experts",
    )(block_e, xb, w1, w3, w2)


def moe_ffn(xs, g, mod, gt2, router_w, router_b, w1, w3, w2):
    bsz = xs.shape[0]
    h, logits = norm_router(xs, g, mod, router_w, router_b)
    h = h.reshape(-1, D_MODEL)
    n_tok = h.shape[0]
    top_logit, top_e = lax.top_k(logits.reshape(n_tok, LANE)[:, :N_EXPERTS], TOP_K)
    gate = jax.nn.softmax(top_logit, axis=-1)
    n_assign = n_tok * TOP_K
    flat_e = top_e.reshape(-1)
    flat_tok = jnp.repeat(jnp.arange(n_tok, dtype=jnp.int32), TOP_K)
    flat_gate = gate.reshape(-1)
    order = jnp.argsort(flat_e)
    sorted_e = flat_e[order]
    counts = jnp.bincount(flat_e, length=N_EXPERTS)
    padded = (counts + MOE_BLOCK - 1) // MOE_BLOCK * MOE_BLOCK
    start = jnp.cumsum(counts) - counts
    p_end = jnp.cumsum(padded)
    p_start = p_end - padded
    dest = p_start[sorted_e] + jnp.arange(n_assign, dtype=jnp.int32) - start[sorted_e]
    n_blocks = -(-n_assign // MOE_BLOCK) + N_EXPERTS
    n_rows = n_blocks * MOE_BLOCK
    row_tok = jnp.zeros((n_rows,), jnp.int32).at[dest].set(flat_tok[order])
    row_gate = jnp.zeros((n_rows,), F32).at[dest].set(flat_gate[order])
    block_e = jnp.minimum(jnp.searchsorted(p_end, jnp.arange(n_blocks) * MOE_BLOCK, side='right'),
                          N_EXPERTS - 1).astype(jnp.int32)
    yb = grouped_experts(h[row_tok], block_e, w1, w3, w2)
    y = jnp.zeros((n_tok, D_MODEL), F32).at[row_tok].add(yb * row_gate[:, None])
    return xs + gt2 * y.reshape(bsz, S_ALL, D_MODEL)


FINAL_TM = 1024


def _final_kernel(x_ref, g_ref, o_ref):
    x = x_ref[0]
    o_ref[0] = x * lax.rsqrt(jnp.mean(x * x, axis=-1, keepdims=True) + EPS) * g_ref[...]


def final_norm(xs, g):
    bsz = xs.shape[0]
    skip = CTX_LEN // 256
    return pl.pallas_call(
        _final_kernel,
        out_shape=jax.ShapeDtypeStruct((bsz, SEQ, D_MODEL), F32),
        grid=(bsz, SEQ // 256),
        in_specs=[pl.BlockSpec((1, 256, D_MODEL), lambda b, s: (b, s + skip, 0)),
                  pl.BlockSpec((1, D_MODEL), lambda b, s: (0, 0))],
        out_specs=pl.BlockSpec((1, 256, D_MODEL), lambda b, s: (b, s, 0)),
        compiler_params=_cparams(("arbitrary", "arbitrary")),
        name="final_norm",
    )(xs, g.reshape(1, D_MODEL))


def _rmsnorm(x, g):
    xf = x.astype(F32)
    y = xf * lax.rsqrt(jnp.mean(xf * xf, axis=-1, keepdims=True) + EPS)
    return (y * g.astype(F32)).astype(x.dtype)


def _l2norm(x):
    return x * lax.rsqrt(jnp.sum(x * x, axis=-1, keepdims=True) + EPS)


def _dwconv(x, w):
    pad_l = CONV_W // 2
    return lax.conv_general_dilated(x, w[:, None, :].astype(x.dtype), window_strides=(1,),
                                    padding=[(pad_l, CONV_W - 1 - pad_l)],
                                    dimension_numbers=('NWC', 'WIO', 'NWC'),
                                    feature_group_count=x.shape[-1])


def _to_col_major(a, rows):
    b = a.shape[0]
    return a.reshape((b, rows, GRID_W) + a.shape[2:]).swapaxes(1, 2).reshape(a.shape)


def _to_row_major(a, rows):
    b = a.shape[0]
    return a.reshape((b, GRID_W, rows) + a.shape[2:]).swapaxes(1, 2).reshape(a.shape)


def _to_chunks(a):
    b, t = a.shape[:2]
    return jnp.moveaxis(a.reshape((b, t // CHUNK, CHUNK) + a.shape[2:]), 3, 1)


def _from_chunks(a):
    a = jnp.moveaxis(a, 1, 3)
    return a.reshape((a.shape[0], a.shape[1] * a.shape[2]) + a.shape[3:])


def _bidirectional_prefix(run, ctx_dirs, lat_dirs, zero_state):
    outs_c, outs_l = [], []
    for d in range(2):
        rev = (lambda a: jnp.flip(a, axis=1)) if d == 1 else (lambda a: a)
        out_c, state = run(tuple(rev(a) for a in ctx_dirs[d]), zero_state)
        out_l, _ = run(tuple(rev(a) for a in lat_dirs[d]), state)
        outs_c.append(rev(out_c))
        outs_l.append(rev(out_l))
    return outs_c[0] + outs_c[1], outs_l[0] + outs_l[1]


def _mlstm_run(args, state):
    q, k, v, ig, lf = (_to_chunks(a) for a in args)
    b = jnp.cumsum(lf, axis=-1)
    b_end = b[..., -1]
    a_end = b_end[..., None] - b + ig
    m_loc = jnp.max(a_end, axis=-1)
    w_end = jnp.exp(a_end - m_loc[..., None])
    c_loc = jnp.einsum('bhnsk,bhnsv->bhnkv', k * w_end[..., None], v)
    n_loc = jnp.einsum('bhnsk,bhns->bhnk', k, w_end)

    def step(carry, xs):
        c_prev, n_prev, m_prev = carry
        c_l, n_l, m_l, b_e = xs
        m_new = jnp.maximum(b_e + m_prev, m_l)
        s_prev = jnp.exp(b_e + m_prev - m_new)
        s_loc = jnp.exp(m_l - m_new)
        c_new = s_prev[..., None, None] * c_prev + s_loc[..., None, None] * c_l
        n_new = s_prev[..., None] * n_prev + s_loc[..., None] * n_l
        return (c_new, n_new, m_new), carry

    xs = tuple(jnp.moveaxis(a, 2, 0) for a in (c_loc, n_loc, m_loc, b_end))
    final, starts = lax.scan(step, state, xs)
    c0, n0, m0 = (jnp.moveaxis(a, 0, 2) for a in starts)
    causal = jnp.tril(jnp.ones((CHUNK, CHUNK), bool))
    d_log = jnp.where(causal, b[..., :, None] - b[..., None, :] + ig[..., None, :], -jnp.inf)
    m_inter = b + m0[..., None]
    m_comb = jnp.maximum(m_inter, jnp.max(d_log, axis=-1))
    s = jnp.einsum('bhntk,bhnsk->bhnts', q, k) * jnp.exp(d_log - m_comb[..., None])
    e_inter = jnp.exp(m_inter - m_comb)
    num = jnp.einsum('bhnts,bhnsv->bhntv', s, v) + e_inter[..., None] * jnp.einsum('bhntk,bhnkv->bhntv', q, c0)
    den = jnp.sum(s, axis=-1) + e_inter * jnp.einsum('bhntk,bhnk->bhnt', q, n0)
    h = num / jnp.maximum(jnp.abs(den), jnp.exp(-m_comb))[..., None]
    return _from_chunks(h), final


def _gdn_run(args, state):
    q, k, v, beta, g = (_to_chunks(a) for a in args)
    gam = jnp.cumsum(g, axis=-1)
    incl = jnp.tril(jnp.ones((CHUNK, CHUNK), bool))
    strict = jnp.tril(jnp.ones((CHUNK, CHUNK), bool), -1)
    decay = jnp.exp(jnp.where(incl, gam[..., :, None] - gam[..., None, :], -jnp.inf))
    kb = k * beta[..., None]
    a_low = jnp.where(strict, jnp.einsum('bhntk,bhnsk->bhnts', kb, k) * decay, 0.0)
    rhs = jnp.concatenate([v * beta[..., None], kb * jnp.exp(gam)[..., None]], axis=-1)
    sol = lax.linalg.triangular_solve(a_low + jnp.eye(CHUNK, dtype=a_low.dtype), rhs,
                                      left_side=True, lower=True)
    dv = v.shape[-1]
    u, w = sol[..., :dv], sol[..., dv:]
    attn = jnp.einsum('bhntk,bhnsk->bhnts', q, k) * decay
    q_dec = q * jnp.exp(gam)[..., None]
    k_tail = k * jnp.exp(gam[..., -1:] - gam)[..., None]
    chunk_dec = jnp.exp(gam[..., -1])

    def step(s_mat, xs):
        q_c, u_c, w_c, a_c, kt_c, cd = xs
        v_new = u_c - jnp.einsum('bhck,bhkv->bhcv', w_c, s_mat)
        o = jnp.einsum('bhck,bhkv->bhcv', q_c, s_mat) + jnp.einsum('bhcs,bhsv->bhcv', a_c, v_new)
        s_mat = cd[..., None, None] * s_mat + jnp.einsum('bhck,bhcv->bhkv', kt_c, v_new)
        return s_mat, o

    xs = tuple(jnp.moveaxis(a, 2, 0) for a in (q_dec, u, w, attn, k_tail, chunk_dec))
    final, o = lax.scan(step, state, xs)
    return _from_chunks(jnp.moveaxis(o, 0, 2)), final


def _lru_run(args, h0):
    log_a, u = args
    a = jnp.exp(log_a)
    u = u.at[:, 0].add(a[:, 0] * h0)

    def combine(left, right):
        a_l, b_l = left
        a_r, b_r = right
        return a_l * a_r, a_r * b_l + b_r

    _, h = lax.associative_scan(combine, (a, u), axis=1)
    return h, h[:, -1]


def _mlstm_inputs(q, k, v, gates, f_bias):
    bsz, t = q.shape[:2]
    q = q.reshape(bsz, t, M_HEADS, M_DQK).astype(F32)
    k = k.reshape(bsz, t, M_HEADS, M_DQK).astype(F32) * (M_DQK ** -0.5)
    v = v.reshape(bsz, t, M_HEADS, M_DV).astype(F32)
    gates = gates.reshape(bsz, t, 2, 2, M_HEADS).astype(F32)
    ig = gates[:, :, :, 0]
    lf = jax.nn.log_sigmoid(gates[:, :, :, 1] + f_bias.astype(F32))
    return tuple((q, k, v, ig[:, :, d], lf[:, :, d]) for d in range(2))


def _gdn_inputs(qkv, beta_pre, alpha_pre, conv_w, a_log, dt_bias):
    bsz, t = qkv.shape[:2]
    qkv = jax.nn.silu(_dwconv(qkv, conv_w)).astype(F32)
    q, k, v = jnp.split(qkv, [G_HEADS * G_DK, 2 * G_HEADS * G_DK], axis=-1)
    q = _l2norm(q.reshape(bsz, t, G_HEADS, G_DK)) * (G_DK ** -0.5)
    k = _l2norm(k.reshape(bsz, t, G_HEADS, G_DK))
    v = v.reshape(bsz, t, G_HEADS, G_DV)
    beta = jax.nn.sigmoid(beta_pre.reshape(bsz, t, 2, G_HEADS).astype(F32))
    g = -jnp.exp(a_log.astype(F32)) * jax.nn.softplus(
        alpha_pre.reshape(bsz, t, 2, G_HEADS).astype(F32) + dt_bias.astype(F32))
    return tuple((q, k, v, beta[:, :, d], g[:, :, d]) for d in range(2))


def _lru_inputs(xr, conv_w, conv_b, gate_w, gate_b, lam):
    bsz, t = xr.shape[:2]
    xb = (_dwconv(xr, conv_w) + conv_b).astype(F32)
    pre = jnp.einsum('btnc,dgncm->btdgnm', xb.reshape(bsz, t, R_BLOCKS, R_BDIM), gate_w.astype(F32))
    pre = pre.reshape(bsz, t, 2, 2, BRANCH_W) + gate_b.astype(F32)
    r = jax.nn.sigmoid(pre[:, :, :, 0])
    i = jax.nn.sigmoid(pre[:, :, :, 1])
    log_a = -LRU_C * jax.nn.softplus(-lam.astype(F32)) * r
    u = jnp.sqrt(-jnp.expm1(2.0 * log_a)) * i * xb[:, :, None]
    return tuple((log_a[:, :, d], u[:, :, d]) for d in range(2))


def mixers_jax(big, small, m_fbias, m_norm_g, g_conv, g_a_log, g_dt_bias, g_norm_g,
               r_conv, r_conv_b, r_gate_w, r_gate_b, r_lambda):
    bsz = big.shape[0]
    rows = SEQ // GRID_W
    big = big.astype(F32)

    def seg(c0, width):
        a = big[:, :, c0:c0 + width]
        return a[:, :CTX_LEN], a[:, CTX_LEN:]

    def sseg(c0):
        a = small[:, :, c0:c0 + 16]
        return a[:, :CTX_LEN], a[:, CTX_LEN:]

    mq, mk, mv, mo = seg(C_MQ, 512), seg(C_MK, 512), seg(C_MV, 1024), seg(C_MO, 1024)
    gqkv, gz = seg(C_GQ, 3072), seg(C_GZ, 1024)
    rx, ry = seg(C_RX, 1024), seg(C_RY, 1024)
    mg, gb, ga = sseg(0), sseg(16), sseg(32)
    m_zero = (jnp.zeros((bsz, M_HEADS, M_DQK, M_DV), F32), jnp.zeros((bsz, M_HEADS, M_DQK), F32),
              jnp.zeros((bsz, M_HEADS), F32))
    m_c, m_l = _bidirectional_prefix(_mlstm_run, _mlstm_inputs(mq[0], mk[0], mv[0], mg[0], m_fbias),
                                     _mlstm_inputs(mq[1], mk[1], mv[1], mg[1], m_fbias), m_zero)
    g_zero = jnp.zeros((bsz, G_HEADS, G_DK, G_DV), F32)
    d_c, d_l = _bidirectional_prefix(_gdn_run, _gdn_inputs(gqkv[0], gb[0], ga[0], g_conv, g_a_log, g_dt_bias),
                                     _gdn_inputs(gqkv[1], gb[1], ga[1], g_conv, g_a_log, g_dt_bias), g_zero)
    r_zero = jnp.zeros((bsz, BRANCH_W), F32)
    r_c, r_l = _bidirectional_prefix(
        _lru_run, _lru_inputs(rx[0], r_conv, r_conv_b, r_gate_w, r_gate_b, r_lambda),
        _lru_inputs(_to_col_major(rx[1], rows), r_conv, r_conv_b, r_gate_w, r_gate_b, r_lambda), r_zero)
    r_l = _to_row_major(r_l, rows)

    def m_fin(h, o):
        t = o.shape[1]
        h = _rmsnorm(h, m_norm_g.reshape(M_HEADS, M_DV)).reshape(bsz, t, BRANCH_W)
        return h * jax.nn.sigmoid(o)

    def g_fin(h, z):
        t = z.shape[1]
        return _rmsnorm(h, g_norm_g).reshape(bsz, t, BRANCH_W) * jax.nn.silu(z)

    ym = jnp.concatenate([m_fin(m_c, mo[0]), m_fin(m_l, mo[1])], axis=1)
    yg = jnp.concatenate([g_fin(d_c, gz[0]), g_fin(d_l, gz[1])], axis=1)
    yr = jnp.concatenate([r_c * jax.nn.gelu(ry[0]), r_l * jax.nn.gelu(ry[1])], axis=1)
    return ym, yg, yr


def _pack_in_weights(w, b):
    segs_w = [w[:, IN_OFFS[i]:IN_OFFS[i + 1]] for i in range(len(IN_COLS))]
    segs_b = [b[IN_OFFS[i]:IN_OFFS[i + 1]] for i in range(len(IN_COLS))]
    w_big = jnp.concatenate([segs_w[i] for i in BIG_SEGS], axis=1).astype(BF16)
    b_big = jnp.concatenate([segs_b[i] for i in BIG_SEGS]).reshape(1, N_BIG)
    pad = N_SMALL - sum(IN_COLS[i] for i in SMALL_SEGS)
    w_small = jnp.pad(jnp.concatenate([segs_w[i] for i in SMALL_SEGS], axis=1), ((0, 0), (0, pad))).astype(BF16)
    b_small = jnp.pad(jnp.concatenate([segs_b[i] for i in SMALL_SEGS]), (0, pad)).reshape(1, N_SMALL)
    return w_big, b_big, w_small, b_small


def kernel(x, c, ctx, c_ctx, mod_w, mod_b, norm1_g, norm2_g, final_g, w_in, b_in, m_fbias, m_norm_g,
           g_conv, g_a_log, g_dt_bias, g_norm_g, r_conv, r_conv_b, r_gate_w, r_gate_b, r_lambda,
           w_branch, w_out, ffn_w1, ffn_w3, ffn_w2, router_w, router_b, moe_w1, moe_w3, moe_w2):
    bsz = x.shape[0]
    xs = jnp.concatenate([ctx, x], axis=1)
    src = jnp.concatenate([c, c_ctx[None]], axis=0)
    mods = modulation(src, mod_w, mod_b)
    for layer in range(DEPTH):
        mod = mods[layer]
        w_big, b_big, w_small, b_small = _pack_in_weights(w_in[layer], b_in[layer])
        big, small = in_projection(xs, norm1_g[layer], mod, w_big, b_big, w_small, b_small)
        ym, yg, yr = mixers_jax(big, small, m_fbias[layer], m_norm_g[layer], g_conv[layer], g_a_log[layer],
                                g_dt_bias[layer], g_norm_g[layer], r_conv[layer], r_conv_b[layer],
                                r_gate_w[layer], r_gate_b[layer], r_lambda[layer])
        xs = merge_branches(ym, yg, yr, big, xs, mod, w_branch[layer].astype(BF16), w_out[layer].astype(BF16))
        j = layer // 2
        if layer % 2 == 0:
            xs = dense_ffn(xs, norm2_g[layer], mod, ffn_w1[j].astype(BF16), ffn_w3[j].astype(BF16),
                           ffn_w2[j].astype(BF16))
        else:
            gt2 = jnp.concatenate([jnp.broadcast_to(mod[bsz:, None, 5 * D_MODEL:], (bsz, CTX_LEN, D_MODEL)),
                                   jnp.broadcast_to(mod[:bsz, None, 5 * D_MODEL:], (bsz, SEQ, D_MODEL))], axis=1)
            xs = moe_ffn(xs, norm2_g[layer], mod, gt2, router_w[j], router_b[j], moe_w1[j].astype(BF16),
                         moe_w3[j].astype(BF16), moe_w2[j].astype(BF16))
    return final_norm(xs, final_g)
```

```python
import jax
import jax.numpy as jnp
import numpy as np
from jax import lax
from jax.experimental import pallas as pl
from jax.experimental.pallas import tpu as pltpu

D_MODEL = 1024
SEQ = 2048
DEPTH = 4
GRID_W = 64
CTX_LEN = 256
S_ALL = CTX_LEN + SEQ
BRANCH_W = D_MODEL
N_BRANCH = 3
CHUNK = 64
CONV_W = 4
M_HEADS = 4
M_DQK = 128
M_DV = BRANCH_W // M_HEADS
G_HEADS = 8
G_DK = 128
G_DV = BRANCH_W // G_HEADS
R_BLOCKS = 8
R_BDIM = BRANCH_W // R_BLOCKS
LRU_C = 8.0
D_FF = 2816
N_EXPERTS = 8
TOP_K = 2
MOE_BLOCK = 128
EPS = 1e-6
IN_COLS = (M_HEADS * M_DQK, M_HEADS * M_DQK, BRANCH_W, BRANCH_W, 4 * M_HEADS,
           2 * G_HEADS * G_DK + BRANCH_W, BRANCH_W, 2 * G_HEADS, 2 * G_HEADS,
           BRANCH_W, BRANCH_W, N_BRANCH * D_MODEL)
IN_OFFS = tuple(int(v) for v in np.cumsum((0,) + IN_COLS))
SMALL_SEGS = (4, 7, 8)
BIG_SEGS = tuple(i for i in range(len(IN_COLS)) if i not in SMALL_SEGS)
N_BIG = sum(IN_COLS[i] for i in BIG_SEGS)
N_SMALL = 128
C_MQ, C_MK, C_MV, C_MO = 0, 512, 1024, 2048
C_GQ, C_GK, C_GV, C_GZ = 3072, 4096, 5120, 6144
C_RX, C_RY, C_MG = 7168, 8192, 9216

LANE = 128
VMEM_LIMIT = 56 * 1024 * 1024
F32 = jnp.float32
BF16 = jnp.bfloat16


def _cparams(sem):
    return pltpu.CompilerParams(dimension_semantics=sem, vmem_limit_bytes=VMEM_LIMIT)


def _sigmoid(v):
    return 1.0 / (1.0 + jnp.exp(-v))


def _ada_norm(x, g, mod_l, mod_c, first_row, which):
    y = x * lax.rsqrt(jnp.mean(x * x, axis=-1, keepdims=True) + EPS) * g
    sh_l, sc_l = _mod_slice(mod_l, 3 * which), _mod_slice(mod_l, 3 * which + 1)
    sh_c, sc_c = _mod_slice(mod_c, 3 * which), _mod_slice(mod_c, 3 * which + 1)
    is_ctx = _is_ctx(x.shape[0], first_row)
    return y * (1.0 + jnp.where(is_ctx, sc_c, sc_l)) + jnp.where(is_ctx, sh_c, sh_l)


def _mod_slice(mod, k):
    return mod[:, k * D_MODEL:(k + 1) * D_MODEL]


def _is_ctx(rows, first_row):
    return (lax.broadcasted_iota(jnp.int32, (rows, 1), 0) + first_row) < CTX_LEN


def _mod_kernel(src_ref, w_ref, b_ref, o_ref):
    s = src_ref[...]
    s = (s * _sigmoid(s)).astype(BF16)
    o_ref[0] = jnp.dot(s, w_ref[0].astype(BF16), preferred_element_type=F32) + b_ref[0]


def modulation(src, mod_w, mod_b):
    rows = src.shape[0]
    tn = 1536
    return pl.pallas_call(
        _mod_kernel,
        out_shape=jax.ShapeDtypeStruct((DEPTH, rows, 6 * D_MODEL), F32),
        grid=(DEPTH, 6 * D_MODEL // tn),
        in_specs=[pl.BlockSpec((rows, D_MODEL), lambda l, n: (0, 0)),
                  pl.BlockSpec((1, D_MODEL, tn), lambda l, n: (l, 0, n)),
                  pl.BlockSpec((1, 1, tn), lambda l, n: (l, 0, n))],
        out_specs=pl.BlockSpec((1, rows, tn), lambda l, n: (l, 0, n)),
        compiler_params=_cparams(("arbitrary", "arbitrary")),
        name="modulation",
    )(src, mod_w, mod_b.reshape(DEPTH, 1, 6 * D_MODEL))


IN_TM = 1152
IN_TN = 512


def _in_kernel(x_ref, g_ref, ml_ref, mc_ref, w_ref, b_ref, ws_ref, bs_ref, o_ref, os_ref, h_ref):
    n = pl.program_id(2)

    @pl.when(n == 0)
    def _():
        h = _ada_norm(x_ref[0], g_ref[...], ml_ref[0], mc_ref[0], pl.program_id(1) * IN_TM, 0)
        h_ref[...] = h.astype(BF16)
        os_ref[0] = jnp.dot(h_ref[...], ws_ref[...], preferred_element_type=F32) + bs_ref[...]

    o_ref[0] = (jnp.dot(h_ref[...], w_ref[...], preferred_element_type=F32) + b_ref[...]).astype(o_ref.dtype)


def in_projection(xs, g, mod, w_big, b_big, w_small, b_small):
    bsz = xs.shape[0]
    mod3 = mod.reshape(bsz + 1, 1, 6 * D_MODEL)
    return pl.pallas_call(
        _in_kernel,
        out_shape=(jax.ShapeDtypeStruct((bsz, S_ALL, N_BIG), BF16),
                   jax.ShapeDtypeStruct((bsz, S_ALL, N_SMALL), F32)),
        grid=(bsz, S_ALL // IN_TM, N_BIG // IN_TN),
        in_specs=[pl.BlockSpec((1, IN_TM, D_MODEL), lambda b, s, n: (b, s, 0)),
                  pl.BlockSpec((1, D_MODEL), lambda b, s, n: (0, 0)),
                  pl.BlockSpec((1, 1, 6 * D_MODEL), lambda b, s, n: (b, 0, 0)),
                  pl.BlockSpec((1, 1, 6 * D_MODEL), lambda b, s, n: (bsz, 0, 0)),
                  pl.BlockSpec((D_MODEL, IN_TN), lambda b, s, n: (0, n)),
                  pl.BlockSpec((1, IN_TN), lambda b, s, n: (0, n)),
                  pl.BlockSpec((D_MODEL, N_SMALL), lambda b, s, n: (0, 0)),
                  pl.BlockSpec((1, N_SMALL), lambda b, s, n: (0, 0))],
        out_specs=(pl.BlockSpec((1, IN_TM, IN_TN), lambda b, s, n: (b, s, n)),
                   pl.BlockSpec((1, IN_TM, N_SMALL), lambda b, s, n: (b, s, 0))),
        scratch_shapes=[pltpu.VMEM((IN_TM, D_MODEL), BF16)],
        compiler_params=_cparams(("arbitrary", "arbitrary", "arbitrary")),
        name="in_projection",
    )(xs, g.reshape(1, D_MODEL), mod3, mod3, w_big, b_big, w_small, b_small)


MERGE_TM = 768


def _merge_kernel(ym_ref, yg_ref, yr_ref, g0_ref, g1_ref, g2_ref, x_ref, ml_ref, mc_ref, wb_ref, wo_ref, o_ref):
    acc = None
    for n, (y_ref, gp_ref) in enumerate(((ym_ref, g0_ref), (yg_ref, g1_ref), (yr_ref, g2_ref))):
        p = jnp.dot(y_ref[0].astype(BF16), wb_ref[n], preferred_element_type=F32)
        p = p * _sigmoid(gp_ref[0].astype(F32))
        acc = p if acc is None else acc + p
    out = jnp.dot(acc.astype(BF16), wo_ref[...], preferred_element_type=F32)
    is_ctx = _is_ctx(MERGE_TM, pl.program_id(1) * MERGE_TM)
    gt = jnp.where(is_ctx, _mod_slice(mc_ref[0], 2), _mod_slice(ml_ref[0], 2))
    o_ref[0] = x_ref[0] + gt * out


def merge_branches(ym, yg, yr, big, xs, mod, w_branch, w_out):
    bsz = xs.shape[0]
    mod3 = mod.reshape(bsz + 1, 1, 6 * D_MODEL)
    row = lambda b, s: (b, s, 0)
    gate_blk = C_MG // D_MODEL
    const = pl.Buffered(1)
    return pl.pallas_call(
        _merge_kernel,
        out_shape=jax.ShapeDtypeStruct(xs.shape, F32),
        grid=(bsz, S_ALL // MERGE_TM),
        in_specs=[pl.BlockSpec((1, MERGE_TM, D_MODEL), row),
                  pl.BlockSpec((1, MERGE_TM, D_MODEL), row),
                  pl.BlockSpec((1, MERGE_TM, D_MODEL), row),
                  pl.BlockSpec((1, MERGE_TM, D_MODEL), lambda b, s: (b, s, gate_blk)),
                  pl.BlockSpec((1, MERGE_TM, D_MODEL), lambda b, s: (b, s, gate_blk + 1)),
                  pl.BlockSpec((1, MERGE_TM, D_MODEL), lambda b, s: (b, s, gate_blk + 2)),
                  pl.BlockSpec((1, MERGE_TM, D_MODEL), row),
                  pl.BlockSpec((1, 1, 6 * D_MODEL), lambda b, s: (b, 0, 0)),
                  pl.BlockSpec((1, 1, 6 * D_MODEL), lambda b, s: (bsz, 0, 0)),
                  pl.BlockSpec((N_BRANCH, BRANCH_W, D_MODEL), lambda b, s: (0, 0, 0), pipeline_mode=const),
                  pl.BlockSpec((D_MODEL, D_MODEL), lambda b, s: (0, 0), pipeline_mode=const)],
        out_specs=pl.BlockSpec((1, MERGE_TM, D_MODEL), row),
        compiler_params=_cparams(("arbitrary", "arbitrary")),
        name="merge_branches",
    )(ym, yg, yr, big, big, big, xs, mod3, mod3, w_branch, w_out)


FFN_TM = 1152
FFN_TF = 256


def _swiglu_chunk(h, w1, w3, w2):
    a = jnp.dot(h, w1, preferred_element_type=F32)
    b = jnp.dot(h, w3, preferred_element_type=F32)
    return jnp.dot((a * _sigmoid(a) * b).astype(BF16), w2, preferred_element_type=F32)


def _ffn_kernel(x_ref, g_ref, ml_ref, mc_ref, w1_ref, w3_ref, w2_ref, o_ref, h_ref, acc_ref):
    f = pl.program_id(2)

    @pl.when(f == 0)
    def _():
        h = _ada_norm(x_ref[0], g_ref[...], ml_ref[0], mc_ref[0], pl.program_id(1) * FFN_TM, 1)
        h_ref[...] = h.astype(BF16)
        acc_ref[...] = jnp.zeros_like(acc_ref)

    acc_ref[...] += _swiglu_chunk(h_ref[...], w1_ref[...], w3_ref[...], w2_ref[...])

    @pl.when(f == pl.num_programs(2) - 1)
    def _():
        is_ctx = _is_ctx(FFN_TM, pl.program_id(1) * FFN_TM)
        gt = jnp.where(is_ctx, _mod_slice(mc_ref[0], 5), _mod_slice(ml_ref[0], 5))
        o_ref[0] = x_ref[0] + gt * acc_ref[...]


def dense_ffn(xs, g, mod, w1, w3, w2):
    bsz = xs.shape[0]
    mod3 = mod.reshape(bsz + 1, 1, 6 * D_MODEL)
    return pl.pallas_call(
        _ffn_kernel,
        out_shape=jax.ShapeDtypeStruct(xs.shape, F32),
        grid=(bsz, S_ALL // FFN_TM, D_FF // FFN_TF),
        in_specs=[pl.BlockSpec((1, FFN_TM, D_MODEL), lambda b, s, f: (b, s, 0)),
                  pl.BlockSpec((1, D_MODEL), lambda b, s, f: (0, 0)),
                  pl.BlockSpec((1, 1, 6 * D_MODEL), lambda b, s, f: (b, 0, 0)),
                  pl.BlockSpec((1, 1, 6 * D_MODEL), lambda b, s, f: (bsz, 0, 0)),
                  pl.BlockSpec((D_MODEL, FFN_TF), lambda b, s, f: (0, f)),
                  pl.BlockSpec((D_MODEL, FFN_TF), lambda b, s, f: (0, f)),
                  pl.BlockSpec((FFN_TF, D_MODEL), lambda b, s, f: (f, 0))],
        out_specs=pl.BlockSpec((1, FFN_TM, D_MODEL), lambda b, s, f: (b, s, 0)),
        scratch_shapes=[pltpu.VMEM((FFN_TM, D_MODEL), BF16), pltpu.VMEM((FFN_TM, D_MODEL), F32)],
        compiler_params=_cparams(("arbitrary", "arbitrary", "arbitrary")),
        name="dense_ffn",
    )(xs, g.reshape(1, D_MODEL), mod3, mod3, w1, w3, w2)


NORM_TM = 768


def _norm_router_kernel(x_ref, g_ref, ml_ref, mc_ref, rw_ref, rb_ref, h_ref, lg_ref):
    h = _ada_norm(x_ref[0], g_ref[...], ml_ref[0], mc_ref[0], pl.program_id(1) * NORM_TM, 1)
    h_ref[0] = h.astype(h_ref.dtype)
    lg_ref[0] = jnp.dot(h, rw_ref[...], preferred_element_type=F32, precision=lax.Precision.HIGHEST) + rb_ref[...]


def norm_router(xs, g, mod, router_w, router_b):
    bsz = xs.shape[0]
    mod3 = mod.reshape(bsz + 1, 1, 6 * D_MODEL)
    rw = jnp.zeros((D_MODEL, LANE), F32).at[:, :N_EXPERTS].set(router_w)
    rb = jnp.zeros((1, LANE), F32).at[0, :N_EXPERTS].set(router_b)
    return pl.pallas_call(
        _norm_router_kernel,
        out_shape=(jax.ShapeDtypeStruct(xs.shape, BF16), jax.ShapeDtypeStruct((bsz, S_ALL, LANE), F32)),
        grid=(bsz, S_ALL // NORM_TM),
        in_specs=[pl.BlockSpec((1, NORM_TM, D_MODEL), lambda b, s: (b, s, 0)),
                  pl.BlockSpec((1, D_MODEL), lambda b, s: (0, 0)),
                  pl.BlockSpec((1, 1, 6 * D_MODEL), lambda b, s: (b, 0, 0)),
                  pl.BlockSpec((1, 1, 6 * D_MODEL), lambda b, s: (bsz, 0, 0)),
                  pl.BlockSpec((D_MODEL, LANE), lambda b, s: (0, 0)),
                  pl.BlockSpec((1, LANE), lambda b, s: (0, 0))],
        out_specs=(pl.BlockSpec((1, NORM_TM, D_MODEL), lambda b, s: (b, s, 0)),
                   pl.BlockSpec((1, NORM_TM, LANE), lambda b, s: (b, s, 0))),
        compiler_params=_cparams(("arbitrary", "arbitrary")),
        name="norm_router",
    )(xs, g.reshape(1, D_MODEL), mod3, mod3, rw, rb)


def _expert_kernel(be_ref, x_ref, w1_ref, w3_ref, w2_ref, o_ref):
    del be_ref
    x = x_ref[...]
    acc = None
    for f in range(D_FF // FFN_TF):
        cols = slice(f * FFN_TF, (f + 1) * FFN_TF)
        y = _swiglu_chunk(x, w1_ref[0, :, cols], w3_ref[0, :, cols], w2_ref[0, cols, :])
        acc = y if acc is None else acc + y
    o_ref[...] = acc


def grouped_experts(xb, block_e, w1, w3, w2):
    n_rows = xb.shape[0]
    const = pl.Buffered(1)
    grid_spec = pltpu.PrefetchScalarGridSpec(
        num_scalar_prefetch=1,
        grid=(n_rows // MOE_BLOCK,),
        in_specs=[pl.BlockSpec((MOE_BLOCK, D_MODEL), lambda i, be: (i, 0)),
                  pl.BlockSpec((1, D_MODEL, D_FF), lambda i, be: (be[i], 0, 0), pipeline_mode=const),
                  pl.BlockSpec((1, D_MODEL, D_FF), lambda i, be: (be[i], 0, 0), pipeline_mode=const),
                  pl.BlockSpec((1, D_FF, D_MODEL), lambda i, be: (be[i], 0, 0), pipeline_mode=const)],
        out_specs=pl.BlockSpec((MOE_BLOCK, D_MODEL), lambda i, be: (i, 0)),
    )
    return pl.pallas_call(
        _expert_kernel,
        out_shape=jax.ShapeDtypeStruct((n_rows, D_MODEL), F32),
        grid_spec=grid_spec,
        compiler_params=_cparams(("arbitrary",)),
        name="grouped_experts",
    )(block_e, xb, w1, w3, w2)


def moe_ffn(xs, g, mod, gt2, router_w, router_b, w1, w3, w2):
    bsz = xs.shape[0]
    h, logits = norm_router(xs, g, mod, router_w, router_b)
    h = h.reshape(-1, D_MODEL)
    n_tok = h.shape[0]
    top_logit, top_e = lax.top_k(logits.reshape(n_tok, LANE)[:, :N_EXPERTS], TOP_K)
    gate = jax.nn.softmax(top_logit, axis=-1)
    n_assign = n_tok * TOP_K
    flat_e = top_e.reshape(-1)
    flat_tok = jnp.repeat(jnp.arange(n_tok, dtype=jnp.int32), TOP_K)
    flat_gate = gate.reshape(-1)
    order = jnp.argsort(flat_e)
    sorted_e = flat_e[order]
    counts = jnp.bincount(flat_e, length=N_EXPERTS)
    padded = (counts + MOE_BLOCK - 1) // MOE_BLOCK * MOE_BLOCK
    start = jnp.cumsum(counts) - counts
    p_end = jnp.cumsum(padded)
    p_start = p_end - padded
    dest = p_start[sorted_e] + jnp.arange(n_assign, dtype=jnp.int32) - start[sorted_e]
    n_blocks = -(-n_assign // MOE_BLOCK) + N_EXPERTS
    n_rows = n_blocks * MOE_BLOCK
    row_tok = jnp.zeros((n_rows,), jnp.int32).at[dest].set(flat_tok[order])
    row_gate = jnp.zeros((n_rows,), F32).at[dest].set(flat_gate[order])
    block_e = jnp.minimum(jnp.searchsorted(p_end, jnp.arange(n_blocks) * MOE_BLOCK, side='right'),
                          N_EXPERTS - 1).astype(jnp.int32)
    yb = grouped_experts(h[row_tok], block_e, w1, w3, w2)
    y = jnp.zeros((n_tok, D_MODEL), F32).at[row_tok].add(yb * row_gate[:, None])
    return xs + gt2 * y.reshape(bsz, S_ALL, D_MODEL)


FINAL_TM = 256


def _final_kernel(x_ref, g_ref, o_ref):
    x = x_ref[0]
    o_ref[0] = x * lax.rsqrt(jnp.mean(x * x, axis=-1, keepdims=True) + EPS) * g_ref[...]


def final_norm(xs, g):
    bsz = xs.shape[0]
    skip = CTX_LEN // FINAL_TM
    return pl.pallas_call(
        _final_kernel,
        out_shape=jax.ShapeDtypeStruct((bsz, SEQ, D_MODEL), F32),
        grid=(bsz, SEQ // FINAL_TM),
        in_specs=[pl.BlockSpec((1, FINAL_TM, D_MODEL), lambda b, s: (b, s + skip, 0)),
                  pl.BlockSpec((1, D_MODEL), lambda b, s: (0, 0))],
        out_specs=pl.BlockSpec((1, FINAL_TM, D_MODEL), lambda b, s: (b, s, 0)),
        compiler_params=_cparams(("arbitrary", "arbitrary")),
        name="final_norm",
    )(xs, g.reshape(1, D_MODEL))


N_CHUNKS = S_ALL // CHUNK
CTX_CHUNKS = CTX_LEN // CHUNK
GATE_ROWS = 8
HI = lax.Precision.HIGHEST
NEG_INF = float("-inf")


def _softplus(v):
    return jnp.maximum(v, 0.0) + jnp.log(1.0 + jnp.exp(-jnp.abs(v)))


def _scan_constants():
    s = np.arange(CHUNK)[:, None]
    t = np.arange(CHUNK)[None, :]
    tri = np.concatenate([(s <= t), (s >= t), np.ones((CHUNK, CHUNK), bool)], axis=1).astype(np.float32)
    return jnp.asarray(tri), jnp.asarray(np.eye(CHUNK, dtype=np.float32))


def _rev_chunk(j):
    return jnp.where(j < CTX_CHUNKS, CTX_CHUNKS - 1 - j, N_CHUNKS + CTX_CHUNKS - 1 - j)


def _head_gate_rows(small, c0, heads, kind_major):
    bsz = small.shape[0]
    a = small[:, :, c0:c0 + 4 * heads].reshape(bsz, N_CHUNKS, CHUNK, 2, 2, heads)
    a = a.transpose((0, 5, 1, 4, 3, 2) if kind_major else (0, 5, 1, 3, 4, 2))
    a = a.reshape(bsz, heads, N_CHUNKS, 4, CHUNK)
    return jnp.pad(a, ((0, 0), (0, 0), (0, 0), (0, GATE_ROWS - 4), (0, 0)))


def _tn_dot(a, b):
    return lax.dot_general(a, b, (((0,), (0,)), ((), ())), preferred_element_type=F32)


def _nt_dot(a, b, precision=None):
    return lax.dot_general(a, b, (((1,), (1,)), ((), ())), preferred_element_type=F32, precision=precision)


def _dir_mask(d):
    t = lax.broadcasted_iota(jnp.int32, (CHUNK, CHUNK), 0)
    s = lax.broadcasted_iota(jnp.int32, (CHUNK, CHUNK), 1)
    return (s <= t, s < t) if d == 0 else (s >= t, s > t)


def _gate_forms(vals, tri, eye, d):
    cum = jnp.dot(vals, tri, preferred_element_type=F32, precision=HI)
    run = cum[:, d * CHUNK:(d + 1) * CHUNK]
    rowid = lax.broadcasted_iota(jnp.int32, vals.shape, 0)
    rows = jnp.where(rowid == 2 * d + 1, run, vals)
    cols = _nt_dot(eye, rows, precision=HI)
    total = cum[2 * d + 1:2 * d + 2, 2 * CHUNK:2 * CHUNK + 1]
    return (rows[2 * d:2 * d + 1], rows[2 * d + 1:2 * d + 2],
            cols[:, 2 * d:2 * d + 1], cols[:, 2 * d + 1:2 * d + 2], total)


def _mlstm_kernel(q_ref, k_ref, v_ref, o_ref, g_ref, fb_ref, ng_ref, tri_ref, eye_ref, y_ref,
                  acc_ref, c_ref, n_ref, m_ref):
    tri, eye = tri_ref[...], eye_ref[...]
    fbias = fb_ref[0]
    scale = M_DQK ** -0.5
    c_ref[...] = jnp.zeros_like(c_ref)
    n_ref[...] = jnp.zeros_like(n_ref)
    m_ref[...] = jnp.zeros_like(m_ref)
    rowid = lax.broadcasted_iota(jnp.int32, (GATE_ROWS, CHUNK), 0)

    def chunk(d, c):
        rows = pl.ds(pl.multiple_of(c * CHUNK, CHUNK), CHUNK)
        raw = g_ref[0, 0, c]
        vals = jnp.where((rowid & 1) == 1, -_softplus(-(raw + fbias)), raw)
        ig_r, b_r, ig_c, b_c, tot = _gate_forms(vals, tri, eye, d)
        incl, _ = _dir_mask(d)
        q, k, v = q_ref[0, rows, :], k_ref[0, rows, :], v_ref[0, rows, :]
        c_prev, n_prev, m_prev = c_ref[d], n_ref[d], m_ref[d][:, :1]
        d_log = jnp.where(incl, b_c - b_r + ig_r, NEG_INF)
        m_inter = b_c + m_prev
        m_comb = jnp.maximum(m_inter, jnp.max(d_log, axis=-1, keepdims=True))
        s = _nt_dot(q, k) * scale * jnp.exp(d_log - m_comb)
        e_inter = jnp.exp(m_inter - m_comb)
        num = (jnp.dot(s.astype(BF16), v, preferred_element_type=F32)
               + e_inter * jnp.dot(q, c_prev.astype(BF16), preferred_element_type=F32))
        den = (jnp.sum(s, axis=-1, keepdims=True)
               + e_inter * jnp.sum(q.astype(F32) * n_prev, axis=-1, keepdims=True))
        acc_ref[d, rows, :] = num / jnp.maximum(jnp.abs(den), jnp.exp(-m_comb))
        a_end_c = tot - b_c + ig_c
        m_loc = jnp.max(tot - b_r + ig_r, axis=-1, keepdims=True)
        kw = k.astype(F32) * (jnp.exp(a_end_c - m_loc) * scale)
        m_new = jnp.maximum(tot + m_prev, m_loc)
        s_prev = jnp.exp(tot + m_prev - m_new)
        s_loc = jnp.exp(m_loc - m_new)
        c_ref[d] = s_prev * c_prev + s_loc * _tn_dot(kw.astype(BF16), v)
        n_ref[d] = s_prev * n_prev + s_loc * jnp.sum(kw, axis=0, keepdims=True)
        m_ref[d] = jnp.broadcast_to(m_new, m_ref.shape[1:])

    def step(j, carry):
        chunk(0, j)
        chunk(1, _rev_chunk(j))
        return carry

    lax.fori_loop(0, N_CHUNKS, step, 0)

    def finish(c, carry):
        rows = pl.ds(pl.multiple_of(c * CHUNK, CHUNK), CHUNK)
        h = acc_ref[0, rows, :] + acc_ref[1, rows, :]
        h = h * lax.rsqrt(jnp.mean(h * h, axis=-1, keepdims=True) + EPS) * ng_ref[...]
        y_ref[0, rows, :] = (h * _sigmoid(o_ref[0, rows, :].astype(F32))).astype(y_ref.dtype)
        return carry

    lax.fori_loop(0, N_CHUNKS, finish, 0)


def mlstm_mixer(big, small, f_bias, norm_g):
    bsz = big.shape[0]
    gates = _head_gate_rows(small, 0, M_HEADS, False)
    fb = jnp.zeros((M_HEADS, GATE_ROWS, CHUNK), F32)
    fb = fb.at[:, 1].set(f_bias[0][:, None]).at[:, 3].set(f_bias[1][:, None])
    tri, eye = _scan_constants()
    qb, kb, vb, ob = C_MQ // M_DQK, C_MK // M_DQK, C_MV // M_DV, C_MO // M_DV
    return pl.pallas_call(
        _mlstm_kernel,
        out_shape=jax.ShapeDtypeStruct((bsz, S_ALL, BRANCH_W), BF16),
        grid=(bsz, M_HEADS),
        in_specs=[pl.BlockSpec((1, S_ALL, M_DQK), lambda b, h: (b, 0, qb + h)),
                  pl.BlockSpec((1, S_ALL, M_DQK), lambda b, h: (b, 0, kb + h)),
                  pl.BlockSpec((1, S_ALL, M_DV), lambda b, h: (b, 0, vb + h)),
                  pl.BlockSpec((1, S_ALL, M_DV), lambda b, h: (b, 0, ob + h)),
                  pl.BlockSpec((1, 1, N_CHUNKS, GATE_ROWS, CHUNK), lambda b, h: (b, h, 0, 0, 0)),
                  pl.BlockSpec((1, GATE_ROWS, CHUNK), lambda b, h: (h, 0, 0)),
                  pl.BlockSpec((1, M_DV), lambda b, h: (0, h)),
                  pl.BlockSpec((CHUNK, 3 * CHUNK), lambda b, h: (0, 0)),
                  pl.BlockSpec((CHUNK, CHUNK), lambda b, h: (0, 0))],
        out_specs=pl.BlockSpec((1, S_ALL, M_DV), lambda b, h: (b, 0, h)),
        scratch_shapes=[pltpu.VMEM((2, S_ALL, M_DV), F32), pltpu.VMEM((2, M_DQK, M_DV), F32),
                        pltpu.VMEM((2, 1, M_DQK), F32), pltpu.VMEM((2, 1, LANE), F32)],
        compiler_params=_cparams(("arbitrary", "arbitrary")),
        name="mlstm_mixer",
    )(big, big, big, big, gates, fb, norm_g.reshape(1, BRANCH_W), tri, eye)


CONV_BLK = 256
CONV_HALO = 16
SOLVE_BLK = 8


def _conv_silu(src_ref, w, dst_ref, l2_scale):
    n_blk = S_ALL // CONV_BLK
    zeros = jnp.zeros((CONV_HALO, src_ref.shape[-1]), F32)
    for i in range(n_blk):
        lo = i * CONV_BLK
        first = i == 0 or lo == CTX_LEN
        last = lo + CONV_BLK in (CTX_LEN, S_ALL)
        body = src_ref[0, lo - (0 if first else CONV_HALO):lo + CONV_BLK + (0 if last else CONV_HALO), :]
        win = jnp.concatenate(([zeros] if first else []) + [body.astype(F32)] + ([zeros] if last else []), axis=0)
        n = win.shape[0]
        y = None
        for j in range(CONV_W):
            tap = win if j == CONV_W // 2 else pltpu.roll(win, (CONV_W // 2 - j) % n, 0)
            term = tap[CONV_HALO:CONV_HALO + CONV_BLK] * w[j:j + 1, :]
            y = term if y is None else y + term
        y = y * _sigmoid(y)
        if l2_scale is not None:
            y = y * (lax.rsqrt(jnp.sum(y * y, axis=-1, keepdims=True) + EPS) * l2_scale)
        dst_ref[lo:lo + CONV_BLK, :] = y.astype(dst_ref.dtype)


def _block_masks():
    t = np.arange(CHUNK)[:, None]
    s = np.arange(CHUNK)[None, :]
    masks = [(t // SOLVE_BLK) == (s // SOLVE_BLK)]
    k = SOLVE_BLK
    while k < CHUNK:
        masks.append(((t // (2 * k)) == (s // (2 * k))) & ((t // k) != (s // k)))
        k *= 2
    return jnp.asarray(np.stack(masks).astype(np.float32))


def _unit_triangular_inverse(a, eye, masks):
    d = a * masks[0]
    d2 = jnp.dot(d, d, preferred_element_type=F32, precision=HI)
    d4 = jnp.dot(d2, d2, preferred_element_type=F32, precision=HI)
    t = eye - d
    t = t + jnp.dot(t, d2, preferred_element_type=F32, precision=HI)
    t = t + jnp.dot(t, d4, preferred_element_type=F32, precision=HI)
    for lvl in range(1, masks.shape[0]):
        tb = t.astype(BF16)
        tl = jnp.dot(tb, (a * masks[lvl]).astype(BF16), preferred_element_type=F32)
        t = t - jnp.dot(tl.astype(BF16), tb, preferred_element_type=F32)
    return t


def _gdn_kernel(xq_ref, xk_ref, xv_ref, z_ref, g_ref, gp_ref, cw_ref, ng_ref, tri_ref, eye_ref, bm_ref, y_ref,
                q_s, k_s, v_s, u_s, wq_s, at_s, kt_s, cd_s, acc_ref, s_ref):
    tri, eye, masks = tri_ref[...], eye_ref[...], bm_ref[...]
    cw = cw_ref[0]
    _conv_silu(xq_ref, cw[:, :G_DK], q_s, G_DK ** -0.5)
    _conv_silu(xk_ref, cw[:, G_DK:2 * G_DK], k_s, 1.0)
    _conv_silu(xv_ref, cw[:, 2 * G_DK:], v_s, None)
    s_ref[...] = jnp.zeros_like(s_ref)
    rowid = lax.broadcasted_iota(jnp.int32, (GATE_ROWS, CHUNK), 0)
    neg_a, dt_bias = gp_ref[0, 0], gp_ref[0, 1]

    def prep(c, carry):
        rows = pl.ds(pl.multiple_of(c * CHUNK, CHUNK), CHUNK)
        raw = g_ref[0, 0, c]
        vals = jnp.where((rowid & 1) == 1, neg_a * _softplus(raw + dt_bias), _sigmoid(raw))
        q, k, v = q_s[rows, :], k_s[rows, :], v_s[rows, :]
        kf, vf, qf = k.astype(F32), v.astype(F32), q.astype(F32)
        kk = _nt_dot(k, k)
        qk = _nt_dot(q, k)
        for d in range(2):
            _, gam_r, beta_c, gam_c, tot = _gate_forms(vals, tri, eye, d)
            incl, strict = _dir_mask(d)
            decay = jnp.exp(jnp.where(incl, gam_c - gam_r, NEG_INF))
            a_tri = jnp.where(strict, kk * beta_c * decay, 0.0)
            t_inv = _unit_triangular_inverse(a_tri, eye, masks)
            e_gam = jnp.exp(gam_c)
            rhs = jnp.concatenate([vf * beta_c, kf * (beta_c * e_gam)], axis=-1).astype(BF16)
            x = jnp.dot(t_inv.astype(BF16), rhs, preferred_element_type=F32)
            u_s[d, rows, :] = x[:, :G_DV]
            rows2 = pl.ds(pl.multiple_of(c * 2 * CHUNK, 2 * CHUNK), 2 * CHUNK)
            wq_s[d, rows2, :] = jnp.concatenate([x[:, G_DV:], qf * e_gam], axis=0).astype(BF16)
            at_s[d, rows, :] = (qk * decay).astype(BF16)
            kt_s[d, rows, :] = (kf * jnp.exp(tot - gam_c)).astype(BF16)
            cd_s[d, c] = jnp.broadcast_to(jnp.exp(tot), cd_s.shape[2:])
        return carry

    lax.fori_loop(0, N_CHUNKS, prep, 0)

    def chunk(d, c):
        rows = pl.ds(pl.multiple_of(c * CHUNK, CHUNK), CHUNK)
        rows2 = pl.ds(pl.multiple_of(c * 2 * CHUNK, 2 * CHUNK), 2 * CHUNK)
        s_prev = s_ref[d]
        ws_qs = jnp.dot(wq_s[d, rows2, :], s_prev.astype(BF16), preferred_element_type=F32)
        v_nb = (u_s[d, rows, :] - ws_qs[:CHUNK]).astype(BF16)
        acc_ref[d, rows, :] = ws_qs[CHUNK:] + jnp.dot(at_s[d, rows, :], v_nb, preferred_element_type=F32)
        s_ref[d] = cd_s[d, c][:1, :] * s_prev + _tn_dot(kt_s[d, rows, :], v_nb)

    def step(j, carry):
        chunk(0, j)
        chunk(1, _rev_chunk(j))
        return carry

    lax.fori_loop(0, N_CHUNKS, step, 0)

    def finish(c, carry):
        rows = pl.ds(pl.multiple_of(c * CHUNK, CHUNK), CHUNK)
        h = acc_ref[0, rows, :] + acc_ref[1, rows, :]
        h = h * lax.rsqrt(jnp.mean(h * h, axis=-1, keepdims=True) + EPS) * ng_ref[...]
        z = z_ref[0, rows, :].astype(F32)
        y_ref[0, rows, :] = (h * (z * _sigmoid(z))).astype(y_ref.dtype)
        return carry

    lax.fori_loop(0, N_CHUNKS, finish, 0)


def gdn_mixer(big, small, conv_w, a_log, dt_bias, norm_g):
    bsz = big.shape[0]
    gates = _head_gate_rows(small, 4 * M_HEADS, G_HEADS, True)
    gp = jnp.zeros((G_HEADS, 2, GATE_ROWS, CHUNK), F32)
    for d in range(2):
        gp = gp.at[:, 0, 2 * d + 1].set(-jnp.exp(a_log[d])[:, None]).at[:, 1, 2 * d + 1].set(dt_bias[d][:, None])
    cw = conv_w.reshape(CONV_W, 3, G_HEADS, G_DK).transpose(2, 0, 1, 3).reshape(G_HEADS, CONV_W, 3 * G_DK)
    cw = jnp.pad(cw, ((0, 0), (0, 8 - CONV_W), (0, 0)))
    tri, eye = _scan_constants()
    masks = _block_masks()
    qb, kb, vb, zb = C_GQ // G_DK, C_GK // G_DK, C_GV // G_DV, C_GZ // G_DV
    return pl.pallas_call(
        _gdn_kernel,
        out_shape=jax.ShapeDtypeStruct((bsz, S_ALL, BRANCH_W), BF16),
        grid=(bsz, G_HEADS),
        in_specs=[pl.BlockSpec((1, S_ALL, G_DK), lambda b, h: (b, 0, qb + h)),
                  pl.BlockSpec((1, S_ALL, G_DK), lambda b, h: (b, 0, kb + h)),
                  pl.BlockSpec((1, S_ALL, G_DV), lambda b, h: (b, 0, vb + h)),
                  pl.BlockSpec((1, S_ALL, G_DV), lambda b, h: (b, 0, zb + h)),
                  pl.BlockSpec((1, 1, N_CHUNKS, GATE_ROWS, CHUNK), lambda b, h: (b, h, 0, 0, 0)),
                  pl.BlockSpec((1, 2, GATE_ROWS, CHUNK), lambda b, h: (h, 0, 0, 0)),
                  pl.BlockSpec((1, 8, 3 * G_DK), lambda b, h: (h, 0, 0)),
                  pl.BlockSpec((1, G_DV), lambda b, h: (0, 0)),
                  pl.BlockSpec((CHUNK, 3 * CHUNK), lambda b, h: (0, 0)),
                  pl.BlockSpec((CHUNK, CHUNK), lambda b, h: (0, 0)),
                  pl.BlockSpec(masks.shape, lambda b, h: (0, 0, 0))],
        out_specs=pl.BlockSpec((1, S_ALL, G_DV), lambda b, h: (b, 0, h)),
        scratch_shapes=[pltpu.VMEM((S_ALL, G_DK), BF16), pltpu.VMEM((S_ALL, G_DK), BF16),
                        pltpu.VMEM((S_ALL, G_DV), BF16),
                        pltpu.VMEM((2, S_ALL, G_DV), F32),
                        pltpu.VMEM((2, 2 * S_ALL, G_DK), BF16),
                        pltpu.VMEM((2, S_ALL, CHUNK), BF16),
                        pltpu.VMEM((2, S_ALL, G_DK), BF16),
                        pltpu.VMEM((2, N_CHUNKS, 8, G_DV), F32),
                        pltpu.VMEM((2, S_ALL, G_DV), F32),
                        pltpu.VMEM((2, G_DK, G_DV), F32)],
        compiler_params=_cparams(("arbitrary", "arbitrary")),
        name="gdn_mixer",
    )(big, big, big, big, gates, gp, cw, norm_g.reshape(1, G_DV), tri, eye, masks)


GRID_H = SEQ // GRID_W
LRU_BLK = 256


def _shift_rows(a, s):
    n = a.shape[0]
    r = lax.broadcasted_iota(jnp.int32, a.shape, 0)
    return jnp.where((r >= s) & (r < n + s), pltpu.roll(a, s % n, 0), 0.0)


def _gelu(v):
    return 0.5 * v * (1.0 + jnp.tanh(0.7978845608028654 * (v + 0.044715 * v * v * v)))


def _lru_kernel(x_ref, y_ref, cw_ref, cb_ref, gw_ref, gb_ref, lam_ref, o_ref,
                xb_s, a_s, u_s, h_s, p_s, end_s, cin_s):
    cw = cw_ref[...]
    xf = x_ref[0].astype(F32)
    xc = xf[:CTX_LEN]
    lat = lambda r0, r1: xf[CTX_LEN + r0 * GRID_W:CTX_LEN + r1 * GRID_W]
    prev1 = jnp.concatenate([_shift_rows(lat(GRID_H - 1, GRID_H), 1), lat(0, GRID_H - 1)], axis=0)
    prev2 = jnp.concatenate([_shift_rows(lat(GRID_H - 2, GRID_H - 1), 1), _shift_rows(lat(GRID_H - 1, GRID_H), 1),
                             lat(0, GRID_H - 2)], axis=0)
    next1 = jnp.concatenate([lat(1, GRID_H), _shift_rows(lat(0, 1), -1)], axis=0)
    taps_c = (_shift_rows(xc, 2), _shift_rows(xc, 1), xc, _shift_rows(xc, -1))
    taps_l = (prev2, prev1, lat(0, GRID_H), next1)
    xb_s[:CTX_LEN, :] = sum(t * cw[j:j + 1, :] for j, t in enumerate(taps_c)) + cb_ref[...]
    xb_s[CTX_LEN:, :] = sum(t * cw[j:j + 1, :] for j, t in enumerate(taps_l)) + cb_ref[...]

    neg_c_sp = -LRU_C * _softplus(-lam_ref[...])

    def gates(i, carry):
        rows = pl.ds(pl.multiple_of(i * LRU_BLK, LRU_BLK), LRU_BLK)
        xb = xb_s[rows, :]
        pre = jnp.dot(xb.astype(BF16), gw_ref[0], preferred_element_type=F32) + gb_ref[0]
        for d in range(2):
            r = _sigmoid(pre[:, (2 * d) * LANE:(2 * d + 1) * LANE])
            g_in = _sigmoid(pre[:, (2 * d + 1) * LANE:(2 * d + 2) * LANE])
            log_a = neg_c_sp[d:d + 1, :] * r
            a = jnp.exp(log_a)
            a_s[d, rows, :] = a
            u_s[d, rows, :] = jnp.sqrt(jnp.maximum(1.0 - a * a, 0.0)) * g_in * xb
        return carry

    lax.fori_loop(0, S_ALL // LRU_BLK, gates, 0)

    def ctx_step(t, carry):
        hf, hb = carry
        tb = CTX_LEN - 1 - t
        hf = a_s[0, pl.ds(t, 1), :] * hf + u_s[0, pl.ds(t, 1), :]
        hb = a_s[1, pl.ds(tb, 1), :] * hb + u_s[1, pl.ds(tb, 1), :]
        h_s[0, pl.ds(t, 1), :] = hf
        h_s[1, pl.ds(tb, 1), :] = hb
        return hf, hb

    zero_row = jnp.zeros((1, LANE), F32)
    hf0, hb0 = lax.fori_loop(0, CTX_LEN, ctx_step, (zero_row, zero_row))

    def col_step(i, carry):
        hf, pf, hb, pb = carry
        rf = pl.ds(pl.multiple_of(CTX_LEN + i * GRID_W, GRID_W), GRID_W)
        rb = pl.ds(pl.multiple_of(CTX_LEN + (GRID_H - 1 - i) * GRID_W, GRID_W), GRID_W)
        af, ab = a_s[0, rf, :], a_s[1, rb, :]
        hf = af * hf + u_s[0, rf, :]
        pf = af * pf
        hb = ab * hb + u_s[1, rb, :]
        pb = ab * pb
        h_s[0, rf, :] = hf
        p_s[0, pl.ds(pl.multiple_of(i * GRID_W, GRID_W), GRID_W), :] = pf
        h_s[1, rb, :] = hb
        p_s[1, pl.ds(pl.multiple_of((GRID_H - 1 - i) * GRID_W, GRID_W), GRID_W), :] = pb
        return hf, pf, hb, pb

    zeros = jnp.zeros((GRID_W, LANE), F32)
    ones = jnp.ones((GRID_W, LANE), F32)
    hf, pf, hb, pb = lax.fori_loop(0, GRID_H, col_step, (zeros, ones, zeros, ones))
    end_s[0], end_s[1], end_s[2], end_s[3] = hf, pf, hb, pb

    def chain_step(w, carry):
        cf, cb = carry
        wb = GRID_W - 1 - w
        cin_s[0, pl.ds(w, 1), :] = cf
        cin_s[1, pl.ds(wb, 1), :] = cb
        cf = end_s[0, pl.ds(w, 1), :] + end_s[1, pl.ds(w, 1), :] * cf
        cb = end_s[2, pl.ds(wb, 1), :] + end_s[3, pl.ds(wb, 1), :] * cb
        return cf, cb

    lax.fori_loop(0, GRID_W, chain_step, (hf0, hb0))

    o_ref[0, :CTX_LEN, :] = ((h_s[0, :CTX_LEN, :] + h_s[1, :CTX_LEN, :])
                             * _gelu(y_ref[0, :CTX_LEN, :].astype(F32))).astype(o_ref.dtype)

    def out_step(i, carry):
        rows = pl.ds(pl.multiple_of(CTX_LEN + i * GRID_W, GRID_W), GRID_W)
        prow = pl.ds(pl.multiple_of(i * GRID_W, GRID_W), GRID_W)
        h = (h_s[0, rows, :] + p_s[0, prow, :] * cin_s[0] + h_s[1, rows, :] + p_s[1, prow, :] * cin_s[1])
        o_ref[0, rows, :] = (h * _gelu(y_ref[0, rows, :].astype(F32))).astype(o_ref.dtype)
        return carry

    lax.fori_loop(0, GRID_H, out_step, 0)


def lru_mixer(big, conv_w, conv_b, gate_w, gate_b, lam):
    bsz = big.shape[0]
    gw = gate_w.transpose(2, 3, 0, 1, 4).reshape(R_BLOCKS, R_BDIM, 4 * R_BDIM).astype(BF16)
    gb = gate_b.reshape(2, 2, R_BLOCKS, R_BDIM).transpose(2, 0, 1, 3).reshape(R_BLOCKS, 1, 4 * R_BDIM)
    cw = jnp.pad(conv_w, ((0, 8 - CONV_W), (0, 0)))
    xb, yb = C_RX // R_BDIM, C_RY // R_BDIM
    return pl.pallas_call(
        _lru_kernel,
        out_shape=jax.ShapeDtypeStruct((bsz, S_ALL, BRANCH_W), BF16),
        grid=(bsz, R_BLOCKS),
        in_specs=[pl.BlockSpec((1, S_ALL, R_BDIM), lambda b, n: (b, 0, xb + n)),
                  pl.BlockSpec((1, S_ALL, R_BDIM), lambda b, n: (b, 0, yb + n)),
                  pl.BlockSpec((8, R_BDIM), lambda b, n: (0, n)),
                  pl.BlockSpec((1, R_BDIM), lambda b, n: (0, n)),
                  pl.BlockSpec((1, R_BDIM, 4 * R_BDIM), lambda b, n: (n, 0, 0)),
                  pl.BlockSpec((1, 1, 4 * R_BDIM), lambda b, n: (n, 0, 0)),
                  pl.BlockSpec((2, R_BDIM), lambda b, n: (0, n))],
        out_specs=pl.BlockSpec((1, S_ALL, R_BDIM), lambda b, n: (b, 0, n)),
        scratch_shapes=[pltpu.VMEM((S_ALL, R_BDIM), F32), pltpu.VMEM((2, S_ALL, R_BDIM), F32),
                        pltpu.VMEM((2, S_ALL, R_BDIM), F32), pltpu.VMEM((2, S_ALL, R_BDIM), F32),
                        pltpu.VMEM((2, SEQ, R_BDIM), F32), pltpu.VMEM((4, GRID_W, R_BDIM), F32),
                        pltpu.VMEM((2, GRID_W, R_BDIM), F32)],
        compiler_params=_cparams(("arbitrary", "arbitrary")),
        name="lru_mixer",
    )(big, big, cw, conv_b.reshape(1, BRANCH_W), gw, gb, lam)


def _pack_in_weights(w, b):
    segs_w = [w[:, IN_OFFS[i]:IN_OFFS[i + 1]] for i in range(len(IN_COLS))]
    segs_b = [b[IN_OFFS[i]:IN_OFFS[i + 1]] for i in range(len(IN_COLS))]
    w_big = jnp.concatenate([segs_w[i] for i in BIG_SEGS], axis=1).astype(BF16)
    b_big = jnp.concatenate([segs_b[i] for i in BIG_SEGS]).reshape(1, N_BIG)
    pad = N_SMALL - sum(IN_COLS[i] for i in SMALL_SEGS)
    w_small = jnp.pad(jnp.concatenate([segs_w[i] for i in SMALL_SEGS], axis=1), ((0, 0), (0, pad))).astype(BF16)
    b_small = jnp.pad(jnp.concatenate([segs_b[i] for i in SMALL_SEGS]), (0, pad)).reshape(1, N_SMALL)
    return w_big, b_big, w_small, b_small


def kernel(x, c, ctx, c_ctx, mod_w, mod_b, norm1_g, norm2_g, final_g, w_in, b_in, m_fbias, m_norm_g,
           g_conv, g_a_log, g_dt_bias, g_norm_g, r_conv, r_conv_b, r_gate_w, r_gate_b, r_lambda,
           w_branch, w_out, ffn_w1, ffn_w3, ffn_w2, router_w, router_b, moe_w1, moe_w3, moe_w2):
    bsz = x.shape[0]
    xs = jnp.concatenate([ctx, x], axis=1)
    src = jnp.concatenate([c, c_ctx[None]], axis=0)
    mods = modulation(src, mod_w, mod_b)
    for layer in range(DEPTH):
        mod = mods[layer]
        w_big, b_big, w_small, b_small = _pack_in_weights(w_in[layer], b_in[layer])
        big, small = in_projection(xs, norm1_g[layer], mod, w_big, b_big, w_small, b_small)
        ym = mlstm_mixer(big, small, m_fbias[layer], m_norm_g[layer])
        yg = gdn_mixer(big, small, g_conv[layer], g_a_log[layer], g_dt_bias[layer], g_norm_g[layer])
        yr = lru_mixer(big, r_conv[layer], r_conv_b[layer], r_gate_w[layer], r_gate_b[layer], r_lambda[layer])
        xs = merge_branches(ym, yg, yr, big, xs, mod, w_branch[layer].astype(BF16), w_out[layer].astype(BF16))
        j = layer // 2
        if layer % 2 == 0:
            xs = dense_ffn(xs, norm2_g[layer], mod, ffn_w1[j].astype(BF16), ffn_w3[j].astype(BF16),
                           ffn_w2[j].astype(BF16))
        else:
            gt2 = jnp.concatenate([jnp.broadcast_to(mod[bsz:, None, 5 * D_MODEL:], (bsz, CTX_LEN, D_MODEL)),
                                   jnp.broadcast_to(mod[:bsz, None, 5 * D_MODEL:], (bsz, SEQ, D_MODEL))], axis=1)
            xs = moe_ffn(xs, norm2_g[layer], mod, gt2, router_w[j], router_b[j], moe_w1[j].astype(BF16),
                         moe_w3[j].astype(BF16), moe_w2[j].astype(BF16))
    return final_norm(xs, final_g)
```

```python
import jax
import jax.numpy as jnp
import numpy as np
from jax import lax
from jax.experimental import pallas as pl
from jax.experimental.pallas import tpu as pltpu

D_MODEL = 1024
SEQ = 2048
DEPTH = 4
GRID_W = 64
CTX_LEN = 256
S_ALL = CTX_LEN + SEQ
BRANCH_W = D_MODEL
N_BRANCH = 3
CHUNK = 64
CONV_W = 4
M_HEADS = 4
M_DQK = 128
M_DV = BRANCH_W // M_HEADS
G_HEADS = 8
G_DK = 128
G_DV = BRANCH_W // G_HEADS
R_BLOCKS = 8
R_BDIM = BRANCH_W // R_BLOCKS
LRU_C = 8.0
D_FF = 2816
N_EXPERTS = 8
TOP_K = 2
MOE_BLOCK = 128
EPS = 1e-6
IN_COLS = (M_HEADS * M_DQK, M_HEADS * M_DQK, BRANCH_W, BRANCH_W, 4 * M_HEADS,
           2 * G_HEADS * G_DK + BRANCH_W, BRANCH_W, 2 * G_HEADS, 2 * G_HEADS,
           BRANCH_W, BRANCH_W, N_BRANCH * D_MODEL)
IN_OFFS = tuple(int(v) for v in np.cumsum((0,) + IN_COLS))
SMALL_SEGS = (4, 7, 8)
BIG_SEGS = tuple(i for i in range(len(IN_COLS)) if i not in SMALL_SEGS)
N_BIG = sum(IN_COLS[i] for i in BIG_SEGS)
N_SMALL = 128
C_MQ, C_MK, C_MV, C_MO = 0, 512, 1024, 2048
C_GQ, C_GK, C_GV, C_GZ = 3072, 4096, 5120, 6144
C_RX, C_RY, C_MG = 7168, 8192, 9216

LANE = 128
VMEM_LIMIT = 56 * 1024 * 1024
F32 = jnp.float32
BF16 = jnp.bfloat16


def _cparams(sem):
    return pltpu.CompilerParams(dimension_semantics=sem, vmem_limit_bytes=VMEM_LIMIT)


def _sigmoid(v):
    return 1.0 / (1.0 + jnp.exp(-v))


def _ada_norm(x, g, mod_l, mod_c, first_row, which):
    y = x * lax.rsqrt(jnp.mean(x * x, axis=-1, keepdims=True) + EPS) * g
    sh_l, sc_l = _mod_slice(mod_l, 3 * which), _mod_slice(mod_l, 3 * which + 1)
    sh_c, sc_c = _mod_slice(mod_c, 3 * which), _mod_slice(mod_c, 3 * which + 1)
    is_ctx = _is_ctx(x.shape[0], first_row)
    return y * (1.0 + jnp.where(is_ctx, sc_c, sc_l)) + jnp.where(is_ctx, sh_c, sh_l)


def _mod_slice(mod, k):
    return mod[:, k * D_MODEL:(k + 1) * D_MODEL]


def _is_ctx(rows, first_row):
    return (lax.broadcasted_iota(jnp.int32, (rows, 1), 0) + first_row) < CTX_LEN


def _mod_kernel(src_ref, w_ref, b_ref, o_ref):
    s = src_ref[...]
    s = (s * _sigmoid(s)).astype(BF16)
    o_ref[0] = jnp.dot(s, w_ref[0].astype(BF16), preferred_element_type=F32) + b_ref[0]


def modulation(src, mod_w, mod_b):
    rows = src.shape[0]
    tn = 1536
    return pl.pallas_call(
        _mod_kernel,
        out_shape=jax.ShapeDtypeStruct((DEPTH, rows, 6 * D_MODEL), F32),
        grid=(DEPTH, 6 * D_MODEL // tn),
        in_specs=[pl.BlockSpec((rows, D_MODEL), lambda l, n: (0, 0)),
                  pl.BlockSpec((1, D_MODEL, tn), lambda l, n: (l, 0, n)),
                  pl.BlockSpec((1, 1, tn), lambda l, n: (l, 0, n))],
        out_specs=pl.BlockSpec((1, rows, tn), lambda l, n: (l, 0, n)),
        compiler_params=_cparams(("arbitrary", "arbitrary")),
        name="modulation",
    )(src, mod_w, mod_b.reshape(DEPTH, 1, 6 * D_MODEL))


IN_TM = 1152
IN_TN = 512


def _in_kernel(x_ref, g_ref, ml_ref, mc_ref, w_ref, b_ref, ws_ref, bs_ref, o_ref, os_ref, h_ref):
    n = pl.program_id(2)

    @pl.when(n == 0)
    def _():
        h = _ada_norm(x_ref[0], g_ref[...], ml_ref[0], mc_ref[0], pl.program_id(1) * IN_TM, 0)
        h_ref[...] = h.astype(BF16)
        os_ref[0] = jnp.dot(h_ref[...], ws_ref[...], preferred_element_type=F32) + bs_ref[...]

    o_ref[0] = (jnp.dot(h_ref[...], w_ref[...], preferred_element_type=F32) + b_ref[...]).astype(o_ref.dtype)


def in_projection(xs, g, mod, w_big, b_big, w_small, b_small):
    bsz = xs.shape[0]
    mod3 = mod.reshape(bsz + 1, 1, 6 * D_MODEL)
    return pl.pallas_call(
        _in_kernel,
        out_shape=(jax.ShapeDtypeStruct((bsz, S_ALL, N_BIG), BF16),
                   jax.ShapeDtypeStruct((bsz, S_ALL, N_SMALL), F32)),
        grid=(bsz, S_ALL // IN_TM, N_BIG // IN_TN),
        in_specs=[pl.BlockSpec((1, IN_TM, D_MODEL), lambda b, s, n: (b, s, 0)),
                  pl.BlockSpec((1, D_MODEL), lambda b, s, n: (0, 0)),
                  pl.BlockSpec((1, 1, 6 * D_MODEL), lambda b, s, n: (b, 0, 0)),
                  pl.BlockSpec((1, 1, 6 * D_MODEL), lambda b, s, n: (bsz, 0, 0)),
                  pl.BlockSpec((D_MODEL, IN_TN), lambda b, s, n: (0, n)),
                  pl.BlockSpec((1, IN_TN), lambda b, s, n: (0, n)),
                  pl.BlockSpec((D_MODEL, N_SMALL), lambda b, s, n: (0, 0)),
                  pl.BlockSpec((1, N_SMALL), lambda b, s, n: (0, 0))],
        out_specs=(pl.BlockSpec((1, IN_TM, IN_TN), lambda b, s, n: (b, s, n)),
                   pl.BlockSpec((1, IN_TM, N_SMALL), lambda b, s, n: (b, s, 0))),
        scratch_shapes=[pltpu.VMEM((IN_TM, D_MODEL), BF16)],
        compiler_params=_cparams(("arbitrary", "arbitrary", "arbitrary")),
        name="in_projection",
    )(xs, g.reshape(1, D_MODEL), mod3, mod3, w_big, b_big, w_small, b_small)


MERGE_TM = 768


def _merge_kernel(ym_ref, yg_ref, yr_ref, g0_ref, g1_ref, g2_ref, x_ref, ml_ref, mc_ref, wb_ref, wo_ref, o_ref):
    acc = None
    for n, (y_ref, gp_ref) in enumerate(((ym_ref, g0_ref), (yg_ref, g1_ref), (yr_ref, g2_ref))):
        p = jnp.dot(y_ref[0].astype(BF16), wb_ref[n], preferred_element_type=F32)
        p = p * _sigmoid(gp_ref[0].astype(F32))
        acc = p if acc is None else acc + p
    out = jnp.dot(acc.astype(BF16), wo_ref[...], preferred_element_type=F32)
    is_ctx = _is_ctx(MERGE_TM, pl.program_id(1) * MERGE_TM)
    gt = jnp.where(is_ctx, _mod_slice(mc_ref[0], 2), _mod_slice(ml_ref[0], 2))
    o_ref[0] = x_ref[0] + gt * out


def merge_branches(ym, yg, yr, big, xs, mod, w_branch, w_out):
    bsz = xs.shape[0]
    mod3 = mod.reshape(bsz + 1, 1, 6 * D_MODEL)
    row = lambda b, s: (b, s, 0)
    gate_blk = C_MG // D_MODEL
    const = pl.Buffered(1)
    return pl.pallas_call(
        _merge_kernel,
        out_shape=jax.ShapeDtypeStruct(xs.shape, F32),
        grid=(bsz, S_ALL // MERGE_TM),
        in_specs=[pl.BlockSpec((1, MERGE_TM, D_MODEL), row),
                  pl.BlockSpec((1, MERGE_TM, D_MODEL), row),
                  pl.BlockSpec((1, MERGE_TM, D_MODEL), row),
                  pl.BlockSpec((1, MERGE_TM, D_MODEL), lambda b, s: (b, s, gate_blk)),
                  pl.BlockSpec((1, MERGE_TM, D_MODEL), lambda b, s: (b, s, gate_blk + 1)),
                  pl.BlockSpec((1, MERGE_TM, D_MODEL), lambda b, s: (b, s, gate_blk + 2)),
                  pl.BlockSpec((1, MERGE_TM, D_MODEL), row),
                  pl.BlockSpec((1, 1, 6 * D_MODEL), lambda b, s: (b, 0, 0)),
                  pl.BlockSpec((1, 1, 6 * D_MODEL), lambda b, s: (bsz, 0, 0)),
                  pl.BlockSpec((N_BRANCH, BRANCH_W, D_MODEL), lambda b, s: (0, 0, 0), pipeline_mode=const),
                  pl.BlockSpec((D_MODEL, D_MODEL), lambda b, s: (0, 0), pipeline_mode=const)],
        out_specs=pl.BlockSpec((1, MERGE_TM, D_MODEL), row),
        compiler_params=_cparams(("arbitrary", "arbitrary")),
        name="merge_branches",
    )(ym, yg, yr, big, big, big, xs, mod3, mod3, w_branch, w_out)


FFN_TM = 1152
FFN_TF = 256


def _swiglu_chunk(h, w1, w3, w2):
    a = jnp.dot(h, w1, preferred_element_type=F32)
    b = jnp.dot(h, w3, preferred_element_type=F32)
    return jnp.dot((a * _sigmoid(a) * b).astype(BF16), w2, preferred_element_type=F32)


def _ffn_kernel(x_ref, g_ref, ml_ref, mc_ref, w1_ref, w3_ref, w2_ref, o_ref, h_ref, acc_ref):
    f = pl.program_id(2)

    @pl.when(f == 0)
    def _():
        h = _ada_norm(x_ref[0], g_ref[...], ml_ref[0], mc_ref[0], pl.program_id(1) * FFN_TM, 1)
        h_ref[...] = h.astype(BF16)
        acc_ref[...] = jnp.zeros_like(acc_ref)

    acc_ref[...] += _swiglu_chunk(h_ref[...], w1_ref[...], w3_ref[...], w2_ref[...])

    @pl.when(f == pl.num_programs(2) - 1)
    def _():
        is_ctx = _is_ctx(FFN_TM, pl.program_id(1) * FFN_TM)
        gt = jnp.where(is_ctx, _mod_slice(mc_ref[0], 5), _mod_slice(ml_ref[0], 5))
        o_ref[0] = x_ref[0] + gt * acc_ref[...]


def dense_ffn(xs, g, mod, w1, w3, w2):
    bsz = xs.shape[0]
    mod3 = mod.reshape(bsz + 1, 1, 6 * D_MODEL)
    return pl.pallas_call(
        _ffn_kernel,
        out_shape=jax.ShapeDtypeStruct(xs.shape, F32),
        grid=(bsz, S_ALL // FFN_TM, D_FF // FFN_TF),
        in_specs=[pl.BlockSpec((1, FFN_TM, D_MODEL), lambda b, s, f: (b, s, 0)),
                  pl.BlockSpec((1, D_MODEL), lambda b, s, f: (0, 0)),
                  pl.BlockSpec((1, 1, 6 * D_MODEL), lambda b, s, f: (b, 0, 0)),
                  pl.BlockSpec((1, 1, 6 * D_MODEL), lambda b, s, f: (bsz, 0, 0)),
                  pl.BlockSpec((D_MODEL, FFN_TF), lambda b, s, f: (0, f)),
                  pl.BlockSpec((D_MODEL, FFN_TF), lambda b, s, f: (0, f)),
                  pl.BlockSpec((FFN_TF, D_MODEL), lambda b, s, f: (f, 0))],
        out_specs=pl.BlockSpec((1, FFN_TM, D_MODEL), lambda b, s, f: (b, s, 0)),
        scratch_shapes=[pltpu.VMEM((FFN_TM, D_MODEL), BF16), pltpu.VMEM((FFN_TM, D_MODEL), F32)],
        compiler_params=_cparams(("arbitrary", "arbitrary", "arbitrary")),
        name="dense_ffn",
    )(xs, g.reshape(1, D_MODEL), mod3, mod3, w1, w3, w2)


NORM_TM = 768


def _norm_router_kernel(x_ref, g_ref, ml_ref, mc_ref, rw_ref, rb_ref, h_ref, lg_ref):
    h = _ada_norm(x_ref[0], g_ref[...], ml_ref[0], mc_ref[0], pl.program_id(1) * NORM_TM, 1)
    h_ref[0] = h.astype(h_ref.dtype)
    lg_ref[0] = jnp.dot(h, rw_ref[...], preferred_element_type=F32, precision=lax.Precision.HIGHEST) + rb_ref[...]


def norm_router(xs, g, mod, router_w, router_b):
    bsz = xs.shape[0]
    mod3 = mod.reshape(bsz + 1, 1, 6 * D_MODEL)
    rw = jnp.zeros((D_MODEL, LANE), F32).at[:, :N_EXPERTS].set(router_w)
    rb = jnp.zeros((1, LANE), F32).at[0, :N_EXPERTS].set(router_b)
    return pl.pallas_call(
        _norm_router_kernel,
        out_shape=(jax.ShapeDtypeStruct(xs.shape, BF16), jax.ShapeDtypeStruct((bsz, S_ALL, LANE), F32)),
        grid=(bsz, S_ALL // NORM_TM),
        in_specs=[pl.BlockSpec((1, NORM_TM, D_MODEL), lambda b, s: (b, s, 0)),
                  pl.BlockSpec((1, D_MODEL), lambda b, s: (0, 0)),
                  pl.BlockSpec((1, 1, 6 * D_MODEL), lambda b, s: (b, 0, 0)),
                  pl.BlockSpec((1, 1, 6 * D_MODEL), lambda b, s: (bsz, 0, 0)),
                  pl.BlockSpec((D_MODEL, LANE), lambda b, s: (0, 0)),
                  pl.BlockSpec((1, LANE), lambda b, s: (0, 0))],
        out_specs=(pl.BlockSpec((1, NORM_TM, D_MODEL), lambda b, s: (b, s, 0)),
                   pl.BlockSpec((1, NORM_TM, LANE), lambda b, s: (b, s, 0))),
        compiler_params=_cparams(("arbitrary", "arbitrary")),
        name="norm_router",
    )(xs, g.reshape(1, D_MODEL), mod3, mod3, rw, rb)


def _expert_kernel(be_ref, x_ref, w1_ref, w3_ref, w2_ref, o_ref):
    del be_ref
    x = x_ref[...]
    acc = None
    for f in range(D_FF // FFN_TF):
        cols = slice(f * FFN_TF, (f + 1) * FFN_TF)
        y = _swiglu_chunk(x, w1_ref[0, :, cols], w3_ref[0, :, cols], w2_ref[0, cols, :])
        acc = y if acc is None else acc + y
    o_ref[...] = acc


def grouped_experts(xb, block_e, w1, w3, w2):
    n_rows = xb.shape[0]
    const = pl.Buffered(1)
    grid_spec = pltpu.PrefetchScalarGridSpec(
        num_scalar_prefetch=1,
        grid=(n_rows // MOE_BLOCK,),
        in_specs=[pl.BlockSpec((MOE_BLOCK, D_MODEL), lambda i, be: (i, 0)),
                  pl.BlockSpec((1, D_MODEL, D_FF), lambda i, be: (be[i], 0, 0), pipeline_mode=const),
                  pl.BlockSpec((1, D_MODEL, D_FF), lambda i, be: (be[i], 0, 0), pipeline_mode=const),
                  pl.BlockSpec((1, D_FF, D_MODEL), lambda i, be: (be[i], 0, 0), pipeline_mode=const)],
        out_specs=pl.BlockSpec((MOE_BLOCK, D_MODEL), lambda i, be: (i, 0)),
    )
    return pl.pallas_call(
        _expert_kernel,
        out_shape=jax.ShapeDtypeStruct((n_rows, D_MODEL), F32),
        grid_spec=grid_spec,
        compiler_params=_cparams(("arbitrary",)),
        name="grouped_experts",
    )(block_e, xb, w1, w3, w2)


def moe_ffn(xs, g, mod, gt2, router_w, router_b, w1, w3, w2):
    bsz = xs.shape[0]
    h, logits = norm_router(xs, g, mod, router_w, router_b)
    h = h.reshape(-1, D_MODEL)
    n_tok = h.shape[0]
    top_logit, top_e = lax.top_k(logits.reshape(n_tok, LANE)[:, :N_EXPERTS], TOP_K)
    gate = jax.nn.softmax(top_logit, axis=-1)
    n_assign = n_tok * TOP_K
    flat_e = top_e.reshape(-1)
    flat_tok = jnp.repeat(jnp.arange(n_tok, dtype=jnp.int32), TOP_K)
    flat_gate = gate.reshape(-1)
    order = jnp.argsort(flat_e)
    sorted_e = flat_e[order]
    counts = jnp.bincount(flat_e, length=N_EXPERTS)
    padded = (counts + MOE_BLOCK - 1) // MOE_BLOCK * MOE_BLOCK
    start = jnp.cumsum(counts) - counts
    p_end = jnp.cumsum(padded)
    p_start = p_end - padded
    dest = p_start[sorted_e] + jnp.arange(n_assign, dtype=jnp.int32) - start[sorted_e]
    n_blocks = -(-n_assign // MOE_BLOCK) + N_EXPERTS
    n_rows = n_blocks * MOE_BLOCK
    row_tok = jnp.zeros((n_rows,), jnp.int32).at[dest].set(flat_tok[order])
    row_gate = jnp.zeros((n_rows,), F32).at[dest].set(flat_gate[order])
    block_e = jnp.minimum(jnp.searchsorted(p_end, jnp.arange(n_blocks) * MOE_BLOCK, side='right'),
                          N_EXPERTS - 1).astype(jnp.int32)
    yb = grouped_experts(h[row_tok], block_e, w1, w3, w2)
    y = jnp.zeros((n_tok, D_MODEL), F32).at[row_tok].add(yb * row_gate[:, None])
    return xs + gt2 * y.reshape(bsz, S_ALL, D_MODEL)


FINAL_TM = 256


def _final_kernel(x_ref, g_ref, o_ref):
    x = x_ref[0]
    o_ref[0] = x * lax.rsqrt(jnp.mean(x * x, axis=-1, keepdims=True) + EPS) * g_ref[...]


def final_norm(xs, g):
    bsz = xs.shape[0]
    skip = CTX_LEN // FINAL_TM
    return pl.pallas_call(
        _final_kernel,
        out_shape=jax.ShapeDtypeStruct((bsz, SEQ, D_MODEL), F32),
        grid=(bsz, SEQ // FINAL_TM),
        in_specs=[pl.BlockSpec((1, FINAL_TM, D_MODEL), lambda b, s: (b, s + skip, 0)),
                  pl.BlockSpec((1, D_MODEL), lambda b, s: (0, 0))],
        out_specs=pl.BlockSpec((1, FINAL_TM, D_MODEL), lambda b, s: (b, s, 0)),
        compiler_params=_cparams(("arbitrary", "arbitrary")),
        name="final_norm",
    )(xs, g.reshape(1, D_MODEL))


N_CHUNKS = S_ALL // CHUNK
CTX_CHUNKS = CTX_LEN // CHUNK
GATE_ROWS = 8
NEG_INF = float("-inf")


def _softplus(v):
    return jnp.maximum(v, 0.0) + jnp.log(1.0 + jnp.exp(-jnp.abs(v)))


def _scan_constants():
    s = np.arange(CHUNK)[:, None]
    t = np.arange(CHUNK)[None, :]
    tri = np.concatenate([(s <= t), (s >= t), np.ones((CHUNK, CHUNK), bool)], axis=1).astype(np.float32)
    return jnp.asarray(tri), jnp.asarray(np.eye(CHUNK, dtype=np.float32))


def _rev_chunk(j):
    return jnp.where(j < CTX_CHUNKS, CTX_CHUNKS - 1 - j, N_CHUNKS + CTX_CHUNKS - 1 - j)


def _head_gate_rows(small, c0, heads, kind_major):
    bsz = small.shape[0]
    a = small[:, :, c0:c0 + 4 * heads].reshape(bsz, N_CHUNKS, CHUNK, 2, 2, heads)
    a = a.transpose((0, 5, 1, 4, 3, 2) if kind_major else (0, 5, 1, 3, 4, 2))
    a = a.reshape(bsz, heads, N_CHUNKS, 4, CHUNK)
    return jnp.pad(a, ((0, 0), (0, 0), (0, 0), (0, GATE_ROWS - 4), (0, 0)))


def _tn_dot(a, b):
    return lax.dot_general(a, b, (((0,), (0,)), ((), ())), preferred_element_type=F32)


def _nt_dot(a, b, precision=None):
    return lax.dot_general(a, b, (((1,), (1,)), ((), ())), preferred_element_type=F32, precision=precision)


def _dir_mask(d):
    t = lax.broadcasted_iota(jnp.int32, (CHUNK, CHUNK), 0)
    s = lax.broadcasted_iota(jnp.int32, (CHUNK, CHUNK), 1)
    return (s <= t, s < t) if d == 0 else (s >= t, s > t)


def _split_dot(a, ones, dot_fn):
    b = ones.astype(BF16)
    hi = a.astype(BF16)
    lo = (a - hi.astype(F32)).astype(BF16)
    return dot_fn(hi, b, preferred_element_type=F32) + dot_fn(lo, b, preferred_element_type=F32)


def _gate_forms(vals, tri, eye):
    cum = _split_dot(vals, tri, jnp.dot)
    kind = lax.broadcasted_iota(jnp.int32, vals.shape, 0) & (GATE_ROWS - 1)
    rows = jnp.where(kind == 1, cum[:, :CHUNK], jnp.where(kind == 3, cum[:, CHUNK:2 * CHUNK], vals))
    cols = _split_dot(rows, eye, lambda a, b, **kw: _nt_dot(b, a))
    return rows, cols, cum[:, 2 * CHUNK:2 * CHUNK + 1]


def _x_row(rows, d, g=0):
    return rows[GATE_ROWS * g + 2 * d:GATE_ROWS * g + 2 * d + 1]


def _run_row(rows, d, g=0):
    return rows[GATE_ROWS * g + 2 * d + 1:GATE_ROWS * g + 2 * d + 2]


def _x_col(cols, d, g=0):
    return cols[:, GATE_ROWS * g + 2 * d:GATE_ROWS * g + 2 * d + 1]


def _run_col(cols, d, g=0):
    return cols[:, GATE_ROWS * g + 2 * d + 1:GATE_ROWS * g + 2 * d + 2]


def _mlstm_kernel(q_ref, k_ref, v_ref, o_ref, g_ref, fb_ref, ng_ref, tri_ref, eye_ref, y_ref,
                  c_s, st_s, gr_s, gc_s):
    tri, eye = tri_ref[...], eye_ref[...]
    fbias = fb_ref[0]
    scale = M_DQK ** -0.5
    rowid = lax.broadcasted_iota(jnp.int32, (GATE_ROWS, CHUNK), 0)
    st_row = lax.broadcasted_iota(jnp.int32, (GATE_ROWS, M_DQK), 0)

    def chunk_rows(c):
        return pl.ds(pl.multiple_of(c * CHUNK, CHUNK), CHUNK)

    def state_rows(c):
        return pl.ds(pl.multiple_of(c * M_DQK, M_DQK), M_DQK)

    def local(c, carry):
        raw = g_ref[0, 0, c]
        vals = jnp.where((rowid & 1) == 1, -_softplus(-(raw + fbias)), raw)
        kf, v = k_ref[0, chunk_rows(c), :].astype(F32), v_ref[0, chunk_rows(c), :]
        rows, cols, totals = _gate_forms(vals, tri, eye)
        gr_s[c] = rows
        gc_s[c] = cols
        kws = []
        for d in range(2):
            tot = totals[2 * d + 1:2 * d + 2]
            m_loc = jnp.max(tot - _run_row(rows, d) + _x_row(rows, d), axis=-1, keepdims=True)
            kw = kf * (jnp.exp(tot - _run_col(cols, d) + _x_col(cols, d) - m_loc) * scale)
            kws.append(kw.astype(BF16))
            st_s[d, c] = jnp.where(st_row == 0, jnp.sum(kw, axis=0, keepdims=True),
                                   jnp.where(st_row == 1, m_loc, tot))
        c_loc = _tn_dot(jnp.concatenate(kws, axis=1), v)
        c_s[0, state_rows(c), :] = c_loc[:M_DQK]
        c_s[1, state_rows(c), :] = c_loc[M_DQK:]
        return carry

    lax.fori_loop(0, N_CHUNKS, local, 0, unroll=2)

    def scan(d):
        def body(j, carry):
            c_prev, n_prev, m_prev = carry
            c = j if d == 0 else _rev_chunk(j)
            st = st_s[d, c]
            n_loc, m_loc, tot = st[0:1], st[1:2, :1], st[2:3, :1]
            c_loc = c_s[d, state_rows(c), :]
            c_s[d, state_rows(c), :] = c_prev
            st_s[d, c] = jnp.where(st_row == 0, n_prev, m_prev)
            m_new = jnp.maximum(tot + m_prev, m_loc)
            s_prev = jnp.exp(tot + m_prev - m_new)
            s_loc = jnp.exp(m_loc - m_new)
            return s_prev * c_prev + s_loc * c_loc, s_prev * n_prev + s_loc * n_loc, m_new

        lax.fori_loop(0, N_CHUNKS, body, (jnp.zeros((M_DQK, M_DV), F32), jnp.zeros((1, M_DQK), F32),
                                          jnp.zeros((1, 1), F32)))

    scan(0)
    scan(1)

    def output(c, carry):
        q, k, v = q_ref[0, chunk_rows(c), :], k_ref[0, chunk_rows(c), :], v_ref[0, chunk_rows(c), :]
        qk = _nt_dot(q, k) * scale
        qf = q.astype(F32)
        rows, cols = gr_s[c], gc_s[c]
        intra, inter = [], []
        for d in range(2):
            st = st_s[d, c]
            b_c = _run_col(cols, d)
            n0, m0 = st[0:1], st[1:2, :1]
            incl, _ = _dir_mask(d)
            d_log = jnp.where(incl, b_c - _run_row(rows, d) + _x_row(rows, d), NEG_INF)
            m_inter = b_c + m0
            m_comb = jnp.maximum(m_inter, jnp.max(d_log, axis=-1, keepdims=True))
            s = qk * jnp.exp(d_log - m_comb)
            e_inter = jnp.exp(m_inter - m_comb)
            den = jnp.sum(s, axis=-1, keepdims=True) + e_inter * jnp.sum(qf * n0, axis=-1, keepdims=True)
            inv = 1.0 / jnp.maximum(jnp.abs(den), jnp.exp(-m_comb))
            intra.append(s * inv)
            inter.append(qf * (e_inter * inv))
        c_in = jnp.concatenate([c_s[0, state_rows(c), :], c_s[1, state_rows(c), :]], axis=0).astype(BF16)
        h = (jnp.dot(jnp.concatenate(intra, axis=1).astype(BF16), jnp.concatenate([v, v], axis=0),
                     preferred_element_type=F32)
             + jnp.dot(jnp.concatenate(inter, axis=1).astype(BF16), c_in, preferred_element_type=F32))
        h = h * lax.rsqrt(jnp.mean(h * h, axis=-1, keepdims=True) + EPS) * ng_ref[...]
        y_ref[0, chunk_rows(c), :] = (h * _sigmoid(o_ref[0, chunk_rows(c), :].astype(F32))).astype(y_ref.dtype)
        return carry

    lax.fori_loop(0, N_CHUNKS, output, 0, unroll=2)


def mlstm_mixer(big, small, f_bias, norm_g):
    bsz = big.shape[0]
    gates = _head_gate_rows(small, 0, M_HEADS, False)
    fb = jnp.zeros((M_HEADS, GATE_ROWS, CHUNK), F32)
    fb = fb.at[:, 1].set(f_bias[0][:, None]).at[:, 3].set(f_bias[1][:, None])
    tri, eye = _scan_constants()
    qb, kb, vb, ob = C_MQ // M_DQK, C_MK // M_DQK, C_MV // M_DV, C_MO // M_DV
    return pl.pallas_call(
        _mlstm_kernel,
        out_shape=jax.ShapeDtypeStruct((bsz, S_ALL, BRANCH_W), BF16),
        grid=(bsz, M_HEADS),
        in_specs=[pl.BlockSpec((1, S_ALL, M_DQK), lambda b, h: (b, 0, qb + h)),
                  pl.BlockSpec((1, S_ALL, M_DQK), lambda b, h: (b, 0, kb + h)),
                  pl.BlockSpec((1, S_ALL, M_DV), lambda b, h: (b, 0, vb + h)),
                  pl.BlockSpec((1, S_ALL, M_DV), lambda b, h: (b, 0, ob + h)),
                  pl.BlockSpec((1, 1, N_CHUNKS, GATE_ROWS, CHUNK), lambda b, h: (b, h, 0, 0, 0)),
                  pl.BlockSpec((1, GATE_ROWS, CHUNK), lambda b, h: (h, 0, 0)),
                  pl.BlockSpec((1, M_DV), lambda b, h: (0, h)),
                  pl.BlockSpec((CHUNK, 3 * CHUNK), lambda b, h: (0, 0)),
                  pl.BlockSpec((CHUNK, CHUNK), lambda b, h: (0, 0))],
        out_specs=pl.BlockSpec((1, S_ALL, M_DV), lambda b, h: (b, 0, h)),
        scratch_shapes=[pltpu.VMEM((2, N_CHUNKS * M_DQK, M_DV), F32),
                        pltpu.VMEM((2, N_CHUNKS, GATE_ROWS, M_DQK), F32),
                        pltpu.VMEM((N_CHUNKS, GATE_ROWS, CHUNK), F32),
                        pltpu.VMEM((N_CHUNKS, CHUNK, GATE_ROWS), F32)],
        compiler_params=_cparams(("arbitrary", "arbitrary")),
        name="mlstm_mixer",
    )(big, big, big, big, gates, fb, norm_g.reshape(1, BRANCH_W), tri, eye)


CONV_BLK = 256
CONV_HALO = 16
SOLVE_BLK = 2
SOLVE_GROUP = 4


def _conv_silu(src_ref, w, dst_ref, l2_scale):
    n_blk = S_ALL // CONV_BLK
    zeros = jnp.zeros((CONV_HALO, src_ref.shape[-1]), F32)
    for i in range(n_blk):
        lo = i * CONV_BLK
        first = i == 0 or lo == CTX_LEN
        last = lo + CONV_BLK in (CTX_LEN, S_ALL)
        body = src_ref[0, lo - (0 if first else CONV_HALO):lo + CONV_BLK + (0 if last else CONV_HALO), :]
        win = jnp.concatenate(([zeros] if first else []) + [body.astype(F32)] + ([zeros] if last else []), axis=0)
        n = win.shape[0]
        y = None
        for j in range(CONV_W):
            tap = win if j == CONV_W // 2 else pltpu.roll(win, (CONV_W // 2 - j) % n, 0)
            term = tap[CONV_HALO:CONV_HALO + CONV_BLK] * w[j:j + 1, :]
            y = term if y is None else y + term
        y = y * _sigmoid(y)
        if l2_scale is not None:
            y = y * (lax.rsqrt(jnp.sum(y * y, axis=-1, keepdims=True) + EPS) * l2_scale)
        dst_ref[lo:lo + CONV_BLK, :] = y.astype(dst_ref.dtype)


def _block_masks():
    n = SOLVE_GROUP * CHUNK
    t = np.arange(n)[:, None]
    s = np.arange(n)[None, :]
    masks = [(t // SOLVE_BLK) == (s // SOLVE_BLK)]
    k = SOLVE_BLK
    while k < CHUNK:
        masks.append(((t // (2 * k)) == (s // (2 * k))) & ((t // k) != (s // k)))
        k *= 2
    return jnp.asarray(np.stack(masks).astype(np.float32))


def _unit_triangular_inverse(a, eye, masks):
    t = eye - a * masks[0]
    for lvl in range(1, masks.shape[0]):
        tb = t.astype(BF16)
        tl = jnp.dot(tb, (a * masks[lvl]).astype(BF16), preferred_element_type=F32)
        t = t - jnp.dot(tl.astype(BF16), tb, preferred_element_type=F32)
    return t


def _block_diag(blocks):
    n = len(blocks)
    z_half = jnp.zeros((CHUNK, CHUNK), F32)
    z_tile = jnp.zeros((CHUNK, 2 * CHUNK), F32)
    out = []
    for p, blk in enumerate(blocks):
        pair = jnp.concatenate([blk, z_half] if p % 2 == 0 else [z_half, blk], axis=1)
        out.append(jnp.concatenate([pair if t == p // 2 else z_tile for t in range(n // 2)], axis=1))
    return jnp.concatenate(out, axis=0)


def _gdn_kernel(xq_ref, xk_ref, xv_ref, z_ref, g_ref, gp_ref, cw_ref, ng_ref, tri_ref, eye_ref, bm_ref, y_ref,
                q_s, k_s, v_s, u_s, wq_s, at_s, kt_s, cd_s, acc_ref, s_ref):
    tri, eye, masks = tri_ref[...], eye_ref[...], bm_ref[...]
    cw = cw_ref[0]
    _conv_silu(xq_ref, cw[:, :G_DK], q_s, G_DK ** -0.5)
    _conv_silu(xk_ref, cw[:, G_DK:2 * G_DK], k_s, 1.0)
    _conv_silu(xv_ref, cw[:, 2 * G_DK:], v_s, None)
    s_ref[...] = jnp.zeros_like(s_ref)
    n_pair = SOLVE_GROUP // 2
    kind = lax.broadcasted_iota(jnp.int32, (n_pair * GATE_ROWS, CHUNK), 0) & 1
    neg_a = jnp.concatenate([gp_ref[0, 0]] * n_pair, axis=0)
    dt_bias = jnp.concatenate([gp_ref[0, 1]] * n_pair, axis=0)
    n_bd = SOLVE_GROUP * CHUNK
    eye_bd = (lax.broadcasted_iota(jnp.int32, (n_bd, n_bd), 0)
              == lax.broadcasted_iota(jnp.int32, (n_bd, n_bd), 1)).astype(F32)
    z_half = jnp.zeros((CHUNK, CHUNK), F32)

    def chunk_rows(c):
        return pl.ds(pl.multiple_of(c * CHUNK, CHUNK), CHUNK)

    def wq_rows(c):
        return pl.ds(pl.multiple_of(c * 2 * CHUNK, 2 * CHUNK), 2 * CHUNK)

    def prep(g, carry):
        c0 = g * n_pair
        raw = jnp.concatenate([g_ref[0, 0, c0 + i] for i in range(n_pair)], axis=0)
        vals = jnp.where(kind == 1, neg_a * _softplus(raw + dt_bias), _sigmoid(raw))
        g_rows, g_cols, totals = _gate_forms(vals, tri, eye)
        blocks, rhs, q_dec = [], [], []
        for i in range(n_pair):
            c = c0 + i
            q, k, v = q_s[chunk_rows(c), :], k_s[chunk_rows(c), :], v_s[chunk_rows(c), :]
            kf, vf, qf = k.astype(F32), v.astype(F32), q.astype(F32)
            kq = _nt_dot(jnp.concatenate([k, q], axis=0), k)
            kk, qk = kq[:CHUNK], kq[CHUNK:]
            for d in range(2):
                gam_r, beta_c, gam_c = _run_row(g_rows, d, i), _x_col(g_cols, d, i), _run_col(g_cols, d, i)
                tot = totals[GATE_ROWS * i + 2 * d + 1:GATE_ROWS * i + 2 * d + 2]
                incl, strict = _dir_mask(d)
                decay = jnp.exp(jnp.where(incl, gam_c - gam_r, NEG_INF))
                blocks.append(jnp.where(strict, kk * beta_c * decay, 0.0))
                e_gam = jnp.exp(gam_c)
                rhs.append(jnp.concatenate([vf * beta_c, kf * (beta_c * e_gam)], axis=-1))
                q_dec.append(qf * e_gam)
                attn = qk * decay
                at_s[d, chunk_rows(c), :] = jnp.concatenate([attn, z_half] if d == 0 else [z_half, attn],
                                                            axis=1).astype(BF16)
                kt_s[d, chunk_rows(c), :] = (kf * jnp.exp(tot - gam_c)).astype(BF16)
                cd_s[d, c] = jnp.broadcast_to(jnp.exp(tot), cd_s.shape[2:])
        t_inv = _unit_triangular_inverse(_block_diag(blocks), eye_bd, masks)
        x = jnp.dot(t_inv.astype(BF16), jnp.concatenate(rhs, axis=0).astype(BF16),
                    preferred_element_type=F32)
        for i in range(n_pair):
            for d in range(2):
                p = 2 * i + d
                xp = x[p * CHUNK:(p + 1) * CHUNK]
                u_s[d, chunk_rows(c0 + i), :] = xp[:, :G_DV]
                wq_s[d, wq_rows(c0 + i), :] = jnp.concatenate([xp[:, G_DV:], q_dec[p]], axis=0).astype(BF16)
        return carry

    lax.fori_loop(0, N_CHUNKS // n_pair, prep, 0)

    z_wq = jnp.zeros((2 * CHUNK, G_DK), BF16)

    def step(j, carry):
        cs = (j, _rev_chunk(j))
        s_prev = (s_ref[0], s_ref[1])
        wq_bd = jnp.concatenate([jnp.concatenate([wq_s[0, wq_rows(cs[0]), :], z_wq], axis=1),
                                 jnp.concatenate([z_wq, wq_s[1, wq_rows(cs[1]), :]], axis=1)], axis=0)
        ws_qs = jnp.dot(wq_bd, jnp.concatenate(s_prev, axis=0).astype(BF16), preferred_element_type=F32)
        v_new = [(u_s[d, chunk_rows(cs[d]), :] - ws_qs[2 * d * CHUNK:(2 * d + 1) * CHUNK]).astype(BF16)
                 for d in range(2)]
        at_bd = jnp.concatenate([at_s[d, chunk_rows(cs[d]), :] for d in range(2)], axis=0)
        o = jnp.dot(at_bd, jnp.concatenate(v_new, axis=0), preferred_element_type=F32)
        kt = jnp.concatenate([kt_s[d, chunk_rows(cs[d]), :] for d in range(2)], axis=1)
        ktv = _tn_dot(kt, jnp.concatenate(v_new, axis=1))
        for d in range(2):
            acc_ref[d, chunk_rows(cs[d]), :] = (ws_qs[(2 * d + 1) * CHUNK:(2 * d + 2) * CHUNK]
                                                + o[d * CHUNK:(d + 1) * CHUNK])
            s_ref[d] = (cd_s[d, cs[d]][:1, :] * s_prev[d]
                        + ktv[d * G_DK:(d + 1) * G_DK, d * G_DV:(d + 1) * G_DV])
        return carry

    lax.fori_loop(0, N_CHUNKS, step, 0)

    def finish(c, carry):
        rows = pl.ds(pl.multiple_of(c * CHUNK, CHUNK), CHUNK)
        h = acc_ref[0, rows, :] + acc_ref[1, rows, :]
        h = h * lax.rsqrt(jnp.mean(h * h, axis=-1, keepdims=True) + EPS) * ng_ref[...]
        z = z_ref[0, rows, :].astype(F32)
        y_ref[0, rows, :] = (h * (z * _sigmoid(z))).astype(y_ref.dtype)
        return carry

    lax.fori_loop(0, N_CHUNKS, finish, 0)


def gdn_mixer(big, small, conv_w, a_log, dt_bias, norm_g):
    bsz = big.shape[0]
    gates = _head_gate_rows(small, 4 * M_HEADS, G_HEADS, True)
    gp = jnp.zeros((G_HEADS, 2, GATE_ROWS, CHUNK), F32)
    for d in range(2):
        gp = gp.at[:, 0, 2 * d + 1].set(-jnp.exp(a_log[d])[:, None]).at[:, 1, 2 * d + 1].set(dt_bias[d][:, None])
    cw = conv_w.reshape(CONV_W, 3, G_HEADS, G_DK).transpose(2, 0, 1, 3).reshape(G_HEADS, CONV_W, 3 * G_DK)
    cw = jnp.pad(cw, ((0, 0), (0, 8 - CONV_W), (0, 0)))
    tri, eye = _scan_constants()
    masks = _block_masks()
    qb, kb, vb, zb = C_GQ // G_DK, C_GK // G_DK, C_GV // G_DV, C_GZ // G_DV
    return pl.pallas_call(
        _gdn_kernel,
        out_shape=jax.ShapeDtypeStruct((bsz, S_ALL, BRANCH_W), BF16),
        grid=(bsz, G_HEADS),
        in_specs=[pl.BlockSpec((1, S_ALL, G_DK), lambda b, h: (b, 0, qb + h)),
                  pl.BlockSpec((1, S_ALL, G_DK), lambda b, h: (b, 0, kb + h)),
                  pl.BlockSpec((1, S_ALL, G_DV), lambda b, h: (b, 0, vb + h)),
                  pl.BlockSpec((1, S_ALL, G_DV), lambda b, h: (b, 0, zb + h)),
                  pl.BlockSpec((1, 1, N_CHUNKS, GATE_ROWS, CHUNK), lambda b, h: (b, h, 0, 0, 0)),
                  pl.BlockSpec((1, 2, GATE_ROWS, CHUNK), lambda b, h: (h, 0, 0, 0)),
                  pl.BlockSpec((1, 8, 3 * G_DK), lambda b, h: (h, 0, 0)),
                  pl.BlockSpec((1, G_DV), lambda b, h: (0, 0)),
                  pl.BlockSpec((CHUNK, 3 * CHUNK), lambda b, h: (0, 0)),
                  pl.BlockSpec((CHUNK, CHUNK), lambda b, h: (0, 0)),
                  pl.BlockSpec(masks.shape, lambda b, h: (0, 0, 0))],
        out_specs=pl.BlockSpec((1, S_ALL, G_DV), lambda b, h: (b, 0, h)),
        scratch_shapes=[pltpu.VMEM((S_ALL, G_DK), BF16), pltpu.VMEM((S_ALL, G_DK), BF16),
                        pltpu.VMEM((S_ALL, G_DV), BF16),
                        pltpu.VMEM((2, S_ALL, G_DV), F32),
                        pltpu.VMEM((2, 2 * S_ALL, G_DK), BF16),
                        pltpu.VMEM((2, S_ALL, 2 * CHUNK), BF16),
                        pltpu.VMEM((2, S_ALL, G_DK), BF16),
                        pltpu.VMEM((2, N_CHUNKS, 8, G_DV), F32),
                        pltpu.VMEM((2, S_ALL, G_DV), F32),
                        pltpu.VMEM((2, G_DK, G_DV), F32)],
        compiler_params=_cparams(("arbitrary", "arbitrary")),
        name="gdn_mixer",
    )(big, big, big, big, gates, gp, cw, norm_g.reshape(1, G_DV), tri, eye, masks)


GRID_H = SEQ // GRID_W
LRU_BLK = 256


def _shift_rows(a, s):
    n = a.shape[0]
    r = lax.broadcasted_iota(jnp.int32, a.shape, 0)
    return jnp.where((r >= s) & (r < n + s), pltpu.roll(a, s % n, 0), 0.0)


def _gelu(v):
    return 0.5 * v * (1.0 + jnp.tanh(0.7978845608028654 * (v + 0.044715 * v * v * v)))


def _lru_kernel(x_ref, y_ref, cw_ref, cb_ref, gw_ref, gb_ref, lam_ref, o_ref,
                xb_s, a_s, u_s, h_s, p_s, end_s, cin_s):
    cw = cw_ref[...]
    xf = x_ref[0].astype(F32)
    xc = xf[:CTX_LEN]
    lat = lambda r0, r1: xf[CTX_LEN + r0 * GRID_W:CTX_LEN + r1 * GRID_W]
    prev1 = jnp.concatenate([_shift_rows(lat(GRID_H - 1, GRID_H), 1), lat(0, GRID_H - 1)], axis=0)
    prev2 = jnp.concatenate([_shift_rows(lat(GRID_H - 2, GRID_H - 1), 1), _shift_rows(lat(GRID_H - 1, GRID_H), 1),
                             lat(0, GRID_H - 2)], axis=0)
    next1 = jnp.concatenate([lat(1, GRID_H), _shift_rows(lat(0, 1), -1)], axis=0)
    taps_c = (_shift_rows(xc, 2), _shift_rows(xc, 1), xc, _shift_rows(xc, -1))
    taps_l = (prev2, prev1, lat(0, GRID_H), next1)
    xb_s[:CTX_LEN, :] = sum(t * cw[j:j + 1, :] for j, t in enumerate(taps_c)) + cb_ref[...]
    xb_s[CTX_LEN:, :] = sum(t * cw[j:j + 1, :] for j, t in enumerate(taps_l)) + cb_ref[...]

    neg_c_sp = -LRU_C * _softplus(-lam_ref[...])

    def gates(i, carry):
        rows = pl.ds(pl.multiple_of(i * LRU_BLK, LRU_BLK), LRU_BLK)
        xb = xb_s[rows, :]
        pre = jnp.dot(xb.astype(BF16), gw_ref[0], preferred_element_type=F32) + gb_ref[0]
        for d in range(2):
            r = _sigmoid(pre[:, (2 * d) * LANE:(2 * d + 1) * LANE])
            g_in = _sigmoid(pre[:, (2 * d + 1) * LANE:(2 * d + 2) * LANE])
            log_a = neg_c_sp[d:d + 1, :] * r
            a = jnp.exp(log_a)
            a_s[d, rows, :] = a
            u_s[d, rows, :] = jnp.sqrt(jnp.maximum(1.0 - a * a, 0.0)) * g_in * xb
        return carry

    lax.fori_loop(0, S_ALL // LRU_BLK, gates, 0)

    def ctx_step(t, carry):
        hf, hb = carry
        tb = CTX_LEN - 1 - t
        hf = a_s[0, pl.ds(t, 1), :] * hf + u_s[0, pl.ds(t, 1), :]
        hb = a_s[1, pl.ds(tb, 1), :] * hb + u_s[1, pl.ds(tb, 1), :]
        h_s[0, pl.ds(t, 1), :] = hf
        h_s[1, pl.ds(tb, 1), :] = hb
        return hf, hb

    zero_row = jnp.zeros((1, LANE), F32)
    hf0, hb0 = lax.fori_loop(0, CTX_LEN, ctx_step, (zero_row, zero_row))

    def col_step(i, carry):
        hf, pf, hb, pb = carry
        rf = pl.ds(pl.multiple_of(CTX_LEN + i * GRID_W, GRID_W), GRID_W)
        rb = pl.ds(pl.multiple_of(CTX_LEN + (GRID_H - 1 - i) * GRID_W, GRID_W), GRID_W)
        af, ab = a_s[0, rf, :], a_s[1, rb, :]
        hf = af * hf + u_s[0, rf, :]
        pf = af * pf
        hb = ab * hb + u_s[1, rb, :]
        pb = ab * pb
        h_s[0, rf, :] = hf
        p_s[0, pl.ds(pl.multiple_of(i * GRID_W, GRID_W), GRID_W), :] = pf
        h_s[1, rb, :] = hb
        p_s[1, pl.ds(pl.multiple_of((GRID_H - 1 - i) * GRID_W, GRID_W), GRID_W), :] = pb
        return hf, pf, hb, pb

    zeros = jnp.zeros((GRID_W, LANE), F32)
    ones = jnp.ones((GRID_W, LANE), F32)
    hf, pf, hb, pb = lax.fori_loop(0, GRID_H, col_step, (zeros, ones, zeros, ones))
    end_s[0], end_s[1], end_s[2], end_s[3] = hf, pf, hb, pb

    def chain_step(w, carry):
        cf, cb = carry
        wb = GRID_W - 1 - w
        cin_s[0, pl.ds(w, 1), :] = cf
        cin_s[1, pl.ds(wb, 1), :] = cb
        cf = end_s[0, pl.ds(w, 1), :] + end_s[1, pl.ds(w, 1), :] * cf
        cb = end_s[2, pl.ds(wb, 1), :] + end_s[3, pl.ds(wb, 1), :] * cb
        return cf, cb

    lax.fori_loop(0, GRID_W, chain_step, (hf0, hb0))

    o_ref[0, :CTX_LEN, :] = ((h_s[0, :CTX_LEN, :] + h_s[1, :CTX_LEN, :])
                             * _gelu(y_ref[0, :CTX_LEN, :].astype(F32))).astype(o_ref.dtype)

    def out_step(i, carry):
        rows = pl.ds(pl.multiple_of(CTX_LEN + i * GRID_W, GRID_W), GRID_W)
        prow = pl.ds(pl.multiple_of(i * GRID_W, GRID_W), GRID_W)
        h = (h_s[0, rows, :] + p_s[0, prow, :] * cin_s[0] + h_s[1, rows, :] + p_s[1, prow, :] * cin_s[1])
        o_ref[0, rows, :] = (h * _gelu(y_ref[0, rows, :].astype(F32))).astype(o_ref.dtype)
        return carry

    lax.fori_loop(0, GRID_H, out_step, 0)


def lru_mixer(big, conv_w, conv_b, gate_w, gate_b, lam):
    bsz = big.shape[0]
    gw = gate_w.transpose(2, 3, 0, 1, 4).reshape(R_BLOCKS, R_BDIM, 4 * R_BDIM).astype(BF16)
    gb = gate_b.reshape(2, 2, R_BLOCKS, R_BDIM).transpose(2, 0, 1, 3).reshape(R_BLOCKS, 1, 4 * R_BDIM)
    cw = jnp.pad(conv_w, ((0, 8 - CONV_W), (0, 0)))
    xb, yb = C_RX // R_BDIM, C_RY // R_BDIM
    return pl.pallas_call(
        _lru_kernel,
        out_shape=jax.ShapeDtypeStruct((bsz, S_ALL, BRANCH_W), BF16),
        grid=(bsz, R_BLOCKS),
        in_specs=[pl.BlockSpec((1, S_ALL, R_BDIM), lambda b, n: (b, 0, xb + n)),
                  pl.BlockSpec((1, S_ALL, R_BDIM), lambda b, n: (b, 0, yb + n)),
                  pl.BlockSpec((8, R_BDIM), lambda b, n: (0, n)),
                  pl.BlockSpec((1, R_BDIM), lambda b, n: (0, n)),
                  pl.BlockSpec((1, R_BDIM, 4 * R_BDIM), lambda b, n: (n, 0, 0)),
                  pl.BlockSpec((1, 1, 4 * R_BDIM), lambda b, n: (n, 0, 0)),
                  pl.BlockSpec((2, R_BDIM), lambda b, n: (0, n))],
        out_specs=pl.BlockSpec((1, S_ALL, R_BDIM), lambda b, n: (b, 0, n)),
        scratch_shapes=[pltpu.VMEM((S_ALL, R_BDIM), F32), pltpu.VMEM((2, S_ALL, R_BDIM), F32),
                        pltpu.VMEM((2, S_ALL, R_BDIM), F32), pltpu.VMEM((2, S_ALL, R_BDIM), F32),
                        pltpu.VMEM((2, SEQ, R_BDIM), F32), pltpu.VMEM((4, GRID_W, R_BDIM), F32),
                        pltpu.VMEM((2, GRID_W, R_BDIM), F32)],
        compiler_params=_cparams(("arbitrary", "arbitrary")),
        name="lru_mixer",
    )(big, big, cw, conv_b.reshape(1, BRANCH_W), gw, gb, lam)


def _pack_in_weights(w, b):
    segs_w = [w[:, IN_OFFS[i]:IN_OFFS[i + 1]] for i in range(len(IN_COLS))]
    segs_b = [b[IN_OFFS[i]:IN_OFFS[i + 1]] for i in range(len(IN_COLS))]
    w_big = jnp.concatenate([segs_w[i] for i in BIG_SEGS], axis=1).astype(BF16)
    b_big = jnp.concatenate([segs_b[i] for i in BIG_SEGS]).reshape(1, N_BIG)
    pad = N_SMALL - sum(IN_COLS[i] for i in SMALL_SEGS)
    w_small = jnp.pad(jnp.concatenate([segs_w[i] for i in SMALL_SEGS], axis=1), ((0, 0), (0, pad))).astype(BF16)
    b_small = jnp.pad(jnp.concatenate([segs_b[i] for i in SMALL_SEGS]), (0, pad)).reshape(1, N_SMALL)
    return w_big, b_big, w_small, b_small


def kernel(x, c, ctx, c_ctx, mod_w, mod_b, norm1_g, norm2_g, final_g, w_in, b_in, m_fbias, m_norm_g,
           g_conv, g_a_log, g_dt_bias, g_norm_g, r_conv, r_conv_b, r_gate_w, r_gate_b, r_lambda,
           w_branch, w_out, ffn_w1, ffn_w3, ffn_w2, router_w, router_b, moe_w1, moe_w3, moe_w2):
    bsz = x.shape[0]
    xs = jnp.concatenate([ctx, x], axis=1)
    src = jnp.concatenate([c, c_ctx[None]], axis=0)
    mods = modulation(src, mod_w, mod_b)
    for layer in range(DEPTH):
        mod = mods[layer]
        w_big, b_big, w_small, b_small = _pack_in_weights(w_in[layer], b_in[layer])
        big, small = in_projection(xs, norm1_g[layer], mod, w_big, b_big, w_small, b_small)
        ym = mlstm_mixer(big, small, m_fbias[layer], m_norm_g[layer])
        yg = gdn_mixer(big, small, g_conv[layer], g_a_log[layer], g_dt_bias[layer], g_norm_g[layer])
        yr = lru_mixer(big, r_conv[layer], r_conv_b[layer], r_gate_w[layer], r_gate_b[layer], r_lambda[layer])
        xs = merge_branches(ym, yg, yr, big, xs, mod, w_branch[layer].astype(BF16), w_out[layer].astype(BF16))
        j = layer // 2
        if layer % 2 == 0:
            xs = dense_ffn(xs, norm2_g[layer], mod, ffn_w1[j].astype(BF16), ffn_w3[j].astype(BF16),
                           ffn_w2[j].astype(BF16))
        else:
            gt2 = jnp.concatenate([jnp.broadcast_to(mod[bsz:, None, 5 * D_MODEL:], (bsz, CTX_LEN, D_MODEL)),
                                   jnp.broadcast_to(mod[:bsz, None, 5 * D_MODEL:], (bsz, SEQ, D_MODEL))], axis=1)
            xs = moe_ffn(xs, norm2_g[layer], mod, gt2, router_w[j], router_b[j], moe_w1[j].astype(BF16),
                         moe_w3[j].astype(BF16), moe_w2[j].astype(BF16))
    return final_norm(xs, final_g)
```

```python
import jax
import jax.numpy as jnp
import numpy as np
from jax import lax
from jax.experimental import pallas as pl
from jax.experimental.pallas import tpu as pltpu

D_MODEL = 1024
SEQ = 2048
DEPTH = 4
GRID_W = 64
CTX_LEN = 256
S_ALL = CTX_LEN + SEQ
BRANCH_W = D_MODEL
N_BRANCH = 3
CHUNK = 64
CONV_W = 4
M_HEADS = 4
M_DQK = 128
M_DV = BRANCH_W // M_HEADS
G_HEADS = 8
G_DK = 128
G_DV = BRANCH_W // G_HEADS
R_BLOCKS = 8
R_BDIM = BRANCH_W // R_BLOCKS
LRU_C = 8.0
D_FF = 2816
N_EXPERTS = 8
TOP_K = 2
MOE_BLOCK = 512
EPS = 1e-6
IN_COLS = (M_HEADS * M_DQK, M_HEADS * M_DQK, BRANCH_W, BRANCH_W, 4 * M_HEADS,
           2 * G_HEADS * G_DK + BRANCH_W, BRANCH_W, 2 * G_HEADS, 2 * G_HEADS,
           BRANCH_W, BRANCH_W, N_BRANCH * D_MODEL)
IN_OFFS = tuple(int(v) for v in np.cumsum((0,) + IN_COLS))
SMALL_SEGS = (4, 7, 8)
BIG_SEGS = tuple(i for i in range(len(IN_COLS)) if i not in SMALL_SEGS)
N_BIG = sum(IN_COLS[i] for i in BIG_SEGS)
N_SMALL = 128
C_MQ, C_MK, C_MV, C_MO = 0, 512, 1024, 2048
C_GQ, C_GK, C_GV, C_GZ = 3072, 4096, 5120, 6144
C_RX, C_RY, C_MG = 7168, 8192, 9216

LANE = 128
VMEM_LIMIT = 56 * 1024 * 1024
F32 = jnp.float32
BF16 = jnp.bfloat16


def _cparams(sem):
    return pltpu.CompilerParams(dimension_semantics=sem, vmem_limit_bytes=VMEM_LIMIT)


def _sigmoid(v):
    return 1.0 / (1.0 + jnp.exp(-v))


def _ada_norm(x, g, mod_l, mod_c, first_row, which):
    y = x * lax.rsqrt(jnp.mean(x * x, axis=-1, keepdims=True) + EPS) * g
    sh_l, sc_l = _mod_slice(mod_l, 3 * which), _mod_slice(mod_l, 3 * which + 1)
    sh_c, sc_c = _mod_slice(mod_c, 3 * which), _mod_slice(mod_c, 3 * which + 1)
    is_ctx = _is_ctx(x.shape[0], first_row)
    return y * (1.0 + jnp.where(is_ctx, sc_c, sc_l)) + jnp.where(is_ctx, sh_c, sh_l)


def _mod_slice(mod, k):
    return mod[:, k * D_MODEL:(k + 1) * D_MODEL]


def _is_ctx(rows, first_row):
    return (lax.broadcasted_iota(jnp.int32, (rows, 1), 0) + first_row) < CTX_LEN


def _mod_kernel(src_ref, w_ref, b_ref, o_ref):
    s = src_ref[...]
    s = (s * _sigmoid(s)).astype(BF16)
    o_ref[0] = jnp.dot(s, w_ref[0].astype(BF16), preferred_element_type=F32) + b_ref[0]


def modulation(src, mod_w, mod_b):
    rows = src.shape[0]
    tn = 1536
    return pl.pallas_call(
        _mod_kernel,
        out_shape=jax.ShapeDtypeStruct((DEPTH, rows, 6 * D_MODEL), F32),
        grid=(DEPTH, 6 * D_MODEL // tn),
        in_specs=[pl.BlockSpec((rows, D_MODEL), lambda l, n: (0, 0)),
                  pl.BlockSpec((1, D_MODEL, tn), lambda l, n: (l, 0, n)),
                  pl.BlockSpec((1, 1, tn), lambda l, n: (l, 0, n))],
        out_specs=pl.BlockSpec((1, rows, tn), lambda l, n: (l, 0, n)),
        compiler_params=_cparams(("arbitrary", "arbitrary")),
        name="modulation",
    )(src, mod_w, mod_b.reshape(DEPTH, 1, 6 * D_MODEL))


IN_TM = 1152
IN_TN = 1024


def _in_kernel(x_ref, g_ref, ml_ref, mc_ref, w_ref, b_ref, ws_ref, bs_ref, o_ref, os_ref, h_ref):
    n = pl.program_id(2)

    @pl.when(n == 0)
    def _():
        h = _ada_norm(x_ref[0], g_ref[...], ml_ref[0], mc_ref[0], pl.program_id(1) * IN_TM, 0)
        h_ref[...] = h.astype(BF16)
        os_ref[0] = jnp.dot(h_ref[...], ws_ref[...], preferred_element_type=F32) + bs_ref[...]

    o_ref[0] = (jnp.dot(h_ref[...], w_ref[...], preferred_element_type=F32) + b_ref[...]).astype(o_ref.dtype)


def in_projection(xs, g, mod, w_big, b_big, w_small, b_small):
    bsz = xs.shape[0]
    mod3 = mod.reshape(bsz + 1, 1, 6 * D_MODEL)
    return pl.pallas_call(
        _in_kernel,
        out_shape=(jax.ShapeDtypeStruct((bsz, S_ALL, N_BIG), BF16),
                   jax.ShapeDtypeStruct((bsz, S_ALL, N_SMALL), F32)),
        grid=(bsz, S_ALL // IN_TM, N_BIG // IN_TN),
        in_specs=[pl.BlockSpec((1, IN_TM, D_MODEL), lambda b, s, n: (b, s, 0)),
                  pl.BlockSpec((1, D_MODEL), lambda b, s, n: (0, 0)),
                  pl.BlockSpec((1, 1, 6 * D_MODEL), lambda b, s, n: (b, 0, 0)),
                  pl.BlockSpec((1, 1, 6 * D_MODEL), lambda b, s, n: (bsz, 0, 0)),
                  pl.BlockSpec((D_MODEL, IN_TN), lambda b, s, n: (0, n)),
                  pl.BlockSpec((1, IN_TN), lambda b, s, n: (0, n)),
                  pl.BlockSpec((D_MODEL, N_SMALL), lambda b, s, n: (0, 0)),
                  pl.BlockSpec((1, N_SMALL), lambda b, s, n: (0, 0))],
        out_specs=(pl.BlockSpec((1, IN_TM, IN_TN), lambda b, s, n: (b, s, n)),
                   pl.BlockSpec((1, IN_TM, N_SMALL), lambda b, s, n: (b, s, 0))),
        scratch_shapes=[pltpu.VMEM((IN_TM, D_MODEL), BF16)],
        compiler_params=_cparams(("arbitrary", "arbitrary", "arbitrary")),
        name="in_projection",
    )(xs, g.reshape(1, D_MODEL), mod3, mod3, w_big, b_big, w_small, b_small)


MERGE_TM = 768


def _merge_kernel(ym_ref, yg_ref, yr_ref, g0_ref, g1_ref, g2_ref, x_ref, ml_ref, mc_ref, wb_ref, wo_ref, o_ref):
    acc = None
    for n, (y_ref, gp_ref) in enumerate(((ym_ref, g0_ref), (yg_ref, g1_ref), (yr_ref, g2_ref))):
        p = jnp.dot(y_ref[0].astype(BF16), wb_ref[n], preferred_element_type=F32)
        p = p * _sigmoid(gp_ref[0].astype(F32))
        acc = p if acc is None else acc + p
    out = jnp.dot(acc.astype(BF16), wo_ref[...], preferred_element_type=F32)
    is_ctx = _is_ctx(MERGE_TM, pl.program_id(1) * MERGE_TM)
    gt = jnp.where(is_ctx, _mod_slice(mc_ref[0], 2), _mod_slice(ml_ref[0], 2))
    o_ref[0] = x_ref[0] + gt * out


def merge_branches(ym, yg, yr, big, xs, mod, w_branch, w_out):
    bsz = xs.shape[0]
    mod3 = mod.reshape(bsz + 1, 1, 6 * D_MODEL)
    row = lambda b, s: (b, s, 0)
    gate_blk = C_MG // D_MODEL
    const = pl.Buffered(1)
    return pl.pallas_call(
        _merge_kernel,
        out_shape=jax.ShapeDtypeStruct(xs.shape, F32),
        grid=(bsz, S_ALL // MERGE_TM),
        in_specs=[pl.BlockSpec((1, MERGE_TM, D_MODEL), row),
                  pl.BlockSpec((1, MERGE_TM, D_MODEL), row),
                  pl.BlockSpec((1, MERGE_TM, D_MODEL), row),
                  pl.BlockSpec((1, MERGE_TM, D_MODEL), lambda b, s: (b, s, gate_blk)),
                  pl.BlockSpec((1, MERGE_TM, D_MODEL), lambda b, s: (b, s, gate_blk + 1)),
                  pl.BlockSpec((1, MERGE_TM, D_MODEL), lambda b, s: (b, s, gate_blk + 2)),
                  pl.BlockSpec((1, MERGE_TM, D_MODEL), row),
                  pl.BlockSpec((1, 1, 6 * D_MODEL), lambda b, s: (b, 0, 0)),
                  pl.BlockSpec((1, 1, 6 * D_MODEL), lambda b, s: (bsz, 0, 0)),
                  pl.BlockSpec((N_BRANCH, BRANCH_W, D_MODEL), lambda b, s: (0, 0, 0), pipeline_mode=const),
                  pl.BlockSpec((D_MODEL, D_MODEL), lambda b, s: (0, 0), pipeline_mode=const)],
        out_specs=pl.BlockSpec((1, MERGE_TM, D_MODEL), row),
        compiler_params=_cparams(("arbitrary", "arbitrary")),
        name="merge_branches",
    )(ym, yg, yr, big, big, big, xs, mod3, mod3, w_branch, w_out)


FFN_TM = 1152
FFN_TF = 256


def _swiglu_chunk(h, w1, w3, w2):
    a = jnp.dot(h, w1, preferred_element_type=F32)
    b = jnp.dot(h, w3, preferred_element_type=F32)
    return jnp.dot((a * _sigmoid(a) * b).astype(BF16), w2, preferred_element_type=F32)


def _ffn_kernel(x_ref, g_ref, ml_ref, mc_ref, w1_ref, w3_ref, w2_ref, o_ref, h_ref, acc_ref):
    f = pl.program_id(2)

    @pl.when(f == 0)
    def _():
        h = _ada_norm(x_ref[0], g_ref[...], ml_ref[0], mc_ref[0], pl.program_id(1) * FFN_TM, 1)
        h_ref[...] = h.astype(BF16)
        acc_ref[...] = jnp.zeros_like(acc_ref)

    acc_ref[...] += _swiglu_chunk(h_ref[...], w1_ref[...], w3_ref[...], w2_ref[...])

    @pl.when(f == pl.num_programs(2) - 1)
    def _():
        is_ctx = _is_ctx(FFN_TM, pl.program_id(1) * FFN_TM)
        gt = jnp.where(is_ctx, _mod_slice(mc_ref[0], 5), _mod_slice(ml_ref[0], 5))
        o_ref[0] = x_ref[0] + gt * acc_ref[...]


def dense_ffn(xs, g, mod, w1, w3, w2):
    bsz = xs.shape[0]
    mod3 = mod.reshape(bsz + 1, 1, 6 * D_MODEL)
    return pl.pallas_call(
        _ffn_kernel,
        out_shape=jax.ShapeDtypeStruct(xs.shape, F32),
        grid=(bsz, S_ALL // FFN_TM, D_FF // FFN_TF),
        in_specs=[pl.BlockSpec((1, FFN_TM, D_MODEL), lambda b, s, f: (b, s, 0)),
                  pl.BlockSpec((1, D_MODEL), lambda b, s, f: (0, 0)),
                  pl.BlockSpec((1, 1, 6 * D_MODEL), lambda b, s, f: (b, 0, 0)),
                  pl.BlockSpec((1, 1, 6 * D_MODEL), lambda b, s, f: (bsz, 0, 0)),
                  pl.BlockSpec((D_MODEL, FFN_TF), lambda b, s, f: (0, f)),
                  pl.BlockSpec((D_MODEL, FFN_TF), lambda b, s, f: (0, f)),
                  pl.BlockSpec((FFN_TF, D_MODEL), lambda b, s, f: (f, 0))],
        out_specs=pl.BlockSpec((1, FFN_TM, D_MODEL), lambda b, s, f: (b, s, 0)),
        scratch_shapes=[pltpu.VMEM((FFN_TM, D_MODEL), BF16), pltpu.VMEM((FFN_TM, D_MODEL), F32)],
        compiler_params=_cparams(("arbitrary", "arbitrary", "arbitrary")),
        name="dense_ffn",
    )(xs, g.reshape(1, D_MODEL), mod3, mod3, w1, w3, w2)


NORM_TM = 768


def _norm_router_kernel(x_ref, g_ref, ml_ref, mc_ref, rw_ref, rb_ref, h_ref, lg_ref):
    h = _ada_norm(x_ref[0], g_ref[...], ml_ref[0], mc_ref[0], pl.program_id(1) * NORM_TM, 1)
    h_ref[0] = h.astype(h_ref.dtype)
    lg_ref[0] = jnp.dot(h, rw_ref[...], preferred_element_type=F32, precision=lax.Precision.HIGHEST) + rb_ref[...]


def norm_router(xs, g, mod, router_w, router_b):
    bsz = xs.shape[0]
    mod3 = mod.reshape(bsz + 1, 1, 6 * D_MODEL)
    rw = jnp.zeros((D_MODEL, LANE), F32).at[:, :N_EXPERTS].set(router_w)
    rb = jnp.zeros((1, LANE), F32).at[0, :N_EXPERTS].set(router_b)
    return pl.pallas_call(
        _norm_router_kernel,
        out_shape=(jax.ShapeDtypeStruct(xs.shape, BF16), jax.ShapeDtypeStruct((bsz, S_ALL, LANE), F32)),
        grid=(bsz, S_ALL // NORM_TM),
        in_specs=[pl.BlockSpec((1, NORM_TM, D_MODEL), lambda b, s: (b, s, 0)),
                  pl.BlockSpec((1, D_MODEL), lambda b, s: (0, 0)),
                  pl.BlockSpec((1, 1, 6 * D_MODEL), lambda b, s: (b, 0, 0)),
                  pl.BlockSpec((1, 1, 6 * D_MODEL), lambda b, s: (bsz, 0, 0)),
                  pl.BlockSpec((D_MODEL, LANE), lambda b, s: (0, 0)),
                  pl.BlockSpec((1, LANE), lambda b, s: (0, 0))],
        out_specs=(pl.BlockSpec((1, NORM_TM, D_MODEL), lambda b, s: (b, s, 0)),
                   pl.BlockSpec((1, NORM_TM, LANE), lambda b, s: (b, s, 0))),
        compiler_params=_cparams(("arbitrary", "arbitrary")),
        name="norm_router",
    )(xs, g.reshape(1, D_MODEL), mod3, mod3, rw, rb)


def _expert_kernel(be_ref, x_ref, w1_ref, w3_ref, w2_ref, o_ref):
    del be_ref
    x = x_ref[...]
    for f in range(D_FF // FFN_TF):
        cols = slice(f * FFN_TF, (f + 1) * FFN_TF)
        y = _swiglu_chunk(x, w1_ref[0, :, cols], w3_ref[0, :, cols], w2_ref[0, cols, :])
        if f == 0:
            o_ref[...] = y
        else:
            o_ref[...] += y


def grouped_experts(xb, block_e, w1, w3, w2):
    n_rows = xb.shape[0]
    const = pl.Buffered(1)
    grid_spec = pltpu.PrefetchScalarGridSpec(
        num_scalar_prefetch=1,
        grid=(n_rows // MOE_BLOCK,),
        in_specs=[pl.BlockSpec((MOE_BLOCK, D_MODEL), lambda i, be: (i, 0)),
                  pl.BlockSpec((1, D_MODEL, D_FF), lambda i, be: (be[i], 0, 0), pipeline_mode=const),
                  pl.BlockSpec((1, D_MODEL, D_FF), lambda i, be: (be[i], 0, 0), pipeline_mode=const),
                  pl.BlockSpec((1, D_FF, D_MODEL), lambda i, be: (be[i], 0, 0), pipeline_mode=const)],
        out_specs=pl.BlockSpec((MOE_BLOCK, D_MODEL), lambda i, be: (i, 0)),
    )
    return pl.pallas_call(
        _expert_kernel,
        out_shape=jax.ShapeDtypeStruct((n_rows, D_MODEL), F32),
        grid_spec=grid_spec,
        compiler_params=_cparams(("arbitrary",)),
        name="grouped_experts",
    )(block_e, xb, w1, w3, w2)


def moe_ffn(xs, g, mod, gt2, router_w, router_b, w1, w3, w2):
    bsz = xs.shape[0]
    h, logits = norm_router(xs, g, mod, router_w, router_b)
    h = h.reshape(-1, D_MODEL)
    n_tok = h.shape[0]
    top_logit, top_e = lax.top_k(logits.reshape(n_tok, LANE)[:, :N_EXPERTS], TOP_K)
    gate = jax.nn.softmax(top_logit, axis=-1)
    n_assign = n_tok * TOP_K
    flat_e = top_e.reshape(-1)
    flat_tok = jnp.repeat(jnp.arange(n_tok, dtype=jnp.int32), TOP_K)
    order = jnp.argsort(flat_e)
    sorted_e = flat_e[order]
    counts = jnp.bincount(flat_e, length=N_EXPERTS)
    padded = (counts + MOE_BLOCK - 1) // MOE_BLOCK * MOE_BLOCK
    start = jnp.cumsum(counts) - counts
    p_end = jnp.cumsum(padded)
    p_start = p_end - padded
    dest = p_start[sorted_e] + jnp.arange(n_assign, dtype=jnp.int32) - start[sorted_e]
    n_blocks = -(-n_assign // MOE_BLOCK) + N_EXPERTS
    n_rows = n_blocks * MOE_BLOCK
    row_tok = jnp.zeros((n_rows,), jnp.int32).at[dest].set(flat_tok[order])
    slot = jnp.zeros((n_assign,), jnp.int32).at[order].set(dest).reshape(n_tok, TOP_K)
    block_e = jnp.minimum(jnp.searchsorted(p_end, jnp.arange(n_blocks) * MOE_BLOCK, side='right'),
                          N_EXPERTS - 1).astype(jnp.int32)
    yb = grouped_experts(h[row_tok], block_e, w1, w3, w2)
    y = sum(yb[slot[:, k]] * gate[:, k:k + 1] for k in range(TOP_K))
    return xs + gt2 * y.reshape(bsz, S_ALL, D_MODEL)


FINAL_TM = 256


def _final_kernel(x_ref, g_ref, o_ref):
    x = x_ref[0]
    o_ref[0] = x * lax.rsqrt(jnp.mean(x * x, axis=-1, keepdims=True) + EPS) * g_ref[...]


def final_norm(xs, g):
    bsz = xs.shape[0]
    skip = CTX_LEN // FINAL_TM
    return pl.pallas_call(
        _final_kernel,
        out_shape=jax.ShapeDtypeStruct((bsz, SEQ, D_MODEL), F32),
        grid=(bsz, SEQ // FINAL_TM),
        in_specs=[pl.BlockSpec((1, FINAL_TM, D_MODEL), lambda b, s: (b, s + skip, 0)),
                  pl.BlockSpec((1, D_MODEL), lambda b, s: (0, 0))],
        out_specs=pl.BlockSpec((1, FINAL_TM, D_MODEL), lambda b, s: (b, s, 0)),
        compiler_params=_cparams(("arbitrary", "arbitrary")),
        name="final_norm",
    )(xs, g.reshape(1, D_MODEL))


N_CHUNKS = S_ALL // CHUNK
CTX_CHUNKS = CTX_LEN // CHUNK
GATE_ROWS = 8
NEG_INF = float("-inf")


def _softplus(v):
    return jnp.maximum(v, 0.0) + jnp.log(1.0 + jnp.exp(-jnp.abs(v)))


def _scan_constants():
    s = np.arange(CHUNK)[:, None]
    t = np.arange(CHUNK)[None, :]
    tri = np.concatenate([(s <= t), (s >= t), np.ones((CHUNK, CHUNK), bool)], axis=1).astype(np.float32)
    return jnp.asarray(tri), jnp.asarray(np.eye(CHUNK, dtype=np.float32))


def _rev_chunk(j):
    return jnp.where(j < CTX_CHUNKS, CTX_CHUNKS - 1 - j, N_CHUNKS + CTX_CHUNKS - 1 - j)


def _head_gate_rows(small, c0, heads, kind_major):
    bsz = small.shape[0]
    a = small[:, :, c0:c0 + 4 * heads].reshape(bsz, N_CHUNKS, CHUNK, 2, 2, heads)
    a = a.transpose((0, 5, 1, 4, 3, 2) if kind_major else (0, 5, 1, 3, 4, 2))
    a = a.reshape(bsz, heads, N_CHUNKS, 4, CHUNK)
    return jnp.pad(a, ((0, 0), (0, 0), (0, 0), (0, GATE_ROWS - 4), (0, 0)))


def _tn_dot(a, b):
    return lax.dot_general(a, b, (((0,), (0,)), ((), ())), preferred_element_type=F32)


def _nt_dot(a, b, precision=None):
    return lax.dot_general(a, b, (((1,), (1,)), ((), ())), preferred_element_type=F32, precision=precision)


def _dir_mask(d):
    t = lax.broadcasted_iota(jnp.int32, (CHUNK, CHUNK), 0)
    s = lax.broadcasted_iota(jnp.int32, (CHUNK, CHUNK), 1)
    return (s <= t, s < t) if d == 0 else (s >= t, s > t)


def _split_dot(a, ones, dot_fn):
    b = ones.astype(BF16)
    hi = a.astype(BF16)
    lo = (a - hi.astype(F32)).astype(BF16)
    return dot_fn(hi, b, preferred_element_type=F32) + dot_fn(lo, b, preferred_element_type=F32)


def _gate_forms(vals, tri, eye):
    cum = _split_dot(vals, tri, jnp.dot)
    kind = lax.broadcasted_iota(jnp.int32, vals.shape, 0) & (GATE_ROWS - 1)
    rows = jnp.where(kind == 1, cum[:, :CHUNK], jnp.where(kind == 3, cum[:, CHUNK:2 * CHUNK], vals))
    cols = _split_dot(rows, eye, lambda a, b, **kw: _nt_dot(b, a))
    return rows, cols, cum[:, 2 * CHUNK:2 * CHUNK + 1]


def _x_row(rows, d, g=0):
    return rows[GATE_ROWS * g + 2 * d:GATE_ROWS * g + 2 * d + 1]


def _run_row(rows, d, g=0):
    return rows[GATE_ROWS * g + 2 * d + 1:GATE_ROWS * g + 2 * d + 2]


def _x_col(cols, d, g=0):
    return cols[:, GATE_ROWS * g + 2 * d:GATE_ROWS * g + 2 * d + 1]


def _run_col(cols, d, g=0):
    return cols[:, GATE_ROWS * g + 2 * d + 1:GATE_ROWS * g + 2 * d + 2]


PAIRS_PER_STEP = 3
OUT_PAIRS_PER_STEP = 1


def _mlstm_kernel(q_ref, k_ref, v_ref, o_ref, g_ref, fb_ref, ng_ref, tri_ref, eye_ref, y_ref,
                  c_s, st_s, gr_s, gc_s):
    tri, eye = tri_ref[...], eye_ref[...]
    fbias = fb_ref[0]
    scale = M_DQK ** -0.5
    kind = lax.broadcasted_iota(jnp.int32, (2 * GATE_ROWS, CHUNK), 0) & 1
    fbias2 = jnp.concatenate([fbias, fbias], axis=0)
    st_row = lax.broadcasted_iota(jnp.int32, (GATE_ROWS, M_DQK), 0)

    def pair_rows(c0):
        return pl.ds(pl.multiple_of(c0 * CHUNK, 2 * CHUNK), 2 * CHUNK)

    def state_rows(c):
        return pl.ds(pl.multiple_of(c * M_DQK, M_DQK), M_DQK)

    def pair_state_rows(c0):
        return pl.ds(pl.multiple_of(c0 * M_DQK, 2 * M_DQK), 2 * M_DQK)

    def local(it, carry):
        pairs = [it * PAIRS_PER_STEP + p for p in range(PAIRS_PER_STEP)]
        forms = []
        for g in pairs:
            raw = jnp.concatenate([g_ref[0, 0, 2 * g], g_ref[0, 0, 2 * g + 1]], axis=0)
            vals = jnp.where(kind == 1, -_softplus(-(raw + fbias2)), raw)
            forms.append(_gate_forms(vals, tri, eye))
        zero = jnp.zeros((CHUNK, 2 * M_DQK), F32)
        lhs_all = []
        for g, (rows, cols, totals) in zip(pairs, forms):
            gr_s[g] = rows
            gc_s[g] = cols
            kf2 = k_ref[0, pair_rows(2 * g), :].astype(F32)
            lhs = []
            for i in range(2):
                kf = kf2[i * CHUNK:(i + 1) * CHUNK]
                kws = []
                for d in range(2):
                    tot = totals[GATE_ROWS * i + 2 * d + 1:GATE_ROWS * i + 2 * d + 2]
                    m_loc = jnp.max(tot - _run_row(rows, d, i) + _x_row(rows, d, i), axis=-1, keepdims=True)
                    kw = kf * (jnp.exp(tot - _run_col(cols, d, i) + _x_col(cols, d, i) - m_loc) * scale)
                    kws.append(kw)
                    st_s[d, 2 * g + i] = jnp.where(st_row == 0, jnp.sum(kw, axis=0, keepdims=True),
                                                   jnp.where(st_row == 1, m_loc, tot))
                lhs.append(jnp.concatenate(kws + [zero] if i == 0 else [zero] + kws, axis=1))
            lhs_all.append(jnp.concatenate(lhs, axis=0).astype(BF16))
        for g, lhs in zip(pairs, lhs_all):
            c_loc = _tn_dot(lhs, v_ref[0, pair_rows(2 * g), :])
            for i in range(2):
                for d in range(2):
                    c_s[d, state_rows(2 * g + i), :] = c_loc[(2 * i + d) * M_DQK:(2 * i + d + 1) * M_DQK]
        return carry

    lax.fori_loop(0, N_CHUNKS // (2 * PAIRS_PER_STEP), local, 0)

    def scan(d):
        def body(j, carry):
            c_prev, n_prev, m_prev = carry
            c = j if d == 0 else _rev_chunk(j)
            st = st_s[d, c]
            n_loc, m_loc, tot = st[0:1], st[1:2, :1], st[2:3, :1]
            c_loc = c_s[d, state_rows(c), :]
            c_s[d, state_rows(c), :] = c_prev
            st_s[d, c] = jnp.where(st_row == 0, n_prev, m_prev)
            m_new = jnp.maximum(tot + m_prev, m_loc)
            s_prev = jnp.exp(tot + m_prev - m_new)
            s_loc = jnp.exp(m_loc - m_new)
            return s_prev * c_prev + s_loc * c_loc, s_prev * n_prev + s_loc * n_loc, m_new

        lax.fori_loop(0, N_CHUNKS, body, (jnp.zeros((M_DQK, M_DV), F32), jnp.zeros((1, M_DQK), F32),
                                          jnp.zeros((1, 1), F32)))

    scan(0)
    scan(1)

    def output(it, carry):
        pairs = [it * OUT_PAIRS_PER_STEP + p for p in range(OUT_PAIRS_PER_STEP)]
        qk_all = [_nt_dot(q_ref[0, pair_rows(2 * g), :], k_ref[0, pair_rows(2 * g), :]) * scale for g in pairs]
        z_intra = jnp.zeros((CHUNK, 2 * CHUNK), F32)
        z_inter = jnp.zeros((CHUNK, M_DQK), F32)
        operands = []
        for g, qk2 in zip(pairs, qk_all):
            c0 = 2 * g
            qf2, v2 = q_ref[0, pair_rows(c0), :].astype(F32), v_ref[0, pair_rows(c0), :]
            rows, cols = gr_s[g], gc_s[g]
            lhs_intra, lhs_inter, rhs_v = [], [], []
            for i in range(2):
                qk = qk2[i * CHUNK:(i + 1) * CHUNK, i * CHUNK:(i + 1) * CHUNK]
                qf = qf2[i * CHUNK:(i + 1) * CHUNK]
                intra, inter = [], []
                for d in range(2):
                    st = st_s[d, c0 + i]
                    b_c = _run_col(cols, d, i)
                    n0, m0 = st[0:1], st[1:2, :1]
                    incl, _ = _dir_mask(d)
                    d_log = jnp.where(incl, b_c - _run_row(rows, d, i) + _x_row(rows, d, i), NEG_INF)
                    m_inter = b_c + m0
                    m_comb = jnp.maximum(m_inter, jnp.max(d_log, axis=-1, keepdims=True))
                    s = qk * jnp.exp(d_log - m_comb)
                    e_inter = jnp.exp(m_inter - m_comb)
                    den = (jnp.sum(s, axis=-1, keepdims=True)
                           + e_inter * jnp.sum(qf * n0, axis=-1, keepdims=True))
                    inv = 1.0 / jnp.maximum(jnp.abs(den), jnp.exp(-m_comb))
                    intra.append(s * inv)
                    inter.append(qf * (e_inter * inv))
                pair = jnp.concatenate(intra, axis=1)
                lhs_intra.append(jnp.concatenate([pair, z_intra] if i == 0 else [z_intra, pair], axis=1))
                lhs_inter.append(jnp.concatenate([inter[0], z_inter, inter[1], z_inter] if i == 0
                                                 else [z_inter, inter[0], z_inter, inter[1]], axis=1))
                v = v2[i * CHUNK:(i + 1) * CHUNK]
                rhs_v += [v, v]
            c_in = jnp.concatenate([c_s[0, pair_state_rows(c0), :], c_s[1, pair_state_rows(c0), :]], axis=0)
            operands.append((jnp.concatenate(lhs_intra, axis=0).astype(BF16), jnp.concatenate(rhs_v, axis=0),
                             jnp.concatenate(lhs_inter, axis=0).astype(BF16), c_in.astype(BF16)))
        hs = [jnp.dot(li, rv, preferred_element_type=F32) + jnp.dot(le, ci, preferred_element_type=F32)
              for li, rv, le, ci in operands]
        for g, h in zip(pairs, hs):
            h = h * lax.rsqrt(jnp.mean(h * h, axis=-1, keepdims=True) + EPS) * ng_ref[...]
            y_ref[0, pair_rows(2 * g), :] = (h * _sigmoid(o_ref[0, pair_rows(2 * g), :].astype(F32))
                                             ).astype(y_ref.dtype)
        return carry

    lax.fori_loop(0, N_CHUNKS // (2 * OUT_PAIRS_PER_STEP), output, 0)


def mlstm_mixer(big, small, f_bias, norm_g):
    bsz = big.shape[0]
    gates = _head_gate_rows(small, 0, M_HEADS, False)
    fb = jnp.zeros((M_HEADS, GATE_ROWS, CHUNK), F32)
    fb = fb.at[:, 1].set(f_bias[0][:, None]).at[:, 3].set(f_bias[1][:, None])
    tri, eye = _scan_constants()
    qb, kb, vb, ob = C_MQ // M_DQK, C_MK // M_DQK, C_MV // M_DV, C_MO // M_DV
    return pl.pallas_call(
        _mlstm_kernel,
        out_shape=jax.ShapeDtypeStruct((bsz, S_ALL, BRANCH_W), BF16),
        grid=(bsz, M_HEADS),
        in_specs=[pl.BlockSpec((1, S_ALL, M_DQK), lambda b, h: (b, 0, qb + h)),
                  pl.BlockSpec((1, S_ALL, M_DQK), lambda b, h: (b, 0, kb + h)),
                  pl.BlockSpec((1, S_ALL, M_DV), lambda b, h: (b, 0, vb + h)),
                  pl.BlockSpec((1, S_ALL, M_DV), lambda b, h: (b, 0, ob + h)),
                  pl.BlockSpec((1, 1, N_CHUNKS, GATE_ROWS, CHUNK), lambda b, h: (b, h, 0, 0, 0)),
                  pl.BlockSpec((1, GATE_ROWS, CHUNK), lambda b, h: (h, 0, 0)),
                  pl.BlockSpec((1, M_DV), lambda b, h: (0, h)),
                  pl.BlockSpec((CHUNK, 3 * CHUNK), lambda b, h: (0, 0)),
                  pl.BlockSpec((CHUNK, CHUNK), lambda b, h: (0, 0))],
        out_specs=pl.BlockSpec((1, S_ALL, M_DV), lambda b, h: (b, 0, h)),
        scratch_shapes=[pltpu.VMEM((2, N_CHUNKS * M_DQK, M_DV), F32),
                        pltpu.VMEM((2, N_CHUNKS, GATE_ROWS, M_DQK), F32),
                        pltpu.VMEM((N_CHUNKS // 2, 2 * GATE_ROWS, CHUNK), F32),
                        pltpu.VMEM((N_CHUNKS // 2, CHUNK, 2 * GATE_ROWS), F32)],
        compiler_params=_cparams(("arbitrary", "arbitrary")),
        name="mlstm_mixer",
    )(big, big, big, big, gates, fb, norm_g.reshape(1, BRANCH_W), tri, eye)


CONV_BLK = 256
CONV_HALO = 16
SOLVE_BLK = 2
SOLVE_GROUP = 4
PREP_GROUPS = 3


def _conv_silu(src_ref, w, dst_ref, l2_scale):
    n_blk = S_ALL // CONV_BLK
    zeros = jnp.zeros((CONV_HALO, src_ref.shape[-1]), F32)
    for i in range(n_blk):
        lo = i * CONV_BLK
        first = i == 0 or lo == CTX_LEN
        last = lo + CONV_BLK in (CTX_LEN, S_ALL)
        body = src_ref[0, lo - (0 if first else CONV_HALO):lo + CONV_BLK + (0 if last else CONV_HALO), :]
        win = jnp.concatenate(([zeros] if first else []) + [body.astype(F32)] + ([zeros] if last else []), axis=0)
        n = win.shape[0]
        y = None
        for j in range(CONV_W):
            tap = win if j == CONV_W // 2 else pltpu.roll(win, (CONV_W // 2 - j) % n, 0)
            term = tap[CONV_HALO:CONV_HALO + CONV_BLK] * w[j:j + 1, :]
            y = term if y is None else y + term
        y = y * _sigmoid(y)
        if l2_scale is not None:
            y = y * (lax.rsqrt(jnp.sum(y * y, axis=-1, keepdims=True) + EPS) * l2_scale)
        dst_ref[lo:lo + CONV_BLK, :] = y.astype(dst_ref.dtype)


def _block_masks():
    n = SOLVE_GROUP * CHUNK
    t = np.arange(n)[:, None]
    s = np.arange(n)[None, :]
    masks = [(t // SOLVE_BLK) == (s // SOLVE_BLK)]
    k = SOLVE_BLK
    while k < CHUNK:
        masks.append(((t // (2 * k)) == (s // (2 * k))) & ((t // k) != (s // k)))
        k *= 2
    return jnp.asarray(np.stack(masks).astype(np.float32))


def _unit_triangular_inverse(systems, eye, masks):
    ts = [eye - a * masks[0] for a in systems]
    for lvl in range(1, masks.shape[0]):
        tbs = [t.astype(BF16) for t in ts]
        tls = [jnp.dot(tb, (a * masks[lvl]).astype(BF16), preferred_element_type=F32)
               for tb, a in zip(tbs, systems)]
        ts = [t - jnp.dot(tl.astype(BF16), tb, preferred_element_type=F32) for t, tl, tb in zip(ts, tls, tbs)]
    return ts


def _block_diag(blocks):
    n = len(blocks)
    z_half = jnp.zeros((CHUNK, CHUNK), F32)
    z_tile = jnp.zeros((CHUNK, 2 * CHUNK), F32)
    out = []
    for p, blk in enumerate(blocks):
        pair = jnp.concatenate([blk, z_half] if p % 2 == 0 else [z_half, blk], axis=1)
        out.append(jnp.concatenate([pair if t == p // 2 else z_tile for t in range(n // 2)], axis=1))
    return jnp.concatenate(out, axis=0)


def _gdn_kernel(xq_ref, xk_ref, xv_ref, z_ref, g_ref, gp_ref, cw_ref, ng_ref, tri_ref, eye_ref, bm_ref, y_ref,
                q_s, k_s, v_s, u_s, wq_s, at_s, kt_s, cd_s, acc_ref, s_ref):
    tri, eye, masks = tri_ref[...], eye_ref[...], bm_ref[...]
    cw = cw_ref[0]
    _conv_silu(xq_ref, cw[:, :G_DK], q_s, G_DK ** -0.5)
    _conv_silu(xk_ref, cw[:, G_DK:2 * G_DK], k_s, 1.0)
    _conv_silu(xv_ref, cw[:, 2 * G_DK:], v_s, None)
    s_ref[...] = jnp.zeros_like(s_ref)
    n_pair = SOLVE_GROUP // 2
    kind = lax.broadcasted_iota(jnp.int32, (n_pair * GATE_ROWS, CHUNK), 0) & 1
    neg_a = jnp.concatenate([gp_ref[0, 0]] * n_pair, axis=0)
    dt_bias = jnp.concatenate([gp_ref[0, 1]] * n_pair, axis=0)
    n_bd = SOLVE_GROUP * CHUNK
    eye_bd = (lax.broadcasted_iota(jnp.int32, (n_bd, n_bd), 0)
              == lax.broadcasted_iota(jnp.int32, (n_bd, n_bd), 1)).astype(F32)
    z_half = jnp.zeros((CHUNK, CHUNK), F32)

    def chunk_rows(c):
        return pl.ds(pl.multiple_of(c * CHUNK, CHUNK), CHUNK)

    def wq_rows(c):
        return pl.ds(pl.multiple_of(c * 2 * CHUNK, 2 * CHUNK), 2 * CHUNK)

    def prep(g, carry):
        systems, rhs_all, q_dec_all = [], [], []
        for sgrp in range(PREP_GROUPS):
            c0 = (g * PREP_GROUPS + sgrp) * n_pair
            raw = jnp.concatenate([g_ref[0, 0, c0 + i] for i in range(n_pair)], axis=0)
            vals = jnp.where(kind == 1, neg_a * _softplus(raw + dt_bias), _sigmoid(raw))
            g_rows, g_cols, totals = _gate_forms(vals, tri, eye)
            blocks, rhs, q_dec = [], [], []
            for i in range(n_pair):
                c = c0 + i
                q, k, v = q_s[chunk_rows(c), :], k_s[chunk_rows(c), :], v_s[chunk_rows(c), :]
                kf, vf, qf = k.astype(F32), v.astype(F32), q.astype(F32)
                kq = _nt_dot(jnp.concatenate([k, q], axis=0), k)
                kk, qk = kq[:CHUNK], kq[CHUNK:]
                for d in range(2):
                    gam_r, beta_c, gam_c = _run_row(g_rows, d, i), _x_col(g_cols, d, i), _run_col(g_cols, d, i)
                    tot = totals[GATE_ROWS * i + 2 * d + 1:GATE_ROWS * i + 2 * d + 2]
                    incl, strict = _dir_mask(d)
                    decay = jnp.exp(jnp.where(incl, gam_c - gam_r, NEG_INF))
                    blocks.append(jnp.where(strict, kk * beta_c * decay, 0.0))
                    e_gam = jnp.exp(gam_c)
                    rhs.append(jnp.concatenate([vf * beta_c, kf * (beta_c * e_gam)], axis=-1))
                    q_dec.append(qf * e_gam)
                    attn = qk * decay
                    at_s[d, chunk_rows(c), :] = jnp.concatenate([attn, z_half] if d == 0 else [z_half, attn],
                                                                axis=1).astype(BF16)
                    kt_s[d, chunk_rows(c), :] = (kf * jnp.exp(tot - gam_c)).astype(BF16)
                    cd_s[d, c] = jnp.broadcast_to(jnp.exp(tot), cd_s.shape[2:])
            systems.append(_block_diag(blocks))
            rhs_all.append(jnp.concatenate(rhs, axis=0).astype(BF16))
            q_dec_all.append(q_dec)
        t_invs = _unit_triangular_inverse(systems, eye_bd, masks)
        for sgrp in range(PREP_GROUPS):
            c0 = (g * PREP_GROUPS + sgrp) * n_pair
            x = jnp.dot(t_invs[sgrp].astype(BF16), rhs_all[sgrp], preferred_element_type=F32)
            for i in range(n_pair):
                for d in range(2):
                    p = 2 * i + d
                    xp = x[p * CHUNK:(p + 1) * CHUNK]
                    u_s[d, chunk_rows(c0 + i), :] = xp[:, :G_DV]
                    wq_s[d, wq_rows(c0 + i), :] = jnp.concatenate([xp[:, G_DV:], q_dec_all[sgrp][p]],
                                                                  axis=0).astype(BF16)
        return carry

    lax.fori_loop(0, N_CHUNKS // (n_pair * PREP_GROUPS), prep, 0)

    z_wq = jnp.zeros((2 * CHUNK, G_DK), BF16)

    def step(j, carry):
        cs = (j, _rev_chunk(j))
        s_prev = (s_ref[0], s_ref[1])
        wq_bd = jnp.concatenate([jnp.concatenate([wq_s[0, wq_rows(cs[0]), :], z_wq], axis=1),
                                 jnp.concatenate([z_wq, wq_s[1, wq_rows(cs[1]), :]], axis=1)], axis=0)
        ws_qs = jnp.dot(wq_bd, jnp.concatenate(s_prev, axis=0).astype(BF16), preferred_element_type=F32)
        v_new = [(u_s[d, chunk_rows(cs[d]), :] - ws_qs[2 * d * CHUNK:(2 * d + 1) * CHUNK]).astype(BF16)
                 for d in range(2)]
        at_bd = jnp.concatenate([at_s[d, chunk_rows(cs[d]), :] for d in range(2)], axis=0)
        o = jnp.dot(at_bd, jnp.concatenate(v_new, axis=0), preferred_element_type=F32)
        kt = jnp.concatenate([kt_s[d, chunk_rows(cs[d]), :] for d in range(2)], axis=1)
        ktv = _tn_dot(kt, jnp.concatenate(v_new, axis=1))
        for d in range(2):
            acc_ref[d, chunk_rows(cs[d]), :] = (ws_qs[(2 * d + 1) * CHUNK:(2 * d + 2) * CHUNK]
                                                + o[d * CHUNK:(d + 1) * CHUNK])
            s_ref[d] = (cd_s[d, cs[d]][:1, :] * s_prev[d]
                        + ktv[d * G_DK:(d + 1) * G_DK, d * G_DV:(d + 1) * G_DV])
        return carry

    lax.fori_loop(0, N_CHUNKS, step, 0)

    def finish(c, carry):
        rows = pl.ds(pl.multiple_of(c * CHUNK, CHUNK), CHUNK)
        h = acc_ref[0, rows, :] + acc_ref[1, rows, :]
        h = h * lax.rsqrt(jnp.mean(h * h, axis=-1, keepdims=True) + EPS) * ng_ref[...]
        z = z_ref[0, rows, :].astype(F32)
        y_ref[0, rows, :] = (h * (z * _sigmoid(z))).astype(y_ref.dtype)
        return carry

    lax.fori_loop(0, N_CHUNKS, finish, 0)


def gdn_mixer(big, small, conv_w, a_log, dt_bias, norm_g):
    bsz = big.shape[0]
    gates = _head_gate_rows(small, 4 * M_HEADS, G_HEADS, True)
    gp = jnp.zeros((G_HEADS, 2, GATE_ROWS, CHUNK), F32)
    for d in range(2):
        gp = gp.at[:, 0, 2 * d + 1].set(-jnp.exp(a_log[d])[:, None]).at[:, 1, 2 * d + 1].set(dt_bias[d][:, None])
    cw = conv_w.reshape(CONV_W, 3, G_HEADS, G_DK).transpose(2, 0, 1, 3).reshape(G_HEADS, CONV_W, 3 * G_DK)
    cw = jnp.pad(cw, ((0, 0), (0, 8 - CONV_W), (0, 0)))
    tri, eye = _scan_constants()
    masks = _block_masks()
    qb, kb, vb, zb = C_GQ // G_DK, C_GK // G_DK, C_GV // G_DV, C_GZ // G_DV
    return pl.pallas_call(
        _gdn_kernel,
        out_shape=jax.ShapeDtypeStruct((bsz, S_ALL, BRANCH_W), BF16),
        grid=(bsz, G_HEADS),
        in_specs=[pl.BlockSpec((1, S_ALL, G_DK), lambda b, h: (b, 0, qb + h)),
                  pl.BlockSpec((1, S_ALL, G_DK), lambda b, h: (b, 0, kb + h)),
                  pl.BlockSpec((1, S_ALL, G_DV), lambda b, h: (b, 0, vb + h)),
                  pl.BlockSpec((1, S_ALL, G_DV), lambda b, h: (b, 0, zb + h)),
                  pl.BlockSpec((1, 1, N_CHUNKS, GATE_ROWS, CHUNK), lambda b, h: (b, h, 0, 0, 0)),
                  pl.BlockSpec((1, 2, GATE_ROWS, CHUNK), lambda b, h: (h, 0, 0, 0)),
                  pl.BlockSpec((1, 8, 3 * G_DK), lambda b, h: (h, 0, 0)),
                  pl.BlockSpec((1, G_DV), lambda b, h: (0, 0)),
                  pl.BlockSpec((CHUNK, 3 * CHUNK), lambda b, h: (0, 0)),
                  pl.BlockSpec((CHUNK, CHUNK), lambda b, h: (0, 0)),
                  pl.BlockSpec(masks.shape, lambda b, h: (0, 0, 0))],
        out_specs=pl.BlockSpec((1, S_ALL, G_DV), lambda b, h: (b, 0, h)),
        scratch_shapes=[pltpu.VMEM((S_ALL, G_DK), BF16), pltpu.VMEM((S_ALL, G_DK), BF16),
                        pltpu.VMEM((S_ALL, G_DV), BF16),
                        pltpu.VMEM((2, S_ALL, G_DV), F32),
                        pltpu.VMEM((2, 2 * S_ALL, G_DK), BF16),
                        pltpu.VMEM((2, S_ALL, 2 * CHUNK), BF16),
                        pltpu.VMEM((2, S_ALL, G_DK), BF16),
                        pltpu.VMEM((2, N_CHUNKS, 8, G_DV), F32),
                        pltpu.VMEM((2, S_ALL, G_DV), F32),
                        pltpu.VMEM((2, G_DK, G_DV), F32)],
        compiler_params=_cparams(("arbitrary", "arbitrary")),
        name="gdn_mixer",
    )(big, big, big, big, gates, gp, cw, norm_g.reshape(1, G_DV), tri, eye, masks)


GRID_H = SEQ // GRID_W
LRU_BLK = 256


def _shift_rows(a, s):
    n = a.shape[0]
    r = lax.broadcasted_iota(jnp.int32, a.shape, 0)
    return jnp.where((r >= s) & (r < n + s), pltpu.roll(a, s % n, 0), 0.0)


def _gelu(v):
    return 0.5 * v * (1.0 + jnp.tanh(0.7978845608028654 * (v + 0.044715 * v * v * v)))


def _lru_kernel(x_ref, y_ref, cw_ref, cb_ref, gw_ref, gb_ref, lam_ref, o_ref,
                xb_s, a_s, u_s, h_s, p_s, end_s, cin_s):
    cw = cw_ref[...]
    xf = x_ref[0].astype(F32)
    xc = xf[:CTX_LEN]
    lat = lambda r0, r1: xf[CTX_LEN + r0 * GRID_W:CTX_LEN + r1 * GRID_W]
    prev1 = jnp.concatenate([_shift_rows(lat(GRID_H - 1, GRID_H), 1), lat(0, GRID_H - 1)], axis=0)
    prev2 = jnp.concatenate([_shift_rows(lat(GRID_H - 2, GRID_H - 1), 1), _shift_rows(lat(GRID_H - 1, GRID_H), 1),
                             lat(0, GRID_H - 2)], axis=0)
    next1 = jnp.concatenate([lat(1, GRID_H), _shift_rows(lat(0, 1), -1)], axis=0)
    taps_c = (_shift_rows(xc, 2), _shift_rows(xc, 1), xc, _shift_rows(xc, -1))
    taps_l = (prev2, prev1, lat(0, GRID_H), next1)
    xb_s[:CTX_LEN, :] = sum(t * cw[j:j + 1, :] for j, t in enumerate(taps_c)) + cb_ref[...]
    xb_s[CTX_LEN:, :] = sum(t * cw[j:j + 1, :] for j, t in enumerate(taps_l)) + cb_ref[...]

    neg_c_sp = -LRU_C * _softplus(-lam_ref[...])

    def gates(i, carry):
        rows = pl.ds(pl.multiple_of(i * LRU_BLK, LRU_BLK), LRU_BLK)
        xb = xb_s[rows, :]
        pre = jnp.dot(xb.astype(BF16), gw_ref[0], preferred_element_type=F32) + gb_ref[0]
        for d in range(2):
            r = _sigmoid(pre[:, (2 * d) * LANE:(2 * d + 1) * LANE])
            g_in = _sigmoid(pre[:, (2 * d + 1) * LANE:(2 * d + 2) * LANE])
            log_a = neg_c_sp[d:d + 1, :] * r
            a = jnp.exp(log_a)
            a_s[d, rows, :] = a
            u_s[d, rows, :] = jnp.sqrt(jnp.maximum(1.0 - a * a, 0.0)) * g_in * xb
        return carry

    lax.fori_loop(0, S_ALL // LRU_BLK, gates, 0)

    def ctx_step(t, carry):
        hf, hb = carry
        tb = CTX_LEN - 1 - t
        hf = a_s[0, pl.ds(t, 1), :] * hf + u_s[0, pl.ds(t, 1), :]
        hb = a_s[1, pl.ds(tb, 1), :] * hb + u_s[1, pl.ds(tb, 1), :]
        h_s[0, pl.ds(t, 1), :] = hf
        h_s[1, pl.ds(tb, 1), :] = hb
        return hf, hb

    zero_row = jnp.zeros((1, LANE), F32)
    hf0, hb0 = lax.fori_loop(0, CTX_LEN, ctx_step, (zero_row, zero_row))

    def col_step(i, carry):
        hf, pf, hb, pb = carry
        rf = pl.ds(pl.multiple_of(CTX_LEN + i * GRID_W, GRID_W), GRID_W)
        rb = pl.ds(pl.multiple_of(CTX_LEN + (GRID_H - 1 - i) * GRID_W, GRID_W), GRID_W)
        af, ab = a_s[0, rf, :], a_s[1, rb, :]
        hf = af * hf + u_s[0, rf, :]
        pf = af * pf
        hb = ab * hb + u_s[1, rb, :]
        pb = ab * pb
        h_s[0, rf, :] = hf
        p_s[0, pl.ds(pl.multiple_of(i * GRID_W, GRID_W), GRID_W), :] = pf
        h_s[1, rb, :] = hb
        p_s[1, pl.ds(pl.multiple_of((GRID_H - 1 - i) * GRID_W, GRID_W), GRID_W), :] = pb
        return hf, pf, hb, pb

    zeros = jnp.zeros((GRID_W, LANE), F32)
    ones = jnp.ones((GRID_W, LANE), F32)
    hf, pf, hb, pb = lax.fori_loop(0, GRID_H, col_step, (zeros, ones, zeros, ones))
    end_s[0], end_s[1], end_s[2], end_s[3] = hf, pf, hb, pb

    def chain_step(w, carry):
        cf, cb = carry
        wb = GRID_W - 1 - w
        cin_s[0, pl.ds(w, 1), :] = cf
        cin_s[1, pl.ds(wb, 1), :] = cb
        cf = end_s[0, pl.ds(w, 1), :] + end_s[1, pl.ds(w, 1), :] * cf
        cb = end_s[2, pl.ds(wb, 1), :] + end_s[3, pl.ds(wb, 1), :] * cb
        return cf, cb

    lax.fori_loop(0, GRID_W, chain_step, (hf0, hb0))

    o_ref[0, :CTX_LEN, :] = ((h_s[0, :CTX_LEN, :] + h_s[1, :CTX_LEN, :])
                             * _gelu(y_ref[0, :CTX_LEN, :].astype(F32))).astype(o_ref.dtype)

    def out_step(i, carry):
        rows = pl.ds(pl.multiple_of(CTX_LEN + i * GRID_W, GRID_W), GRID_W)
        prow = pl.ds(pl.multiple_of(i * GRID_W, GRID_W), GRID_W)
        h = (h_s[0, rows, :] + p_s[0, prow, :] * cin_s[0] + h_s[1, rows, :] + p_s[1, prow, :] * cin_s[1])
        o_ref[0, rows, :] = (h * _gelu(y_ref[0, rows, :].astype(F32))).astype(o_ref.dtype)
        return carry

    lax.fori_loop(0, GRID_H, out_step, 0)


def lru_mixer(big, conv_w, conv_b, gate_w, gate_b, lam):
    bsz = big.shape[0]
    gw = gate_w.transpose(2, 3, 0, 1, 4).reshape(R_BLOCKS, R_BDIM, 4 * R_BDIM).astype(BF16)
    gb = gate_b.reshape(2, 2, R_BLOCKS, R_BDIM).transpose(2, 0, 1, 3).reshape(R_BLOCKS, 1, 4 * R_BDIM)
    cw = jnp.pad(conv_w, ((0, 8 - CONV_W), (0, 0)))
    xb, yb = C_RX // R_BDIM, C_RY // R_BDIM
    return pl.pallas_call(
        _lru_kernel,
        out_shape=jax.ShapeDtypeStruct((bsz, S_ALL, BRANCH_W), BF16),
        grid=(bsz, R_BLOCKS),
        in_specs=[pl.BlockSpec((1, S_ALL, R_BDIM), lambda b, n: (b, 0, xb + n)),
                  pl.BlockSpec((1, S_ALL, R_BDIM), lambda b, n: (b, 0, yb + n)),
                  pl.BlockSpec((8, R_BDIM), lambda b, n: (0, n)),
                  pl.BlockSpec((1, R_BDIM), lambda b, n: (0, n)),
                  pl.BlockSpec((1, R_BDIM, 4 * R_BDIM), lambda b, n: (n, 0, 0)),
                  pl.BlockSpec((1, 1, 4 * R_BDIM), lambda b, n: (n, 0, 0)),
                  pl.BlockSpec((2, R_BDIM), lambda b, n: (0, n))],
        out_specs=pl.BlockSpec((1, S_ALL, R_BDIM), lambda b, n: (b, 0, n)),
        scratch_shapes=[pltpu.VMEM((S_ALL, R_BDIM), F32), pltpu.VMEM((2, S_ALL, R_BDIM), F32),
                        pltpu.VMEM((2, S_ALL, R_BDIM), F32), pltpu.VMEM((2, S_ALL, R_BDIM), F32),
                        pltpu.VMEM((2, SEQ, R_BDIM), F32), pltpu.VMEM((4, GRID_W, R_BDIM), F32),
                        pltpu.VMEM((2, GRID_W, R_BDIM), F32)],
        compiler_params=_cparams(("arbitrary", "arbitrary")),
        name="lru_mixer",
    )(big, big, cw, conv_b.reshape(1, BRANCH_W), gw, gb, lam)


def _pack_in_weights(w, b):
    segs_w = [w[:, IN_OFFS[i]:IN_OFFS[i + 1]] for i in range(len(IN_COLS))]
    segs_b = [b[IN_OFFS[i]:IN_OFFS[i + 1]] for i in range(len(IN_COLS))]
    w_big = jnp.concatenate([segs_w[i] for i in BIG_SEGS], axis=1).astype(BF16)
    b_big = jnp.concatenate([segs_b[i] for i in BIG_SEGS]).reshape(1, N_BIG)
    pad = N_SMALL - sum(IN_COLS[i] for i in SMALL_SEGS)
    w_small = jnp.pad(jnp.concatenate([segs_w[i] for i in SMALL_SEGS], axis=1), ((0, 0), (0, pad))).astype(BF16)
    b_small = jnp.pad(jnp.concatenate([segs_b[i] for i in SMALL_SEGS]), (0, pad)).reshape(1, N_SMALL)
    return w_big, b_big, w_small, b_small


def kernel(x, c, ctx, c_ctx, mod_w, mod_b, norm1_g, norm2_g, final_g, w_in, b_in, m_fbias, m_norm_g,
           g_conv, g_a_log, g_dt_bias, g_norm_g, r_conv, r_conv_b, r_gate_w, r_gate_b, r_lambda,
           w_branch, w_out, ffn_w1, ffn_w3, ffn_w2, router_w, router_b, moe_w1, moe_w3, moe_w2):
    bsz = x.shape[0]
    xs = jnp.concatenate([ctx, x], axis=1)
    src = jnp.concatenate([c, c_ctx[None]], axis=0)
    mods = modulation(src, mod_w, mod_b)
    for layer in range(DEPTH):
        mod = mods[layer]
        w_big, b_big, w_small, b_small = _pack_in_weights(w_in[layer], b_in[layer])
        big, small = in_projection(xs, norm1_g[layer], mod, w_big, b_big, w_small, b_small)
        ym = mlstm_mixer(big, small, m_fbias[layer], m_norm_g[layer])
        yg = gdn_mixer(big, small, g_conv[layer], g_a_log[layer], g_dt_bias[layer], g_norm_g[layer])
        yr = lru_mixer(big, r_conv[layer], r_conv_b[layer], r_gate_w[layer], r_gate_b[layer], r_lambda[layer])
        xs = merge_branches(ym, yg, yr, big, xs, mod, w_branch[layer].astype(BF16), w_out[layer].astype(BF16))
        j = layer // 2
        if layer % 2 == 0:
            xs = dense_ffn(xs, norm2_g[layer], mod, ffn_w1[j].astype(BF16), ffn_w3[j].astype(BF16),
                           ffn_w2[j].astype(BF16))
        else:
            gt2 = jnp.concatenate([jnp.broadcast_to(mod[bsz:, None, 5 * D_MODEL:], (bsz, CTX_LEN, D_MODEL)),
                                   jnp.broadcast_to(mod[:bsz, None, 5 * D_MODEL:], (bsz, SEQ, D_MODEL))], axis=1)
            xs = moe_ffn(xs, norm2_g[layer], mod, gt2, router_w[j], router_b[j], moe_w1[j].astype(BF16),
                         moe_w3[j].astype(BF16), moe_w2[j].astype(BF16))
    return final_norm(xs, final_g)
```

```python
import jax
import jax.numpy as jnp
import numpy as np
from jax import lax
from jax.experimental import pallas as pl
from jax.experimental.pallas import tpu as pltpu

D_MODEL = 1024
SEQ = 2048
DEPTH = 4
GRID_W = 64
CTX_LEN = 256
S_ALL = CTX_LEN + SEQ
BRANCH_W = D_MODEL
N_BRANCH = 3
CHUNK = 64
CONV_W = 4
M_HEADS = 4
M_DQK = 128
M_DV = BRANCH_W // M_HEADS
G_HEADS = 8
G_DK = 128
G_DV = BRANCH_W // G_HEADS
R_BLOCKS = 8
R_BDIM = BRANCH_W // R_BLOCKS
LRU_C = 8.0
D_FF = 2816
N_EXPERTS = 8
TOP_K = 2
MOE_BLOCK = 512
EPS = 1e-6
IN_COLS = (M_HEADS * M_DQK, M_HEADS * M_DQK, BRANCH_W, BRANCH_W, 4 * M_HEADS,
           2 * G_HEADS * G_DK + BRANCH_W, BRANCH_W, 2 * G_HEADS, 2 * G_HEADS,
           BRANCH_W, BRANCH_W, N_BRANCH * D_MODEL)
IN_OFFS = tuple(int(v) for v in np.cumsum((0,) + IN_COLS))
SMALL_SEGS = (4, 7, 8)
BIG_SEGS = tuple(i for i in range(len(IN_COLS)) if i not in SMALL_SEGS)
N_BIG = sum(IN_COLS[i] for i in BIG_SEGS)
N_SMALL = 128
C_MQ, C_MK, C_MV, C_MO = 0, 512, 1024, 2048
C_GQ, C_GK, C_GV, C_GZ = 3072, 4096, 5120, 6144
C_RX, C_RY, C_MG = 7168, 8192, 9216

LANE = 128
VMEM_LIMIT = 56 * 1024 * 1024
F32 = jnp.float32
BF16 = jnp.bfloat16


def _cparams(sem):
    return pltpu.CompilerParams(dimension_semantics=sem, vmem_limit_bytes=VMEM_LIMIT)


def _sigmoid(v):
    return 0.5 * jnp.tanh(0.5 * v) + 0.5


def _ada_norm(x, g, mod_l, mod_c, first_row, which):
    y = x * lax.rsqrt(jnp.mean(x * x, axis=-1, keepdims=True) + EPS) * g
    sh_l, sc_l = _mod_slice(mod_l, 3 * which), _mod_slice(mod_l, 3 * which + 1)
    sh_c, sc_c = _mod_slice(mod_c, 3 * which), _mod_slice(mod_c, 3 * which + 1)
    is_ctx = _is_ctx(x.shape[0], first_row)
    return y * (1.0 + jnp.where(is_ctx, sc_c, sc_l)) + jnp.where(is_ctx, sh_c, sh_l)


def _mod_slice(mod, k):
    return mod[:, k * D_MODEL:(k + 1) * D_MODEL]


def _is_ctx(rows, first_row):
    return (lax.broadcasted_iota(jnp.int32, (rows, 1), 0) + first_row) < CTX_LEN


def _mod_kernel(src_ref, w_ref, b_ref, o_ref):
    s = src_ref[...]
    s = (s * _sigmoid(s)).astype(BF16)
    o_ref[0] = jnp.dot(s, w_ref[0].astype(BF16), preferred_element_type=F32) + b_ref[0]


def modulation(src, mod_w, mod_b):
    rows = src.shape[0]
    tn = 1536
    return pl.pallas_call(
        _mod_kernel,
        out_shape=jax.ShapeDtypeStruct((DEPTH, rows, 6 * D_MODEL), F32),
        grid=(DEPTH, 6 * D_MODEL // tn),
        in_specs=[pl.BlockSpec((rows, D_MODEL), lambda l, n: (0, 0)),
                  pl.BlockSpec((1, D_MODEL, tn), lambda l, n: (l, 0, n)),
                  pl.BlockSpec((1, 1, tn), lambda l, n: (l, 0, n))],
        out_specs=pl.BlockSpec((1, rows, tn), lambda l, n: (l, 0, n)),
        compiler_params=_cparams(("arbitrary", "arbitrary")),
        name="modulation",
    )(src, mod_w, mod_b.reshape(DEPTH, 1, 6 * D_MODEL))


IN_TM = 1152
IN_TN = 1024


def _in_kernel(x_ref, g_ref, ml_ref, mc_ref, w_ref, b_ref, ws_ref, bs_ref, o_ref, os_ref, h_ref):
    n = pl.program_id(2)

    @pl.when(n == 0)
    def _():
        h = _ada_norm(x_ref[0], g_ref[...], ml_ref[0], mc_ref[0], pl.program_id(1) * IN_TM, 0)
        h_ref[...] = h.astype(BF16)
        os_ref[0] = jnp.dot(h_ref[...], ws_ref[...], preferred_element_type=F32) + bs_ref[...]

    o_ref[0] = (jnp.dot(h_ref[...], w_ref[...], preferred_element_type=F32) + b_ref[...]).astype(o_ref.dtype)


def in_projection(xs, g, mod, w_big, b_big, w_small, b_small):
    bsz = xs.shape[0]
    mod3 = mod.reshape(bsz + 1, 1, 6 * D_MODEL)
    return pl.pallas_call(
        _in_kernel,
        out_shape=(jax.ShapeDtypeStruct((bsz, S_ALL, N_BIG), BF16),
                   jax.ShapeDtypeStruct((bsz, S_ALL, N_SMALL), F32)),
        grid=(bsz, S_ALL // IN_TM, N_BIG // IN_TN),
        in_specs=[pl.BlockSpec((1, IN_TM, D_MODEL), lambda b, s, n: (b, s, 0)),
                  pl.BlockSpec((1, D_MODEL), lambda b, s, n: (0, 0)),
                  pl.BlockSpec((1, 1, 6 * D_MODEL), lambda b, s, n: (b, 0, 0)),
                  pl.BlockSpec((1, 1, 6 * D_MODEL), lambda b, s, n: (bsz, 0, 0)),
                  pl.BlockSpec((D_MODEL, IN_TN), lambda b, s, n: (0, n)),
                  pl.BlockSpec((1, IN_TN), lambda b, s, n: (0, n)),
                  pl.BlockSpec((D_MODEL, N_SMALL), lambda b, s, n: (0, 0)),
                  pl.BlockSpec((1, N_SMALL), lambda b, s, n: (0, 0))],
        out_specs=(pl.BlockSpec((1, IN_TM, IN_TN), lambda b, s, n: (b, s, n)),
                   pl.BlockSpec((1, IN_TM, N_SMALL), lambda b, s, n: (b, s, 0))),
        scratch_shapes=[pltpu.VMEM((IN_TM, D_MODEL), BF16)],
        compiler_params=_cparams(("arbitrary", "arbitrary", "arbitrary")),
        name="in_projection",
    )(xs, g.reshape(1, D_MODEL), mod3, mod3, w_big, b_big, w_small, b_small)


MERGE_TM = 768


def _merge_kernel(ym_ref, yg_ref, yr_ref, g0_ref, g1_ref, g2_ref, x_ref, ml_ref, mc_ref, wb_ref, wo_ref, o_ref):
    acc = None
    for n, (y_ref, gp_ref) in enumerate(((ym_ref, g0_ref), (yg_ref, g1_ref), (yr_ref, g2_ref))):
        p = jnp.dot(y_ref[0].astype(BF16), wb_ref[n], preferred_element_type=F32)
        p = p * _sigmoid(gp_ref[0].astype(F32))
        acc = p if acc is None else acc + p
    out = jnp.dot(acc.astype(BF16), wo_ref[...], preferred_element_type=F32)
    is_ctx = _is_ctx(MERGE_TM, pl.program_id(1) * MERGE_TM)
    gt = jnp.where(is_ctx, _mod_slice(mc_ref[0], 2), _mod_slice(ml_ref[0], 2))
    o_ref[0] = x_ref[0] + gt * out


def merge_branches(ym, yg, yr, big, xs, mod, w_branch, w_out):
    bsz = xs.shape[0]
    mod3 = mod.reshape(bsz + 1, 1, 6 * D_MODEL)
    row = lambda b, s: (b, s, 0)
    gate_blk = C_MG // D_MODEL
    const = pl.Buffered(1)
    return pl.pallas_call(
        _merge_kernel,
        out_shape=jax.ShapeDtypeStruct(xs.shape, F32),
        grid=(bsz, S_ALL // MERGE_TM),
        in_specs=[pl.BlockSpec((1, MERGE_TM, D_MODEL), row),
                  pl.BlockSpec((1, MERGE_TM, D_MODEL), row),
                  pl.BlockSpec((1, MERGE_TM, D_MODEL), row),
                  pl.BlockSpec((1, MERGE_TM, D_MODEL), lambda b, s: (b, s, gate_blk)),
                  pl.BlockSpec((1, MERGE_TM, D_MODEL), lambda b, s: (b, s, gate_blk + 1)),
                  pl.BlockSpec((1, MERGE_TM, D_MODEL), lambda b, s: (b, s, gate_blk + 2)),
                  pl.BlockSpec((1, MERGE_TM, D_MODEL), row),
                  pl.BlockSpec((1, 1, 6 * D_MODEL), lambda b, s: (b, 0, 0)),
                  pl.BlockSpec((1, 1, 6 * D_MODEL), lambda b, s: (bsz, 0, 0)),
                  pl.BlockSpec((N_BRANCH, BRANCH_W, D_MODEL), lambda b, s: (0, 0, 0), pipeline_mode=const),
                  pl.BlockSpec((D_MODEL, D_MODEL), lambda b, s: (0, 0), pipeline_mode=const)],
        out_specs=pl.BlockSpec((1, MERGE_TM, D_MODEL), row),
        compiler_params=_cparams(("arbitrary", "arbitrary")),
        name="merge_branches",
    )(ym, yg, yr, big, big, big, xs, mod3, mod3, w_branch, w_out)


FFN_TM = 1152
FFN_TF = 256


def _swiglu_chunk(h, w1, w3, w2):
    a = jnp.dot(h, w1, preferred_element_type=F32)
    b = jnp.dot(h, w3, preferred_element_type=F32)
    return jnp.dot((a * _sigmoid(a) * b).astype(BF16), w2, preferred_element_type=F32)


def _ffn_kernel(x_ref, g_ref, ml_ref, mc_ref, w1_ref, w3_ref, w2_ref, o_ref, h_ref, acc_ref):
    f = pl.program_id(2)

    @pl.when(f == 0)
    def _():
        h = _ada_norm(x_ref[0], g_ref[...], ml_ref[0], mc_ref[0], pl.program_id(1) * FFN_TM, 1)
        h_ref[...] = h.astype(BF16)
        acc_ref[...] = jnp.zeros_like(acc_ref)

    acc_ref[...] += _swiglu_chunk(h_ref[...], w1_ref[...], w3_ref[...], w2_ref[...])

    @pl.when(f == pl.num_programs(2) - 1)
    def _():
        is_ctx = _is_ctx(FFN_TM, pl.program_id(1) * FFN_TM)
        gt = jnp.where(is_ctx, _mod_slice(mc_ref[0], 5), _mod_slice(ml_ref[0], 5))
        o_ref[0] = x_ref[0] + gt * acc_ref[...]


def dense_ffn(xs, g, mod, w1, w3, w2):
    bsz = xs.shape[0]
    mod3 = mod.reshape(bsz + 1, 1, 6 * D_MODEL)
    return pl.pallas_call(
        _ffn_kernel,
        out_shape=jax.ShapeDtypeStruct(xs.shape, F32),
        grid=(bsz, S_ALL // FFN_TM, D_FF // FFN_TF),
        in_specs=[pl.BlockSpec((1, FFN_TM, D_MODEL), lambda b, s, f: (b, s, 0)),
                  pl.BlockSpec((1, D_MODEL), lambda b, s, f: (0, 0)),
                  pl.BlockSpec((1, 1, 6 * D_MODEL), lambda b, s, f: (b, 0, 0)),
                  pl.BlockSpec((1, 1, 6 * D_MODEL), lambda b, s, f: (bsz, 0, 0)),
                  pl.BlockSpec((D_MODEL, FFN_TF), lambda b, s, f: (0, f)),
                  pl.BlockSpec((D_MODEL, FFN_TF), lambda b, s, f: (0, f)),
                  pl.BlockSpec((FFN_TF, D_MODEL), lambda b, s, f: (f, 0))],
        out_specs=pl.BlockSpec((1, FFN_TM, D_MODEL), lambda b, s, f: (b, s, 0)),
        scratch_shapes=[pltpu.VMEM((FFN_TM, D_MODEL), BF16), pltpu.VMEM((FFN_TM, D_MODEL), F32)],
        compiler_params=_cparams(("arbitrary", "arbitrary", "arbitrary")),
        name="dense_ffn",
    )(xs, g.reshape(1, D_MODEL), mod3, mod3, w1, w3, w2)


NORM_TM = 768


def _norm_router_kernel(x_ref, g_ref, ml_ref, mc_ref, rw_ref, rb_ref, h_ref, lg_ref):
    h = _ada_norm(x_ref[0], g_ref[...], ml_ref[0], mc_ref[0], pl.program_id(1) * NORM_TM, 1)
    h_ref[0] = h.astype(h_ref.dtype)
    lg_ref[0] = jnp.dot(h, rw_ref[...], preferred_element_type=F32, precision=lax.Precision.HIGHEST) + rb_ref[...]


def norm_router(xs, g, mod, router_w, router_b):
    bsz = xs.shape[0]
    mod3 = mod.reshape(bsz + 1, 1, 6 * D_MODEL)
    rw = jnp.zeros((D_MODEL, LANE), F32).at[:, :N_EXPERTS].set(router_w)
    rb = jnp.zeros((1, LANE), F32).at[0, :N_EXPERTS].set(router_b)
    return pl.pallas_call(
        _norm_router_kernel,
        out_shape=(jax.ShapeDtypeStruct(xs.shape, BF16), jax.ShapeDtypeStruct((bsz, S_ALL, LANE), F32)),
        grid=(bsz, S_ALL // NORM_TM),
        in_specs=[pl.BlockSpec((1, NORM_TM, D_MODEL), lambda b, s: (b, s, 0)),
                  pl.BlockSpec((1, D_MODEL), lambda b, s: (0, 0)),
                  pl.BlockSpec((1, 1, 6 * D_MODEL), lambda b, s: (b, 0, 0)),
                  pl.BlockSpec((1, 1, 6 * D_MODEL), lambda b, s: (bsz, 0, 0)),
                  pl.BlockSpec((D_MODEL, LANE), lambda b, s: (0, 0)),
                  pl.BlockSpec((1, LANE), lambda b, s: (0, 0))],
        out_specs=(pl.BlockSpec((1, NORM_TM, D_MODEL), lambda b, s: (b, s, 0)),
                   pl.BlockSpec((1, NORM_TM, LANE), lambda b, s: (b, s, 0))),
        compiler_params=_cparams(("arbitrary", "arbitrary")),
        name="norm_router",
    )(xs, g.reshape(1, D_MODEL), mod3, mod3, rw, rb)


def _expert_kernel(be_ref, x_ref, w1_ref, w3_ref, w2_ref, o_ref):
    del be_ref
    x = x_ref[...]
    for f in range(D_FF // FFN_TF):
        cols = slice(f * FFN_TF, (f + 1) * FFN_TF)
        y = _swiglu_chunk(x, w1_ref[0, :, cols], w3_ref[0, :, cols], w2_ref[0, cols, :])
        if f == 0:
            o_ref[...] = y
        else:
            o_ref[...] += y


def grouped_experts(xb, block_e, w1, w3, w2):
    n_rows = xb.shape[0]
    const = pl.Buffered(1)
    grid_spec = pltpu.PrefetchScalarGridSpec(
        num_scalar_prefetch=1,
        grid=(n_rows // MOE_BLOCK,),
        in_specs=[pl.BlockSpec((MOE_BLOCK, D_MODEL), lambda i, be: (i, 0)),
                  pl.BlockSpec((1, D_MODEL, D_FF), lambda i, be: (be[i], 0, 0), pipeline_mode=const),
                  pl.BlockSpec((1, D_MODEL, D_FF), lambda i, be: (be[i], 0, 0), pipeline_mode=const),
                  pl.BlockSpec((1, D_FF, D_MODEL), lambda i, be: (be[i], 0, 0), pipeline_mode=const)],
        out_specs=pl.BlockSpec((MOE_BLOCK, D_MODEL), lambda i, be: (i, 0)),
    )
    return pl.pallas_call(
        _expert_kernel,
        out_shape=jax.ShapeDtypeStruct((n_rows, D_MODEL), F32),
        grid_spec=grid_spec,
        compiler_params=_cparams(("arbitrary",)),
        name="grouped_experts",
    )(block_e, xb, w1, w3, w2)


def moe_ffn(xs, g, mod, gt2, router_w, router_b, w1, w3, w2):
    bsz = xs.shape[0]
    h, logits = norm_router(xs, g, mod, router_w, router_b)
    h = h.reshape(-1, D_MODEL)
    n_tok = h.shape[0]
    top_logit, top_e = lax.top_k(logits.reshape(n_tok, LANE)[:, :N_EXPERTS], TOP_K)
    gate = jax.nn.softmax(top_logit, axis=-1)
    n_assign = n_tok * TOP_K
    flat_e = top_e.reshape(-1)
    flat_tok = jnp.repeat(jnp.arange(n_tok, dtype=jnp.int32), TOP_K)
    order = jnp.argsort(flat_e)
    sorted_e = flat_e[order]
    counts = jnp.bincount(flat_e, length=N_EXPERTS)
    padded = (counts + MOE_BLOCK - 1) // MOE_BLOCK * MOE_BLOCK
    start = jnp.cumsum(counts) - counts
    p_end = jnp.cumsum(padded)
    p_start = p_end - padded
    dest = p_start[sorted_e] + jnp.arange(n_assign, dtype=jnp.int32) - start[sorted_e]
    n_blocks = -(-n_assign // MOE_BLOCK) + N_EXPERTS
    n_rows = n_blocks * MOE_BLOCK
    row_tok = jnp.zeros((n_rows,), jnp.int32).at[dest].set(flat_tok[order])
    slot = jnp.zeros((n_assign,), jnp.int32).at[order].set(dest).reshape(n_tok, TOP_K)
    block_e = jnp.minimum(jnp.searchsorted(p_end, jnp.arange(n_blocks) * MOE_BLOCK, side='right'),
                          N_EXPERTS - 1).astype(jnp.int32)
    yb = grouped_experts(h[row_tok], block_e, w1, w3, w2)
    y = sum(yb[slot[:, k]] * gate[:, k:k + 1] for k in range(TOP_K))
    return xs + gt2 * y.reshape(bsz, S_ALL, D_MODEL)


FINAL_TM = 256


def _final_kernel(x_ref, g_ref, o_ref):
    x = x_ref[0]
    o_ref[0] = x * lax.rsqrt(jnp.mean(x * x, axis=-1, keepdims=True) + EPS) * g_ref[...]


def final_norm(xs, g):
    bsz = xs.shape[0]
    skip = CTX_LEN // FINAL_TM
    return pl.pallas_call(
        _final_kernel,
        out_shape=jax.ShapeDtypeStruct((bsz, SEQ, D_MODEL), F32),
        grid=(bsz, SEQ // FINAL_TM),
        in_specs=[pl.BlockSpec((1, FINAL_TM, D_MODEL), lambda b, s: (b, s + skip, 0)),
                  pl.BlockSpec((1, D_MODEL), lambda b, s: (0, 0))],
        out_specs=pl.BlockSpec((1, FINAL_TM, D_MODEL), lambda b, s: (b, s, 0)),
        compiler_params=_cparams(("arbitrary", "arbitrary")),
        name="final_norm",
    )(xs, g.reshape(1, D_MODEL))


N_CHUNKS = S_ALL // CHUNK
CTX_CHUNKS = CTX_LEN // CHUNK
GATE_ROWS = 8
NEG_INF = float("-inf")


def _softplus(v):
    return jnp.maximum(v, 0.0) + jnp.log(1.0 + jnp.exp(-jnp.abs(v)))


def _scan_constants():
    s = np.arange(CHUNK)[:, None]
    t = np.arange(CHUNK)[None, :]
    tri = np.concatenate([(s <= t), (s >= t), np.ones((CHUNK, CHUNK), bool)], axis=1).astype(np.float32)
    return jnp.asarray(tri), jnp.asarray(np.eye(CHUNK, dtype=np.float32))


def _rev_chunk(j):
    return jnp.where(j < CTX_CHUNKS, CTX_CHUNKS - 1 - j, N_CHUNKS + CTX_CHUNKS - 1 - j)


def _head_gate_rows(small, c0, heads, kind_major):
    bsz = small.shape[0]
    a = small[:, :, c0:c0 + 4 * heads].reshape(bsz, N_CHUNKS, CHUNK, 2, 2, heads)
    a = a.transpose((0, 5, 1, 4, 3, 2) if kind_major else (0, 5, 1, 3, 4, 2))
    a = a.reshape(bsz, heads, N_CHUNKS, 4, CHUNK)
    return jnp.pad(a, ((0, 0), (0, 0), (0, 0), (0, GATE_ROWS - 4), (0, 0)))


def _tn_dot(a, b):
    return lax.dot_general(a, b, (((0,), (0,)), ((), ())), preferred_element_type=F32)


def _nt_dot(a, b, precision=None):
    return lax.dot_general(a, b, (((1,), (1,)), ((), ())), preferred_element_type=F32, precision=precision)


def _dir_mask(d):
    t = lax.broadcasted_iota(jnp.int32, (CHUNK, CHUNK), 0)
    s = lax.broadcasted_iota(jnp.int32, (CHUNK, CHUNK), 1)
    return (s <= t, s < t) if d == 0 else (s >= t, s > t)


def _split_dot(a, ones, dot_fn):
    b = ones.astype(BF16)
    hi = a.astype(BF16)
    lo = (a - hi.astype(F32)).astype(BF16)
    return dot_fn(hi, b, preferred_element_type=F32) + dot_fn(lo, b, preferred_element_type=F32)


def _gate_forms(vals, tri, eye):
    cum = _split_dot(vals, tri, jnp.dot)
    kind = lax.broadcasted_iota(jnp.int32, vals.shape, 0) & (GATE_ROWS - 1)
    rows = jnp.where(kind == 1, cum[:, :CHUNK], jnp.where(kind == 3, cum[:, CHUNK:2 * CHUNK], vals))
    cols = _split_dot(rows, eye, lambda a, b, **kw: _nt_dot(b, a))
    return rows, cols, cum[:, 2 * CHUNK:2 * CHUNK + 1]


def _x_row(rows, d, g=0):
    return rows[GATE_ROWS * g + 2 * d:GATE_ROWS * g + 2 * d + 1]


def _run_row(rows, d, g=0):
    return rows[GATE_ROWS * g + 2 * d + 1:GATE_ROWS * g + 2 * d + 2]


def _x_col(cols, d, g=0):
    return cols[:, GATE_ROWS * g + 2 * d:GATE_ROWS * g + 2 * d + 1]


def _run_col(cols, d, g=0):
    return cols[:, GATE_ROWS * g + 2 * d + 1:GATE_ROWS * g + 2 * d + 2]


PAIRS_PER_STEP = 3
OUT_PAIRS_PER_STEP = 1


def _mlstm_kernel(q_ref, k_ref, v_ref, o_ref, g_ref, fb_ref, ng_ref, tri_ref, eye_ref, y_ref,
                  c_s, st_s, gr_s, gc_s):
    tri, eye = tri_ref[...], eye_ref[...]
    fbias = fb_ref[0]
    scale = M_DQK ** -0.5
    kind = lax.broadcasted_iota(jnp.int32, (2 * GATE_ROWS, CHUNK), 0) & 1
    fbias2 = jnp.concatenate([fbias, fbias], axis=0)
    st_row = lax.broadcasted_iota(jnp.int32, (GATE_ROWS, M_DQK), 0)

    def pair_rows(c0):
        return pl.ds(pl.multiple_of(c0 * CHUNK, 2 * CHUNK), 2 * CHUNK)

    def state_rows(c):
        return pl.ds(pl.multiple_of(c * M_DQK, M_DQK), M_DQK)

    def pair_state_rows(c0):
        return pl.ds(pl.multiple_of(c0 * M_DQK, 2 * M_DQK), 2 * M_DQK)

    def local(it, carry):
        pairs = [it * PAIRS_PER_STEP + p for p in range(PAIRS_PER_STEP)]
        forms = []
        for g in pairs:
            raw = jnp.concatenate([g_ref[0, 0, 2 * g], g_ref[0, 0, 2 * g + 1]], axis=0)
            vals = jnp.where(kind == 1, -_softplus(-(raw + fbias2)), raw)
            forms.append(_gate_forms(vals, tri, eye))
        zero = jnp.zeros((CHUNK, 2 * M_DQK), F32)
        lhs_all = []
        for g, (rows, cols, totals) in zip(pairs, forms):
            gr_s[g] = rows
            gc_s[g] = cols
            kf2 = k_ref[0, pair_rows(2 * g), :].astype(F32)
            lhs = []
            for i in range(2):
                kf = kf2[i * CHUNK:(i + 1) * CHUNK]
                kws = []
                for d in range(2):
                    tot = totals[GATE_ROWS * i + 2 * d + 1:GATE_ROWS * i + 2 * d + 2]
                    m_loc = jnp.max(tot - _run_row(rows, d, i) + _x_row(rows, d, i), axis=-1, keepdims=True)
                    kw = kf * (jnp.exp(tot - _run_col(cols, d, i) + _x_col(cols, d, i) - m_loc) * scale)
                    kws.append(kw)
                    st_s[d, 2 * g + i] = jnp.where(st_row == 0, jnp.sum(kw, axis=0, keepdims=True),
                                                   jnp.where(st_row == 1, m_loc, tot))
                lhs.append(jnp.concatenate(kws + [zero] if i == 0 else [zero] + kws, axis=1))
            lhs_all.append(jnp.concatenate(lhs, axis=0).astype(BF16))
        for g, lhs in zip(pairs, lhs_all):
            c_loc = _tn_dot(lhs, v_ref[0, pair_rows(2 * g), :])
            for i in range(2):
                for d in range(2):
                    c_s[d, state_rows(2 * g + i), :] = c_loc[(2 * i + d) * M_DQK:(2 * i + d + 1) * M_DQK]
        return carry

    lax.fori_loop(0, N_CHUNKS // (2 * PAIRS_PER_STEP), local, 0)

    def scan(d):
        def body(j, carry):
            c_prev, n_prev, m_prev = carry
            c = j if d == 0 else _rev_chunk(j)
            st = st_s[d, c]
            n_loc, m_loc, tot = st[0:1], st[1:2, :1], st[2:3, :1]
            c_loc = c_s[d, state_rows(c), :]
            c_s[d, state_rows(c), :] = c_prev
            st_s[d, c] = jnp.where(st_row == 0, n_prev, m_prev)
            m_new = jnp.maximum(tot + m_prev, m_loc)
            s_prev = jnp.exp(tot + m_prev - m_new)
            s_loc = jnp.exp(m_loc - m_new)
            return s_prev * c_prev + s_loc * c_loc, s_prev * n_prev + s_loc * n_loc, m_new

        lax.fori_loop(0, N_CHUNKS, body, (jnp.zeros((M_DQK, M_DV), F32), jnp.zeros((1, M_DQK), F32),
                                          jnp.zeros((1, 1), F32)))

    scan(0)
    scan(1)

    def output(it, carry):
        pairs = [it * OUT_PAIRS_PER_STEP + p for p in range(OUT_PAIRS_PER_STEP)]
        qk_all = [_nt_dot(q_ref[0, pair_rows(2 * g), :], k_ref[0, pair_rows(2 * g), :]) * scale for g in pairs]
        z_intra = jnp.zeros((CHUNK, 2 * CHUNK), F32)
        z_inter = jnp.zeros((CHUNK, M_DQK), F32)
        operands = []
        for g, qk2 in zip(pairs, qk_all):
            c0 = 2 * g
            qf2, v2 = q_ref[0, pair_rows(c0), :].astype(F32), v_ref[0, pair_rows(c0), :]
            rows, cols = gr_s[g], gc_s[g]
            lhs_intra, lhs_inter, rhs_v = [], [], []
            for i in range(2):
                qk = qk2[i * CHUNK:(i + 1) * CHUNK, i * CHUNK:(i + 1) * CHUNK]
                qf = qf2[i * CHUNK:(i + 1) * CHUNK]
                intra, inter = [], []
                for d in range(2):
                    st = st_s[d, c0 + i]
                    b_c = _run_col(cols, d, i)
                    n0, m0 = st[0:1], st[1:2, :1]
                    incl, _ = _dir_mask(d)
                    d_log = jnp.where(incl, b_c - _run_row(rows, d, i) + _x_row(rows, d, i), NEG_INF)
                    m_inter = b_c + m0
                    m_comb = jnp.maximum(m_inter, jnp.max(d_log, axis=-1, keepdims=True))
                    s = qk * jnp.exp(d_log - m_comb)
                    e_inter = jnp.exp(m_inter - m_comb)
                    den = (jnp.sum(s, axis=-1, keepdims=True)
                           + e_inter * jnp.sum(qf * n0, axis=-1, keepdims=True))
                    inv = 1.0 / jnp.maximum(jnp.abs(den), jnp.exp(-m_comb))
                    intra.append(s * inv)
                    inter.append(qf * (e_inter * inv))
                pair = jnp.concatenate(intra, axis=1)
                lhs_intra.append(jnp.concatenate([pair, z_intra] if i == 0 else [z_intra, pair], axis=1))
                lhs_inter.append(jnp.concatenate([inter[0], z_inter, inter[1], z_inter] if i == 0
                                                 else [z_inter, inter[0], z_inter, inter[1]], axis=1))
                v = v2[i * CHUNK:(i + 1) * CHUNK]
                rhs_v += [v, v]
            c_in = jnp.concatenate([c_s[0, pair_state_rows(c0), :], c_s[1, pair_state_rows(c0), :]], axis=0)
            operands.append((jnp.concatenate(lhs_intra, axis=0).astype(BF16), jnp.concatenate(rhs_v, axis=0),
                             jnp.concatenate(lhs_inter, axis=0).astype(BF16), c_in.astype(BF16)))
        hs = [jnp.dot(li, rv, preferred_element_type=F32) + jnp.dot(le, ci, preferred_element_type=F32)
              for li, rv, le, ci in operands]
        for g, h in zip(pairs, hs):
            h = h * lax.rsqrt(jnp.mean(h * h, axis=-1, keepdims=True) + EPS) * ng_ref[...]
            y_ref[0, pair_rows(2 * g), :] = (h * _sigmoid(o_ref[0, pair_rows(2 * g), :].astype(F32))
                                             ).astype(y_ref.dtype)
        return carry

    lax.fori_loop(0, N_CHUNKS // (2 * OUT_PAIRS_PER_STEP), output, 0)


def mlstm_mixer(big, small, f_bias, norm_g):
    bsz = big.shape[0]
    gates = _head_gate_rows(small, 0, M_HEADS, False)
    fb = jnp.zeros((M_HEADS, GATE_ROWS, CHUNK), F32)
    fb = fb.at[:, 1].set(f_bias[0][:, None]).at[:, 3].set(f_bias[1][:, None])
    tri, eye = _scan_constants()
    qb, kb, vb, ob = C_MQ // M_DQK, C_MK // M_DQK, C_MV // M_DV, C_MO // M_DV
    return pl.pallas_call(
        _mlstm_kernel,
        out_shape=jax.ShapeDtypeStruct((bsz, S_ALL, BRANCH_W), BF16),
        grid=(bsz, M_HEADS),
        in_specs=[pl.BlockSpec((1, S_ALL, M_DQK), lambda b, h: (b, 0, qb + h)),
                  pl.BlockSpec((1, S_ALL, M_DQK), lambda b, h: (b, 0, kb + h)),
                  pl.BlockSpec((1, S_ALL, M_DV), lambda b, h: (b, 0, vb + h)),
                  pl.BlockSpec((1, S_ALL, M_DV), lambda b, h: (b, 0, ob + h)),
                  pl.BlockSpec((1, 1, N_CHUNKS, GATE_ROWS, CHUNK), lambda b, h: (b, h, 0, 0, 0)),
                  pl.BlockSpec((1, GATE_ROWS, CHUNK), lambda b, h: (h, 0, 0)),
                  pl.BlockSpec((1, M_DV), lambda b, h: (0, h)),
                  pl.BlockSpec((CHUNK, 3 * CHUNK), lambda b, h: (0, 0)),
                  pl.BlockSpec((CHUNK, CHUNK), lambda b, h: (0, 0))],
        out_specs=pl.BlockSpec((1, S_ALL, M_DV), lambda b, h: (b, 0, h)),
        scratch_shapes=[pltpu.VMEM((2, N_CHUNKS * M_DQK, M_DV), F32),
                        pltpu.VMEM((2, N_CHUNKS, GATE_ROWS, M_DQK), F32),
                        pltpu.VMEM((N_CHUNKS // 2, 2 * GATE_ROWS, CHUNK), F32),
                        pltpu.VMEM((N_CHUNKS // 2, CHUNK, 2 * GATE_ROWS), F32)],
        compiler_params=_cparams(("arbitrary", "arbitrary")),
        name="mlstm_mixer",
    )(big, big, big, big, gates, fb, norm_g.reshape(1, BRANCH_W), tri, eye)


CONV_BLK = 256
CONV_HALO = 16
SOLVE_BLK = 2
SOLVE_GROUP = 4
PREP_GROUPS = 3


def _conv_silu(src_ref, w, dst_ref, l2_scale):
    n_blk = S_ALL // CONV_BLK
    zeros = jnp.zeros((CONV_HALO, src_ref.shape[-1]), F32)
    for i in range(n_blk):
        lo = i * CONV_BLK
        first = i == 0 or lo == CTX_LEN
        last = lo + CONV_BLK in (CTX_LEN, S_ALL)
        body = src_ref[0, lo - (0 if first else CONV_HALO):lo + CONV_BLK + (0 if last else CONV_HALO), :]
        win = jnp.concatenate(([zeros] if first else []) + [body.astype(F32)] + ([zeros] if last else []), axis=0)
        n = win.shape[0]
        y = None
        for j in range(CONV_W):
            tap = win if j == CONV_W // 2 else pltpu.roll(win, (CONV_W // 2 - j) % n, 0)
            term = tap[CONV_HALO:CONV_HALO + CONV_BLK] * w[j:j + 1, :]
            y = term if y is None else y + term
        y = y * _sigmoid(y)
        if l2_scale is not None:
            y = y * (lax.rsqrt(jnp.sum(y * y, axis=-1, keepdims=True) + EPS) * l2_scale)
        dst_ref[lo:lo + CONV_BLK, :] = y.astype(dst_ref.dtype)


def _block_masks():
    n = SOLVE_GROUP * CHUNK
    t = np.arange(n)[:, None]
    s = np.arange(n)[None, :]
    masks = [(t // SOLVE_BLK) == (s // SOLVE_BLK)]
    k = SOLVE_BLK
    while k < CHUNK:
        masks.append(((t // (2 * k)) == (s // (2 * k))) & ((t // k) != (s // k)))
        k *= 2
    return jnp.asarray(np.stack(masks).astype(np.float32))


def _unit_triangular_inverse(systems, eye, masks):
    ts = [eye - a * masks[0] for a in systems]
    for lvl in range(1, masks.shape[0]):
        tbs = [t.astype(BF16) for t in ts]
        tls = [jnp.dot(tb, (a * masks[lvl]).astype(BF16), preferred_element_type=F32)
               for tb, a in zip(tbs, systems)]
        ts = [t - jnp.dot(tl.astype(BF16), tb, preferred_element_type=F32) for t, tl, tb in zip(ts, tls, tbs)]
    return ts


def _block_diag(blocks):
    n = len(blocks)
    z_half = jnp.zeros((CHUNK, CHUNK), F32)
    z_tile = jnp.zeros((CHUNK, 2 * CHUNK), F32)
    out = []
    for p, blk in enumerate(blocks):
        pair = jnp.concatenate([blk, z_half] if p % 2 == 0 else [z_half, blk], axis=1)
        out.append(jnp.concatenate([pair if t == p // 2 else z_tile for t in range(n // 2)], axis=1))
    return jnp.concatenate(out, axis=0)


def _gdn_kernel(xq_ref, xk_ref, xv_ref, z_ref, g_ref, gp_ref, cw_ref, ng_ref, tri_ref, eye_ref, bm_ref, y_ref,
                q_s, k_s, v_s, u_s, wq_s, at_s, kt_s, cd_s, acc_ref):
    tri, eye, masks = tri_ref[...], eye_ref[...], bm_ref[...]
    cw = cw_ref[0]
    _conv_silu(xq_ref, cw[:, :G_DK], q_s, G_DK ** -0.5)
    _conv_silu(xk_ref, cw[:, G_DK:2 * G_DK], k_s, 1.0)
    _conv_silu(xv_ref, cw[:, 2 * G_DK:], v_s, None)
    n_pair = SOLVE_GROUP // 2
    kind = lax.broadcasted_iota(jnp.int32, (n_pair * GATE_ROWS, CHUNK), 0) & 1
    neg_a = jnp.concatenate([gp_ref[0, 0]] * n_pair, axis=0)
    dt_bias = jnp.concatenate([gp_ref[0, 1]] * n_pair, axis=0)
    n_bd = SOLVE_GROUP * CHUNK
    eye_bd = (lax.broadcasted_iota(jnp.int32, (n_bd, n_bd), 0)
              == lax.broadcasted_iota(jnp.int32, (n_bd, n_bd), 1)).astype(F32)
    z_half = jnp.zeros((CHUNK, CHUNK), F32)

    def chunk_rows(c):
        return pl.ds(pl.multiple_of(c * CHUNK, CHUNK), CHUNK)

    def wq_rows(c):
        return pl.ds(pl.multiple_of(c * 2 * CHUNK, 2 * CHUNK), 2 * CHUNK)

    def prep(g, carry):
        systems, rhs_all, q_dec_all = [], [], []
        for sgrp in range(PREP_GROUPS):
            c0 = (g * PREP_GROUPS + sgrp) * n_pair
            raw = jnp.concatenate([g_ref[0, 0, c0 + i] for i in range(n_pair)], axis=0)
            vals = jnp.where(kind == 1, neg_a * _softplus(raw + dt_bias), _sigmoid(raw))
            g_rows, g_cols, totals = _gate_forms(vals, tri, eye)
            blocks, rhs, q_dec = [], [], []
            for i in range(n_pair):
                c = c0 + i
                q, k, v = q_s[chunk_rows(c), :], k_s[chunk_rows(c), :], v_s[chunk_rows(c), :]
                kf, vf, qf = k.astype(F32), v.astype(F32), q.astype(F32)
                kq = _nt_dot(jnp.concatenate([k, q], axis=0), k)
                kk, qk = kq[:CHUNK], kq[CHUNK:]
                for d in range(2):
                    gam_r, beta_c, gam_c = _run_row(g_rows, d, i), _x_col(g_cols, d, i), _run_col(g_cols, d, i)
                    tot = totals[GATE_ROWS * i + 2 * d + 1:GATE_ROWS * i + 2 * d + 2]
                    incl, strict = _dir_mask(d)
                    decay = jnp.exp(jnp.where(incl, gam_c - gam_r, NEG_INF))
                    blocks.append(jnp.where(strict, kk * beta_c * decay, 0.0))
                    e_gam = jnp.exp(gam_c)
                    rhs.append(jnp.concatenate([vf * beta_c, kf * (beta_c * e_gam)], axis=-1))
                    q_dec.append(qf * e_gam)
                    attn = qk * decay
                    at_s[d, chunk_rows(c), :] = jnp.concatenate([attn, z_half] if d == 0 else [z_half, attn],
                                                                axis=1).astype(BF16)
                    kt_s[d, chunk_rows(c), :] = (kf * jnp.exp(tot - gam_c)).astype(BF16)
                    cd_s[d, c] = jnp.broadcast_to(jnp.exp(tot), cd_s.shape[2:])
            systems.append(_block_diag(blocks))
            rhs_all.append(jnp.concatenate(rhs, axis=0).astype(BF16))
            q_dec_all.append(q_dec)
        t_invs = _unit_triangular_inverse(systems, eye_bd, masks)
        for sgrp in range(PREP_GROUPS):
            c0 = (g * PREP_GROUPS + sgrp) * n_pair
            x = jnp.dot(t_invs[sgrp].astype(BF16), rhs_all[sgrp], preferred_element_type=F32)
            for i in range(n_pair):
                for d in range(2):
                    p = 2 * i + d
                    xp = x[p * CHUNK:(p + 1) * CHUNK]
                    u_s[d, chunk_rows(c0 + i), :] = xp[:, :G_DV]
                    wq_s[d, wq_rows(c0 + i), :] = jnp.concatenate([xp[:, G_DV:], q_dec_all[sgrp][p]],
                                                                  axis=0).astype(BF16)
        return carry

    lax.fori_loop(0, N_CHUNKS // (n_pair * PREP_GROUPS), prep, 0)

    z_v = jnp.zeros((CHUNK, G_DV), BF16)

    def step(j, s_prev):
        cs = (j, _rev_chunk(j))
        ws_qs = [jnp.dot(wq_s[d, wq_rows(cs[d]), :], s_prev[d].astype(BF16), preferred_element_type=F32)
                 for d in range(2)]
        v_new = [(u_s[d, chunk_rows(cs[d]), :] - ws_qs[d][:CHUNK]).astype(BF16) for d in range(2)]
        o = [jnp.dot(at_s[d, chunk_rows(cs[d]), :],
                     jnp.concatenate([v_new[d], z_v] if d == 0 else [z_v, v_new[d]], axis=0),
                     preferred_element_type=F32) for d in range(2)]
        ktv = [_tn_dot(kt_s[d, chunk_rows(cs[d]), :], v_new[d]) for d in range(2)]
        for d in range(2):
            acc_ref[d, chunk_rows(cs[d]), :] = ws_qs[d][CHUNK:] + o[d]
        return tuple(cd_s[d, cs[d]][:1, :] * s_prev[d] + ktv[d] for d in range(2))

    s_zero = jnp.zeros((G_DK, G_DV), F32)
    lax.fori_loop(0, N_CHUNKS, step, (s_zero, s_zero))

    def finish(c, carry):
        rows = pl.ds(pl.multiple_of(c * CHUNK, CHUNK), CHUNK)
        h = acc_ref[0, rows, :] + acc_ref[1, rows, :]
        h = h * lax.rsqrt(jnp.mean(h * h, axis=-1, keepdims=True) + EPS) * ng_ref[...]
        z = z_ref[0, rows, :].astype(F32)
        y_ref[0, rows, :] = (h * (z * _sigmoid(z))).astype(y_ref.dtype)
        return carry

    lax.fori_loop(0, N_CHUNKS, finish, 0, unroll=4)


def gdn_mixer(big, small, conv_w, a_log, dt_bias, norm_g):
    bsz = big.shape[0]
    gates = _head_gate_rows(small, 4 * M_HEADS, G_HEADS, True)
    gp = jnp.zeros((G_HEADS, 2, GATE_ROWS, CHUNK), F32)
    for d in range(2):
        gp = gp.at[:, 0, 2 * d + 1].set(-jnp.exp(a_log[d])[:, None]).at[:, 1, 2 * d + 1].set(dt_bias[d][:, None])
    cw = conv_w.reshape(CONV_W, 3, G_HEADS, G_DK).transpose(2, 0, 1, 3).reshape(G_HEADS, CONV_W, 3 * G_DK)
    cw = jnp.pad(cw, ((0, 0), (0, 8 - CONV_W), (0, 0)))
    tri, eye = _scan_constants()
    masks = _block_masks()
    qb, kb, vb, zb = C_GQ // G_DK, C_GK // G_DK, C_GV // G_DV, C_GZ // G_DV
    return pl.pallas_call(
        _gdn_kernel,
        out_shape=jax.ShapeDtypeStruct((bsz, S_ALL, BRANCH_W), BF16),
        grid=(bsz, G_HEADS),
        in_specs=[pl.BlockSpec((1, S_ALL, G_DK), lambda b, h: (b, 0, qb + h)),
                  pl.BlockSpec((1, S_ALL, G_DK), lambda b, h: (b, 0, kb + h)),
                  pl.BlockSpec((1, S_ALL, G_DV), lambda b, h: (b, 0, vb + h)),
                  pl.BlockSpec((1, S_ALL, G_DV), lambda b, h: (b, 0, zb + h)),
                  pl.BlockSpec((1, 1, N_CHUNKS, GATE_ROWS, CHUNK), lambda b, h: (b, h, 0, 0, 0)),
                  pl.BlockSpec((1, 2, GATE_ROWS, CHUNK), lambda b, h: (h, 0, 0, 0)),
                  pl.BlockSpec((1, 8, 3 * G_DK), lambda b, h: (h, 0, 0)),
                  pl.BlockSpec((1, G_DV), lambda b, h: (0, 0)),
                  pl.BlockSpec((CHUNK, 3 * CHUNK), lambda b, h: (0, 0)),
                  pl.BlockSpec((CHUNK, CHUNK), lambda b, h: (0, 0)),
                  pl.BlockSpec(masks.shape, lambda b, h: (0, 0, 0))],
        out_specs=pl.BlockSpec((1, S_ALL, G_DV), lambda b, h: (b, 0, h)),
        scratch_shapes=[pltpu.VMEM((S_ALL, G_DK), BF16), pltpu.VMEM((S_ALL, G_DK), BF16),
                        pltpu.VMEM((S_ALL, G_DV), BF16),
                        pltpu.VMEM((2, S_ALL, G_DV), F32),
                        pltpu.VMEM((2, 2 * S_ALL, G_DK), BF16),
                        pltpu.VMEM((2, S_ALL, 2 * CHUNK), BF16),
                        pltpu.VMEM((2, S_ALL, G_DK), BF16),
                        pltpu.VMEM((2, N_CHUNKS, 8, G_DV), F32),
                        pltpu.VMEM((2, S_ALL, G_DV), F32)],
        compiler_params=_cparams(("arbitrary", "arbitrary")),
        name="gdn_mixer",
    )(big, big, big, big, gates, gp, cw, norm_g.reshape(1, G_DV), tri, eye, masks)


GRID_H = SEQ // GRID_W
LRU_BLK = 256


def _shift_rows(a, s):
    n = a.shape[0]
    r = lax.broadcasted_iota(jnp.int32, a.shape, 0)
    return jnp.where((r >= s) & (r < n + s), pltpu.roll(a, s % n, 0), 0.0)


def _gelu(v):
    return 0.5 * v * (1.0 + jnp.tanh(0.7978845608028654 * (v + 0.044715 * v * v * v)))


def _lru_kernel(x_ref, y_ref, cw_ref, cb_ref, gw_ref, gb_ref, lam_ref, o_ref,
                xb_s, a_s, u_s, h_s, p_s, end_s, cin_s):
    cw = cw_ref[...]
    xf = x_ref[0].astype(F32)
    xc = xf[:CTX_LEN]
    lat = lambda r0, r1: xf[CTX_LEN + r0 * GRID_W:CTX_LEN + r1 * GRID_W]
    prev1 = jnp.concatenate([_shift_rows(lat(GRID_H - 1, GRID_H), 1), lat(0, GRID_H - 1)], axis=0)
    prev2 = jnp.concatenate([_shift_rows(lat(GRID_H - 2, GRID_H - 1), 1), _shift_rows(lat(GRID_H - 1, GRID_H), 1),
                             lat(0, GRID_H - 2)], axis=0)
    next1 = jnp.concatenate([lat(1, GRID_H), _shift_rows(lat(0, 1), -1)], axis=0)
    taps_c = (_shift_rows(xc, 2), _shift_rows(xc, 1), xc, _shift_rows(xc, -1))
    taps_l = (prev2, prev1, lat(0, GRID_H), next1)
    xb_s[:CTX_LEN, :] = sum(t * cw[j:j + 1, :] for j, t in enumerate(taps_c)) + cb_ref[...]
    xb_s[CTX_LEN:, :] = sum(t * cw[j:j + 1, :] for j, t in enumerate(taps_l)) + cb_ref[...]

    neg_c_sp = -LRU_C * _softplus(-lam_ref[...])

    def gates(i, carry):
        rows = pl.ds(pl.multiple_of(i * LRU_BLK, LRU_BLK), LRU_BLK)
        xb = xb_s[rows, :]
        pre = jnp.dot(xb.astype(BF16), gw_ref[0], preferred_element_type=F32) + gb_ref[0]
        for d in range(2):
            r = _sigmoid(pre[:, (2 * d) * LANE:(2 * d + 1) * LANE])
            g_in = _sigmoid(pre[:, (2 * d + 1) * LANE:(2 * d + 2) * LANE])
            log_a = neg_c_sp[d:d + 1, :] * r
            a = jnp.exp(log_a)
            a_s[d, rows, :] = a
            u_s[d, rows, :] = jnp.sqrt(jnp.maximum(1.0 - a * a, 0.0)) * g_in * xb
        return carry

    lax.fori_loop(0, S_ALL // LRU_BLK, gates, 0)

    def ctx_step(t, carry):
        hf, hb = carry
        tb = CTX_LEN - 1 - t
        hf = a_s[0, pl.ds(t, 1), :] * hf + u_s[0, pl.ds(t, 1), :]
        hb = a_s[1, pl.ds(tb, 1), :] * hb + u_s[1, pl.ds(tb, 1), :]
        h_s[0, pl.ds(t, 1), :] = hf
        h_s[1, pl.ds(tb, 1), :] = hb
        return hf, hb

    zero_row = jnp.zeros((1, LANE), F32)
    hf0, hb0 = lax.fori_loop(0, CTX_LEN, ctx_step, (zero_row, zero_row))

    def col_step(i, carry):
        hf, pf, hb, pb = carry
        rf = pl.ds(pl.multiple_of(CTX_LEN + i * GRID_W, GRID_W), GRID_W)
        rb = pl.ds(pl.multiple_of(CTX_LEN + (GRID_H - 1 - i) * GRID_W, GRID_W), GRID_W)
        af, ab = a_s[0, rf, :], a_s[1, rb, :]
        hf = af * hf + u_s[0, rf, :]
        pf = af * pf
        hb = ab * hb + u_s[1, rb, :]
        pb = ab * pb
        h_s[0, rf, :] = hf
        p_s[0, pl.ds(pl.multiple_of(i * GRID_W, GRID_W), GRID_W), :] = pf
        h_s[1, rb, :] = hb
        p_s[1, pl.ds(pl.multiple_of((GRID_H - 1 - i) * GRID_W, GRID_W), GRID_W), :] = pb
        return hf, pf, hb, pb

    zeros = jnp.zeros((GRID_W, LANE), F32)
    ones = jnp.ones((GRID_W, LANE), F32)
    hf, pf, hb, pb = lax.fori_loop(0, GRID_H, col_step, (zeros, ones, zeros, ones))
    end_s[0], end_s[1], end_s[2], end_s[3] = hf, pf, hb, pb

    def chain_step(w, carry):
        cf, cb = carry
        wb = GRID_W - 1 - w
        cin_s[0, pl.ds(w, 1), :] = cf
        cin_s[1, pl.ds(wb, 1), :] = cb
        cf = end_s[0, pl.ds(w, 1), :] + end_s[1, pl.ds(w, 1), :] * cf
        cb = end_s[2, pl.ds(wb, 1), :] + end_s[3, pl.ds(wb, 1), :] * cb
        return cf, cb

    lax.fori_loop(0, GRID_W, chain_step, (hf0, hb0))

    o_ref[0, :CTX_LEN, :] = ((h_s[0, :CTX_LEN, :] + h_s[1, :CTX_LEN, :])
                             * _gelu(y_ref[0, :CTX_LEN, :].astype(F32))).astype(o_ref.dtype)

    def out_step(i, carry):
        rows = pl.ds(pl.multiple_of(CTX_LEN + i * GRID_W, GRID_W), GRID_W)
        prow = pl.ds(pl.multiple_of(i * GRID_W, GRID_W), GRID_W)
        h = (h_s[0, rows, :] + p_s[0, prow, :] * cin_s[0] + h_s[1, rows, :] + p_s[1, prow, :] * cin_s[1])
        o_ref[0, rows, :] = (h * _gelu(y_ref[0, rows, :].astype(F32))).astype(o_ref.dtype)
        return carry

    lax.fori_loop(0, GRID_H, out_step, 0)


def lru_mixer(big, conv_w, conv_b, gate_w, gate_b, lam):
    bsz = big.shape[0]
    gw = gate_w.transpose(2, 3, 0, 1, 4).reshape(R_BLOCKS, R_BDIM, 4 * R_BDIM).astype(BF16)
    gb = gate_b.reshape(2, 2, R_BLOCKS, R_BDIM).transpose(2, 0, 1, 3).reshape(R_BLOCKS, 1, 4 * R_BDIM)
    cw = jnp.pad(conv_w, ((0, 8 - CONV_W), (0, 0)))
    xb, yb = C_RX // R_BDIM, C_RY // R_BDIM
    return pl.pallas_call(
        _lru_kernel,
        out_shape=jax.ShapeDtypeStruct((bsz, S_ALL, BRANCH_W), BF16),
        grid=(bsz, R_BLOCKS),
        in_specs=[pl.BlockSpec((1, S_ALL, R_BDIM), lambda b, n: (b, 0, xb + n)),
                  pl.BlockSpec((1, S_ALL, R_BDIM), lambda b, n: (b, 0, yb + n)),
                  pl.BlockSpec((8, R_BDIM), lambda b, n: (0, n)),
                  pl.BlockSpec((1, R_BDIM), lambda b, n: (0, n)),
                  pl.BlockSpec((1, R_BDIM, 4 * R_BDIM), lambda b, n: (n, 0, 0)),
                  pl.BlockSpec((1, 1, 4 * R_BDIM), lambda b, n: (n, 0, 0)),
                  pl.BlockSpec((2, R_BDIM), lambda b, n: (0, n))],
        out_specs=pl.BlockSpec((1, S_ALL, R_BDIM), lambda b, n: (b, 0, n)),
        scratch_shapes=[pltpu.VMEM((S_ALL, R_BDIM), F32), pltpu.VMEM((2, S_ALL, R_BDIM), F32),
                        pltpu.VMEM((2, S_ALL, R_BDIM), F32), pltpu.VMEM((2, S_ALL, R_BDIM), F32),
                        pltpu.VMEM((2, SEQ, R_BDIM), F32), pltpu.VMEM((4, GRID_W, R_BDIM), F32),
                        pltpu.VMEM((2, GRID_W, R_BDIM), F32)],
        compiler_params=_cparams(("arbitrary", "arbitrary")),
        name="lru_mixer",
    )(big, big, cw, conv_b.reshape(1, BRANCH_W), gw, gb, lam)


def _pack_in_weights(w, b):
    segs_w = [w[:, IN_OFFS[i]:IN_OFFS[i + 1]] for i in range(len(IN_COLS))]
    segs_b = [b[IN_OFFS[i]:IN_OFFS[i + 1]] for i in range(len(IN_COLS))]
    w_big = jnp.concatenate([segs_w[i] for i in BIG_SEGS], axis=1).astype(BF16)
    b_big = jnp.concatenate([segs_b[i] for i in BIG_SEGS]).reshape(1, N_BIG)
    pad = N_SMALL - sum(IN_COLS[i] for i in SMALL_SEGS)
    w_small = jnp.pad(jnp.concatenate([segs_w[i] for i in SMALL_SEGS], axis=1), ((0, 0), (0, pad))).astype(BF16)
    b_small = jnp.pad(jnp.concatenate([segs_b[i] for i in SMALL_SEGS]), (0, pad)).reshape(1, N_SMALL)
    return w_big, b_big, w_small, b_small


def kernel(x, c, ctx, c_ctx, mod_w, mod_b, norm1_g, norm2_g, final_g, w_in, b_in, m_fbias, m_norm_g,
           g_conv, g_a_log, g_dt_bias, g_norm_g, r_conv, r_conv_b, r_gate_w, r_gate_b, r_lambda,
           w_branch, w_out, ffn_w1, ffn_w3, ffn_w2, router_w, router_b, moe_w1, moe_w3, moe_w2):
    bsz = x.shape[0]
    xs = jnp.concatenate([ctx, x], axis=1)
    src = jnp.concatenate([c, c_ctx[None]], axis=0)
    mods = modulation(src, mod_w, mod_b)
    for layer in range(DEPTH):
        mod = mods[layer]
        w_big, b_big, w_small, b_small = _pack_in_weights(w_in[layer], b_in[layer])
        big, small = in_projection(xs, norm1_g[layer], mod, w_big, b_big, w_small, b_small)
        ym = mlstm_mixer(big, small, m_fbias[layer], m_norm_g[layer])
        yg = gdn_mixer(big, small, g_conv[layer], g_a_log[layer], g_dt_bias[layer], g_norm_g[layer])
        yr = lru_mixer(big, r_conv[layer], r_conv_b[layer], r_gate_w[layer], r_gate_b[layer], r_lambda[layer])
        xs = merge_branches(ym, yg, yr, big, xs, mod, w_branch[layer].astype(BF16), w_out[layer].astype(BF16))
        j = layer // 2
        if layer % 2 == 0:
            xs = dense_ffn(xs, norm2_g[layer], mod, ffn_w1[j].astype(BF16), ffn_w3[j].astype(BF16),
                           ffn_w2[j].astype(BF16))
        else:
            gt2 = jnp.concatenate([jnp.broadcast_to(mod[bsz:, None, 5 * D_MODEL:], (bsz, CTX_LEN, D_MODEL)),
                                   jnp.broadcast_to(mod[:bsz, None, 5 * D_MODEL:], (bsz, SEQ, D_MODEL))], axis=1)
            xs = moe_ffn(xs, norm2_g[layer], mod, gt2, router_w[j], router_b[j], moe_w1[j].astype(BF16),
                         moe_w3[j].astype(BF16), moe_w2[j].astype(BF16))
    return final_norm(xs, final_g)
```

```python
import jax
import jax.numpy as jnp
import numpy as np
from jax import lax
from jax.experimental import pallas as pl
from jax.experimental.pallas import tpu as pltpu

D_MODEL = 1024
SEQ = 2048
DEPTH = 4
GRID_W = 64
CTX_LEN = 256
S_ALL = CTX_LEN + SEQ
BRANCH_W = D_MODEL
N_BRANCH = 3
CHUNK = 64
CONV_W = 4
M_HEADS = 4
M_DQK = 128
M_DV = BRANCH_W // M_HEADS
G_HEADS = 8
G_DK = 128
G_DV = BRANCH_W // G_HEADS
R_BLOCKS = 8
R_BDIM = BRANCH_W // R_BLOCKS
LRU_C = 8.0
D_FF = 2816
N_EXPERTS = 8
TOP_K = 2
MOE_BLOCK = 512
EPS = 1e-6
IN_COLS = (M_HEADS * M_DQK, M_HEADS * M_DQK, BRANCH_W, BRANCH_W, 4 * M_HEADS,
           2 * G_HEADS * G_DK + BRANCH_W, BRANCH_W, 2 * G_HEADS, 2 * G_HEADS,
           BRANCH_W, BRANCH_W, N_BRANCH * D_MODEL)
IN_OFFS = tuple(int(v) for v in np.cumsum((0,) + IN_COLS))
SMALL_SEGS = (4, 7, 8)
BIG_SEGS = tuple(i for i in range(len(IN_COLS)) if i not in SMALL_SEGS)
N_BIG = sum(IN_COLS[i] for i in BIG_SEGS)
N_SMALL = 128
C_MQ, C_MK, C_MV, C_MO = 0, 512, 1024, 2048
C_GQ, C_GK, C_GV, C_GZ = 3072, 4096, 5120, 6144
C_RX, C_RY, C_MG = 7168, 8192, 9216

LANE = 128
VMEM_LIMIT = 56 * 1024 * 1024
F32 = jnp.float32
BF16 = jnp.bfloat16


def _cparams(sem):
    return pltpu.CompilerParams(dimension_semantics=sem, vmem_limit_bytes=VMEM_LIMIT)


def _sigmoid(v):
    return 0.5 * jnp.tanh(0.5 * v) + 0.5


def _ada_norm(x, g, mod_l, mod_c, first_row, which):
    y = x * lax.rsqrt(jnp.mean(x * x, axis=-1, keepdims=True) + EPS) * g
    sh_l, sc_l = _mod_slice(mod_l, 3 * which), _mod_slice(mod_l, 3 * which + 1)
    sh_c, sc_c = _mod_slice(mod_c, 3 * which), _mod_slice(mod_c, 3 * which + 1)
    is_ctx = _is_ctx(x.shape[0], first_row)
    return y * (1.0 + jnp.where(is_ctx, sc_c, sc_l)) + jnp.where(is_ctx, sh_c, sh_l)


def _mod_slice(mod, k):
    return mod[:, k * D_MODEL:(k + 1) * D_MODEL]


def _is_ctx(rows, first_row):
    return (lax.broadcasted_iota(jnp.int32, (rows, 1), 0) + first_row) < CTX_LEN


def _mod_kernel(src_ref, w_ref, b_ref, o_ref):
    s = src_ref[...]
    s = (s * _sigmoid(s)).astype(BF16)
    o_ref[0] = jnp.dot(s, w_ref[0].astype(BF16), preferred_element_type=F32) + b_ref[0]


def modulation(src, mod_w, mod_b):
    rows = src.shape[0]
    tn = 1536
    return pl.pallas_call(
        _mod_kernel,
        out_shape=jax.ShapeDtypeStruct((DEPTH, rows, 6 * D_MODEL), F32),
        grid=(DEPTH, 6 * D_MODEL // tn),
        in_specs=[pl.BlockSpec((rows, D_MODEL), lambda l, n: (0, 0)),
                  pl.BlockSpec((1, D_MODEL, tn), lambda l, n: (l, 0, n)),
                  pl.BlockSpec((1, 1, tn), lambda l, n: (l, 0, n))],
        out_specs=pl.BlockSpec((1, rows, tn), lambda l, n: (l, 0, n)),
        compiler_params=_cparams(("arbitrary", "arbitrary")),
        name="modulation",
    )(src, mod_w, mod_b.reshape(DEPTH, 1, 6 * D_MODEL))


IN_TM = 1152
IN_TN = 1024


def _in_kernel(x_ref, g_ref, ml_ref, mc_ref, w_ref, b_ref, ws_ref, bs_ref, o_ref, os_ref, h_ref):
    n = pl.program_id(2)

    @pl.when(n == 0)
    def _():
        h = _ada_norm(x_ref[0], g_ref[...], ml_ref[0], mc_ref[0], pl.program_id(1) * IN_TM, 0)
        h_ref[...] = h.astype(BF16)
        os_ref[0] = jnp.dot(h_ref[...], ws_ref[...], preferred_element_type=F32) + bs_ref[...]

    o_ref[0] = (jnp.dot(h_ref[...], w_ref[...], preferred_element_type=F32) + b_ref[...]).astype(o_ref.dtype)


def in_projection(xs, g, mod, w_big, b_big, w_small, b_small):
    bsz = xs.shape[0]
    mod3 = mod.reshape(bsz + 1, 1, 6 * D_MODEL)
    return pl.pallas_call(
        _in_kernel,
        out_shape=(jax.ShapeDtypeStruct((bsz, S_ALL, N_BIG), BF16),
                   jax.ShapeDtypeStruct((bsz, S_ALL, N_SMALL), F32)),
        grid=(bsz, S_ALL // IN_TM, N_BIG // IN_TN),
        in_specs=[pl.BlockSpec((1, IN_TM, D_MODEL), lambda b, s, n: (b, s, 0)),
                  pl.BlockSpec((1, D_MODEL), lambda b, s, n: (0, 0)),
                  pl.BlockSpec((1, 1, 6 * D_MODEL), lambda b, s, n: (b, 0, 0)),
                  pl.BlockSpec((1, 1, 6 * D_MODEL), lambda b, s, n: (bsz, 0, 0)),
                  pl.BlockSpec((D_MODEL, IN_TN), lambda b, s, n: (0, n)),
                  pl.BlockSpec((1, IN_TN), lambda b, s, n: (0, n)),
                  pl.BlockSpec((D_MODEL, N_SMALL), lambda b, s, n: (0, 0)),
                  pl.BlockSpec((1, N_SMALL), lambda b, s, n: (0, 0))],
        out_specs=(pl.BlockSpec((1, IN_TM, IN_TN), lambda b, s, n: (b, s, n)),
                   pl.BlockSpec((1, IN_TM, N_SMALL), lambda b, s, n: (b, s, 0))),
        scratch_shapes=[pltpu.VMEM((IN_TM, D_MODEL), BF16)],
        compiler_params=_cparams(("arbitrary", "arbitrary", "arbitrary")),
        name="in_projection",
    )(xs, g.reshape(1, D_MODEL), mod3, mod3, w_big, b_big, w_small, b_small)


MERGE_TM = 768


def _merge_kernel(ym_ref, yg_ref, yr_ref, g0_ref, g1_ref, g2_ref, x_ref, ml_ref, mc_ref, wb_ref, wo_ref, o_ref):
    acc = None
    for n, (y_ref, gp_ref) in enumerate(((ym_ref, g0_ref), (yg_ref, g1_ref), (yr_ref, g2_ref))):
        p = jnp.dot(y_ref[0].astype(BF16), wb_ref[n], preferred_element_type=F32)
        p = p * _sigmoid(gp_ref[0].astype(F32))
        acc = p if acc is None else acc + p
    out = jnp.dot(acc.astype(BF16), wo_ref[...], preferred_element_type=F32)
    is_ctx = _is_ctx(MERGE_TM, pl.program_id(1) * MERGE_TM)
    gt = jnp.where(is_ctx, _mod_slice(mc_ref[0], 2), _mod_slice(ml_ref[0], 2))
    o_ref[0] = x_ref[0] + gt * out


def merge_branches(ym, yg, yr, big, xs, mod, w_branch, w_out):
    bsz = xs.shape[0]
    mod3 = mod.reshape(bsz + 1, 1, 6 * D_MODEL)
    row = lambda b, s: (b, s, 0)
    gate_blk = C_MG // D_MODEL
    const = pl.Buffered(1)
    return pl.pallas_call(
        _merge_kernel,
        out_shape=jax.ShapeDtypeStruct(xs.shape, F32),
        grid=(bsz, S_ALL // MERGE_TM),
        in_specs=[pl.BlockSpec((1, MERGE_TM, D_MODEL), row),
                  pl.BlockSpec((1, MERGE_TM, D_MODEL), row),
                  pl.BlockSpec((1, MERGE_TM, D_MODEL), row),
                  pl.BlockSpec((1, MERGE_TM, D_MODEL), lambda b, s: (b, s, gate_blk)),
                  pl.BlockSpec((1, MERGE_TM, D_MODEL), lambda b, s: (b, s, gate_blk + 1)),
                  pl.BlockSpec((1, MERGE_TM, D_MODEL), lambda b, s: (b, s, gate_blk + 2)),
                  pl.BlockSpec((1, MERGE_TM, D_MODEL), row),
                  pl.BlockSpec((1, 1, 6 * D_MODEL), lambda b, s: (b, 0, 0)),
                  pl.BlockSpec((1, 1, 6 * D_MODEL), lambda b, s: (bsz, 0, 0)),
                  pl.BlockSpec((N_BRANCH, BRANCH_W, D_MODEL), lambda b, s: (0, 0, 0), pipeline_mode=const),
                  pl.BlockSpec((D_MODEL, D_MODEL), lambda b, s: (0, 0), pipeline_mode=const)],
        out_specs=pl.BlockSpec((1, MERGE_TM, D_MODEL), row),
        compiler_params=_cparams(("arbitrary", "arbitrary")),
        name="merge_branches",
    )(ym, yg, yr, big, big, big, xs, mod3, mod3, w_branch, w_out)


FFN_TM = 1152
FFN_TF = 256


def _swiglu_chunk(h, w1, w3, w2):
    a = jnp.dot(h, w1, preferred_element_type=F32)
    b = jnp.dot(h, w3, preferred_element_type=F32)
    return jnp.dot((a * _sigmoid(a) * b).astype(BF16), w2, preferred_element_type=F32)


def _ffn_kernel(x_ref, g_ref, ml_ref, mc_ref, w1_ref, w3_ref, w2_ref, o_ref, h_ref, acc_ref):
    f = pl.program_id(2)

    @pl.when(f == 0)
    def _():
        h = _ada_norm(x_ref[0], g_ref[...], ml_ref[0], mc_ref[0], pl.program_id(1) * FFN_TM, 1)
        h_ref[...] = h.astype(BF16)
        acc_ref[...] = jnp.zeros_like(acc_ref)

    acc_ref[...] += _swiglu_chunk(h_ref[...], w1_ref[...], w3_ref[...], w2_ref[...])

    @pl.when(f == pl.num_programs(2) - 1)
    def _():
        is_ctx = _is_ctx(FFN_TM, pl.program_id(1) * FFN_TM)
        gt = jnp.where(is_ctx, _mod_slice(mc_ref[0], 5), _mod_slice(ml_ref[0], 5))
        o_ref[0] = x_ref[0] + gt * acc_ref[...]


def dense_ffn(xs, g, mod, w1, w3, w2):
    bsz = xs.shape[0]
    mod3 = mod.reshape(bsz + 1, 1, 6 * D_MODEL)
    return pl.pallas_call(
        _ffn_kernel,
        out_shape=jax.ShapeDtypeStruct(xs.shape, F32),
        grid=(bsz, S_ALL // FFN_TM, D_FF // FFN_TF),
        in_specs=[pl.BlockSpec((1, FFN_TM, D_MODEL), lambda b, s, f: (b, s, 0)),
                  pl.BlockSpec((1, D_MODEL), lambda b, s, f: (0, 0)),
                  pl.BlockSpec((1, 1, 6 * D_MODEL), lambda b, s, f: (b, 0, 0)),
                  pl.BlockSpec((1, 1, 6 * D_MODEL), lambda b, s, f: (bsz, 0, 0)),
                  pl.BlockSpec((D_MODEL, FFN_TF), lambda b, s, f: (0, f)),
                  pl.BlockSpec((D_MODEL, FFN_TF), lambda b, s, f: (0, f)),
                  pl.BlockSpec((FFN_TF, D_MODEL), lambda b, s, f: (f, 0))],
        out_specs=pl.BlockSpec((1, FFN_TM, D_MODEL), lambda b, s, f: (b, s, 0)),
        scratch_shapes=[pltpu.VMEM((FFN_TM, D_MODEL), BF16), pltpu.VMEM((FFN_TM, D_MODEL), F32)],
        compiler_params=_cparams(("arbitrary", "arbitrary", "arbitrary")),
        name="dense_ffn",
    )(xs, g.reshape(1, D_MODEL), mod3, mod3, w1, w3, w2)


NORM_TM = 768


def _norm_router_kernel(x_ref, g_ref, ml_ref, mc_ref, rw_ref, rb_ref, h_ref, lg_ref):
    h = _ada_norm(x_ref[0], g_ref[...], ml_ref[0], mc_ref[0], pl.program_id(1) * NORM_TM, 1)
    h_ref[0] = h.astype(h_ref.dtype)
    lg_ref[0] = jnp.dot(h, rw_ref[...], preferred_element_type=F32, precision=lax.Precision.HIGHEST) + rb_ref[...]


def norm_router(xs, g, mod, router_w, router_b):
    bsz = xs.shape[0]
    mod3 = mod.reshape(bsz + 1, 1, 6 * D_MODEL)
    rw = jnp.zeros((D_MODEL, LANE), F32).at[:, :N_EXPERTS].set(router_w)
    rb = jnp.zeros((1, LANE), F32).at[0, :N_EXPERTS].set(router_b)
    return pl.pallas_call(
        _norm_router_kernel,
        out_shape=(jax.ShapeDtypeStruct(xs.shape, BF16), jax.ShapeDtypeStruct((bsz, S_ALL, LANE), F32)),
        grid=(bsz, S_ALL // NORM_TM),
        in_specs=[pl.BlockSpec((1, NORM_TM, D_MODEL), lambda b, s: (b, s, 0)),
                  pl.BlockSpec((1, D_MODEL), lambda b, s: (0, 0)),
                  pl.BlockSpec((1, 1, 6 * D_MODEL), lambda b, s: (b, 0, 0)),
                  pl.BlockSpec((1, 1, 6 * D_MODEL), lambda b, s: (bsz, 0, 0)),
                  pl.BlockSpec((D_MODEL, LANE), lambda b, s: (0, 0)),
                  pl.BlockSpec((1, LANE), lambda b, s: (0, 0))],
        out_specs=(pl.BlockSpec((1, NORM_TM, D_MODEL), lambda b, s: (b, s, 0)),
                   pl.BlockSpec((1, NORM_TM, LANE), lambda b, s: (b, s, 0))),
        compiler_params=_cparams(("arbitrary", "arbitrary")),
        name="norm_router",
    )(xs, g.reshape(1, D_MODEL), mod3, mod3, rw, rb)


def _expert_kernel(be_ref, x_ref, w1_ref, w3_ref, w2_ref, o_ref):
    del be_ref
    x = x_ref[...]
    for f in range(D_FF // FFN_TF):
        cols = slice(f * FFN_TF, (f + 1) * FFN_TF)
        y = _swiglu_chunk(x, w1_ref[0, :, cols], w3_ref[0, :, cols], w2_ref[0, cols, :])
        if f == 0:
            o_ref[...] = y
        else:
            o_ref[...] += y


def grouped_experts(xb, block_e, w1, w3, w2):
    n_rows = xb.shape[0]
    const = pl.Buffered(1)
    grid_spec = pltpu.PrefetchScalarGridSpec(
        num_scalar_prefetch=1,
        grid=(n_rows // MOE_BLOCK,),
        in_specs=[pl.BlockSpec((MOE_BLOCK, D_MODEL), lambda i, be: (i, 0)),
                  pl.BlockSpec((1, D_MODEL, D_FF), lambda i, be: (be[i], 0, 0), pipeline_mode=const),
                  pl.BlockSpec((1, D_MODEL, D_FF), lambda i, be: (be[i], 0, 0), pipeline_mode=const),
                  pl.BlockSpec((1, D_FF, D_MODEL), lambda i, be: (be[i], 0, 0), pipeline_mode=const)],
        out_specs=pl.BlockSpec((MOE_BLOCK, D_MODEL), lambda i, be: (i, 0)),
    )
    return pl.pallas_call(
        _expert_kernel,
        out_shape=jax.ShapeDtypeStruct((n_rows, D_MODEL), F32),
        grid_spec=grid_spec,
        compiler_params=_cparams(("arbitrary",)),
        name="grouped_experts",
    )(block_e, xb, w1, w3, w2)


def moe_ffn(xs, g, mod, gt2, router_w, router_b, w1, w3, w2):
    bsz = xs.shape[0]
    h, logits = norm_router(xs, g, mod, router_w, router_b)
    h = h.reshape(-1, D_MODEL)
    n_tok = h.shape[0]
    top_logit, top_e = lax.top_k(logits.reshape(n_tok, LANE)[:, :N_EXPERTS], TOP_K)
    gate = jax.nn.softmax(top_logit, axis=-1)
    n_assign = n_tok * TOP_K
    flat_e = top_e.reshape(-1)
    flat_tok = jnp.repeat(jnp.arange(n_tok, dtype=jnp.int32), TOP_K)
    onehot = (flat_e[:, None] == jnp.arange(N_EXPERTS, dtype=flat_e.dtype)[None, :]).astype(jnp.int32)
    seen = jnp.cumsum(onehot, axis=0)
    padded = (seen[-1] + MOE_BLOCK - 1) // MOE_BLOCK * MOE_BLOCK
    p_end = jnp.cumsum(padded)
    slot = jnp.sum(onehot * ((p_end - padded)[None, :] + seen - 1), axis=1)
    n_blocks = -(-n_assign // MOE_BLOCK) + N_EXPERTS
    n_rows = n_blocks * MOE_BLOCK
    row_tok = jnp.zeros((n_rows,), jnp.int32).at[slot].set(flat_tok)
    slot = slot.reshape(n_tok, TOP_K)
    block_e = jnp.minimum(jnp.searchsorted(p_end, jnp.arange(n_blocks) * MOE_BLOCK, side='right'),
                          N_EXPERTS - 1).astype(jnp.int32)
    yb = grouped_experts(h[row_tok], block_e, w1, w3, w2)
    y = sum(yb[slot[:, k]] * gate[:, k:k + 1] for k in range(TOP_K))
    return xs + gt2 * y.reshape(bsz, S_ALL, D_MODEL)


FINAL_TM = 256


def _final_kernel(x_ref, g_ref, o_ref):
    x = x_ref[0]
    o_ref[0] = x * lax.rsqrt(jnp.mean(x * x, axis=-1, keepdims=True) + EPS) * g_ref[...]


def final_norm(xs, g):
    bsz = xs.shape[0]
    skip = CTX_LEN // FINAL_TM
    return pl.pallas_call(
        _final_kernel,
        out_shape=jax.ShapeDtypeStruct((bsz, SEQ, D_MODEL), F32),
        grid=(bsz, SEQ // FINAL_TM),
        in_specs=[pl.BlockSpec((1, FINAL_TM, D_MODEL), lambda b, s: (b, s + skip, 0)),
                  pl.BlockSpec((1, D_MODEL), lambda b, s: (0, 0))],
        out_specs=pl.BlockSpec((1, FINAL_TM, D_MODEL), lambda b, s: (b, s, 0)),
        compiler_params=_cparams(("arbitrary", "arbitrary")),
        name="final_norm",
    )(xs, g.reshape(1, D_MODEL))


N_CHUNKS = S_ALL // CHUNK
CTX_CHUNKS = CTX_LEN // CHUNK
GATE_ROWS = 8
NEG_INF = float("-inf")


def _softplus(v):
    return jnp.maximum(v, 0.0) + jnp.log(1.0 + jnp.exp(-jnp.abs(v)))


def _scan_constants():
    s = np.arange(CHUNK)[:, None]
    t = np.arange(CHUNK)[None, :]
    tri = np.concatenate([(s <= t), (s >= t), np.ones((CHUNK, CHUNK), bool)], axis=1).astype(np.float32)
    return jnp.asarray(tri), jnp.asarray(np.eye(CHUNK, dtype=np.float32))


def _rev_chunk(j):
    return jnp.where(j < CTX_CHUNKS, CTX_CHUNKS - 1 - j, N_CHUNKS + CTX_CHUNKS - 1 - j)


def _head_gate_rows(small, c0, heads, kind_major):
    bsz = small.shape[0]
    a = small[:, :, c0:c0 + 4 * heads].reshape(bsz, N_CHUNKS, CHUNK, 2, 2, heads)
    a = a.transpose((0, 5, 1, 4, 3, 2) if kind_major else (0, 5, 1, 3, 4, 2))
    a = a.reshape(bsz, heads, N_CHUNKS, 4, CHUNK)
    return jnp.pad(a, ((0, 0), (0, 0), (0, 0), (0, GATE_ROWS - 4), (0, 0)))


def _tn_dot(a, b):
    return lax.dot_general(a, b, (((0,), (0,)), ((), ())), preferred_element_type=F32)


def _nt_dot(a, b, precision=None):
    return lax.dot_general(a, b, (((1,), (1,)), ((), ())), preferred_element_type=F32, precision=precision)


def _dir_mask(d):
    t = lax.broadcasted_iota(jnp.int32, (CHUNK, CHUNK), 0)
    s = lax.broadcasted_iota(jnp.int32, (CHUNK, CHUNK), 1)
    return (s <= t, s < t) if d == 0 else (s >= t, s > t)


def _split_dot(a, ones, dot_fn):
    b = ones.astype(BF16)
    hi = a.astype(BF16)
    lo = (a - hi.astype(F32)).astype(BF16)
    return dot_fn(hi, b, preferred_element_type=F32) + dot_fn(lo, b, preferred_element_type=F32)


def _gate_forms(vals, tri, eye):
    cum = _split_dot(vals, tri, jnp.dot)
    kind = lax.broadcasted_iota(jnp.int32, vals.shape, 0) & (GATE_ROWS - 1)
    rows = jnp.where(kind == 1, cum[:, :CHUNK], jnp.where(kind == 3, cum[:, CHUNK:2 * CHUNK], vals))
    cols = _split_dot(rows, eye, lambda a, b, **kw: _nt_dot(b, a))
    return rows, cols, cum[:, 2 * CHUNK:2 * CHUNK + 1]


def _x_row(rows, d, g=0):
    return rows[GATE_ROWS * g + 2 * d:GATE_ROWS * g + 2 * d + 1]


def _run_row(rows, d, g=0):
    return rows[GATE_ROWS * g + 2 * d + 1:GATE_ROWS * g + 2 * d + 2]


def _x_col(cols, d, g=0):
    return cols[:, GATE_ROWS * g + 2 * d:GATE_ROWS * g + 2 * d + 1]


def _run_col(cols, d, g=0):
    return cols[:, GATE_ROWS * g + 2 * d + 1:GATE_ROWS * g + 2 * d + 2]


PAIRS_PER_STEP = 3
OUT_PAIRS_PER_STEP = 1


def _mlstm_kernel(q_ref, k_ref, v_ref, o_ref, g_ref, fb_ref, ng_ref, tri_ref, eye_ref, y_ref,
                  c_s, st_s, gr_s, gc_s):
    tri, eye = tri_ref[...], eye_ref[...]
    fbias = fb_ref[0]
    scale = M_DQK ** -0.5
    kind = lax.broadcasted_iota(jnp.int32, (2 * GATE_ROWS, CHUNK), 0) & 1
    fbias2 = jnp.concatenate([fbias, fbias], axis=0)
    st_row = lax.broadcasted_iota(jnp.int32, (GATE_ROWS, M_DQK), 0)

    def pair_rows(c0):
        return pl.ds(pl.multiple_of(c0 * CHUNK, 2 * CHUNK), 2 * CHUNK)

    def state_rows(c):
        return pl.ds(pl.multiple_of(c * M_DQK, M_DQK), M_DQK)

    def pair_state_rows(c0):
        return pl.ds(pl.multiple_of(c0 * M_DQK, 2 * M_DQK), 2 * M_DQK)

    def local(it, carry):
        pairs = [it * PAIRS_PER_STEP + p for p in range(PAIRS_PER_STEP)]
        forms = []
        for g in pairs:
            raw = jnp.concatenate([g_ref[0, 0, 2 * g], g_ref[0, 0, 2 * g + 1]], axis=0)
            vals = jnp.where(kind == 1, -_softplus(-(raw + fbias2)), raw)
            forms.append(_gate_forms(vals, tri, eye))
        zero = jnp.zeros((CHUNK, 2 * M_DQK), F32)
        lhs_all = []
        for g, (rows, cols, totals) in zip(pairs, forms):
            gr_s[g] = rows
            gc_s[g] = cols
            kf2 = k_ref[0, pair_rows(2 * g), :].astype(F32)
            lhs = []
            for i in range(2):
                kf = kf2[i * CHUNK:(i + 1) * CHUNK]
                kws = []
                for d in range(2):
                    tot = totals[GATE_ROWS * i + 2 * d + 1:GATE_ROWS * i + 2 * d + 2]
                    m_loc = jnp.max(tot - _run_row(rows, d, i) + _x_row(rows, d, i), axis=-1, keepdims=True)
                    kw = kf * (jnp.exp(tot - _run_col(cols, d, i) + _x_col(cols, d, i) - m_loc) * scale)
                    kws.append(kw)
                    st_s[d, 2 * g + i] = jnp.where(st_row == 0, jnp.sum(kw, axis=0, keepdims=True),
                                                   jnp.where(st_row == 1, m_loc, tot))
                lhs.append(jnp.concatenate(kws + [zero] if i == 0 else [zero] + kws, axis=1))
            lhs_all.append(jnp.concatenate(lhs, axis=0).astype(BF16))
        for g, lhs in zip(pairs, lhs_all):
            c_loc = _tn_dot(lhs, v_ref[0, pair_rows(2 * g), :])
            for i in range(2):
                for d in range(2):
                    c_s[d, state_rows(2 * g + i), :] = c_loc[(2 * i + d) * M_DQK:(2 * i + d + 1) * M_DQK]
        return carry

    lax.fori_loop(0, N_CHUNKS // (2 * PAIRS_PER_STEP), local, 0)

    def scan(d):
        def body(j, carry):
            c_prev, n_prev, m_prev = carry
            c = j if d == 0 else _rev_chunk(j)
            st = st_s[d, c]
            n_loc, m_loc, tot = st[0:1], st[1:2, :1], st[2:3, :1]
            c_loc = c_s[d, state_rows(c), :]
            c_s[d, state_rows(c), :] = c_prev
            st_s[d, c] = jnp.where(st_row == 0, n_prev, m_prev)
            m_new = jnp.maximum(tot + m_prev, m_loc)
            s_prev = jnp.exp(tot + m_prev - m_new)
            s_loc = jnp.exp(m_loc - m_new)
            return s_prev * c_prev + s_loc * c_loc, s_prev * n_prev + s_loc * n_loc, m_new

        lax.fori_loop(0, N_CHUNKS, body, (jnp.zeros((M_DQK, M_DV), F32), jnp.zeros((1, M_DQK), F32),
                                          jnp.zeros((1, 1), F32)))

    scan(0)
    scan(1)

    def output(it, carry):
        pairs = [it * OUT_PAIRS_PER_STEP + p for p in range(OUT_PAIRS_PER_STEP)]
        qk_all = [_nt_dot(q_ref[0, pair_rows(2 * g), :], k_ref[0, pair_rows(2 * g), :]) * scale for g in pairs]
        z_intra = jnp.zeros((CHUNK, 2 * CHUNK), F32)
        z_inter = jnp.zeros((CHUNK, M_DQK), F32)
        operands = []
        for g, qk2 in zip(pairs, qk_all):
            c0 = 2 * g
            qf2, v2 = q_ref[0, pair_rows(c0), :].astype(F32), v_ref[0, pair_rows(c0), :]
            rows, cols = gr_s[g], gc_s[g]
            lhs_intra, lhs_inter, rhs_v = [], [], []
            for i in range(2):
                qk = qk2[i * CHUNK:(i + 1) * CHUNK, i * CHUNK:(i + 1) * CHUNK]
                qf = qf2[i * CHUNK:(i + 1) * CHUNK]
                intra, inter = [], []
                for d in range(2):
                    st = st_s[d, c0 + i]
                    b_c = _run_col(cols, d, i)
                    n0, m0 = st[0:1], st[1:2, :1]
                    incl, _ = _dir_mask(d)
                    d_log = jnp.where(incl, b_c - _run_row(rows, d, i) + _x_row(rows, d, i), NEG_INF)
                    m_inter = b_c + m0
                    m_comb = jnp.maximum(m_inter, jnp.max(d_log, axis=-1, keepdims=True))
                    s = qk * jnp.exp(d_log - m_comb)
                    e_inter = jnp.exp(m_inter - m_comb)
                    den = (jnp.sum(s, axis=-1, keepdims=True)
                           + e_inter * jnp.sum(qf * n0, axis=-1, keepdims=True))
                    inv = 1.0 / jnp.maximum(jnp.abs(den), jnp.exp(-m_comb))
                    intra.append(s * inv)
                    inter.append(qf * (e_inter * inv))
                pair = jnp.concatenate(intra, axis=1)
                lhs_intra.append(jnp.concatenate([pair, z_intra] if i == 0 else [z_intra, pair], axis=1))
                lhs_inter.append(jnp.concatenate([inter[0], z_inter, inter[1], z_inter] if i == 0
                                                 else [z_inter, inter[0], z_inter, inter[1]], axis=1))
                v = v2[i * CHUNK:(i + 1) * CHUNK]
                rhs_v += [v, v]
            c_in = jnp.concatenate([c_s[0, pair_state_rows(c0), :], c_s[1, pair_state_rows(c0), :]], axis=0)
            operands.append((jnp.concatenate(lhs_intra, axis=0).astype(BF16), jnp.concatenate(rhs_v, axis=0),
                             jnp.concatenate(lhs_inter, axis=0).astype(BF16), c_in.astype(BF16)))
        hs = [jnp.dot(li, rv, preferred_element_type=F32) + jnp.dot(le, ci, preferred_element_type=F32)
              for li, rv, le, ci in operands]
        for g, h in zip(pairs, hs):
            h = h * lax.rsqrt(jnp.mean(h * h, axis=-1, keepdims=True) + EPS) * ng_ref[...]
            y_ref[0, pair_rows(2 * g), :] = (h * _sigmoid(o_ref[0, pair_rows(2 * g), :].astype(F32))
                                             ).astype(y_ref.dtype)
        return carry

    lax.fori_loop(0, N_CHUNKS // (2 * OUT_PAIRS_PER_STEP), output, 0)


def mlstm_mixer(big, small, f_bias, norm_g):
    bsz = big.shape[0]
    gates = _head_gate_rows(small, 0, M_HEADS, False)
    fb = jnp.zeros((M_HEADS, GATE_ROWS, CHUNK), F32)
    fb = fb.at[:, 1].set(f_bias[0][:, None]).at[:, 3].set(f_bias[1][:, None])
    tri, eye = _scan_constants()
    qb, kb, vb, ob = C_MQ // M_DQK, C_MK // M_DQK, C_MV // M_DV, C_MO // M_DV
    return pl.pallas_call(
        _mlstm_kernel,
        out_shape=jax.ShapeDtypeStruct((bsz, S_ALL, BRANCH_W), BF16),
        grid=(bsz, M_HEADS),
        in_specs=[pl.BlockSpec((1, S_ALL, M_DQK), lambda b, h: (b, 0, qb + h)),
                  pl.BlockSpec((1, S_ALL, M_DQK), lambda b, h: (b, 0, kb + h)),
                  pl.BlockSpec((1, S_ALL, M_DV), lambda b, h: (b, 0, vb + h)),
                  pl.BlockSpec((1, S_ALL, M_DV), lambda b, h: (b, 0, ob + h)),
                  pl.BlockSpec((1, 1, N_CHUNKS, GATE_ROWS, CHUNK), lambda b, h: (b, h, 0, 0, 0)),
                  pl.BlockSpec((1, GATE_ROWS, CHUNK), lambda b, h: (h, 0, 0)),
                  pl.BlockSpec((1, M_DV), lambda b, h: (0, h)),
                  pl.BlockSpec((CHUNK, 3 * CHUNK), lambda b, h: (0, 0)),
                  pl.BlockSpec((CHUNK, CHUNK), lambda b, h: (0, 0))],
        out_specs=pl.BlockSpec((1, S_ALL, M_DV), lambda b, h: (b, 0, h)),
        scratch_shapes=[pltpu.VMEM((2, N_CHUNKS * M_DQK, M_DV), F32),
                        pltpu.VMEM((2, N_CHUNKS, GATE_ROWS, M_DQK), F32),
                        pltpu.VMEM((N_CHUNKS // 2, 2 * GATE_ROWS, CHUNK), F32),
                        pltpu.VMEM((N_CHUNKS // 2, CHUNK, 2 * GATE_ROWS), F32)],
        compiler_params=_cparams(("arbitrary", "arbitrary")),
        name="mlstm_mixer",
    )(big, big, big, big, gates, fb, norm_g.reshape(1, BRANCH_W), tri, eye)


CONV_BLK = 256
CONV_HALO = 16
SOLVE_BLK = 2
SOLVE_GROUP = 4
PREP_GROUPS = 3


def _conv_silu(src_ref, w, dst_ref, l2_scale):
    n_blk = S_ALL // CONV_BLK
    zeros = jnp.zeros((CONV_HALO, src_ref.shape[-1]), F32)
    for i in range(n_blk):
        lo = i * CONV_BLK
        first = i == 0 or lo == CTX_LEN
        last = lo + CONV_BLK in (CTX_LEN, S_ALL)
        body = src_ref[0, lo - (0 if first else CONV_HALO):lo + CONV_BLK + (0 if last else CONV_HALO), :]
        win = jnp.concatenate(([zeros] if first else []) + [body.astype(F32)] + ([zeros] if last else []), axis=0)
        n = win.shape[0]
        y = None
        for j in range(CONV_W):
            tap = win if j == CONV_W // 2 else pltpu.roll(win, (CONV_W // 2 - j) % n, 0)
            term = tap[CONV_HALO:CONV_HALO + CONV_BLK] * w[j:j + 1, :]
            y = term if y is None else y + term
        y = y * _sigmoid(y)
        if l2_scale is not None:
            y = y * (lax.rsqrt(jnp.sum(y * y, axis=-1, keepdims=True) + EPS) * l2_scale)
        dst_ref[lo:lo + CONV_BLK, :] = y.astype(dst_ref.dtype)


def _block_masks():
    n = SOLVE_GROUP * CHUNK
    t = np.arange(n)[:, None]
    s = np.arange(n)[None, :]
    masks = [(t // SOLVE_BLK) == (s // SOLVE_BLK)]
    k = SOLVE_BLK
    while k < CHUNK:
        masks.append(((t // (2 * k)) == (s // (2 * k))) & ((t // k) != (s // k)))
        k *= 2
    return jnp.asarray(np.stack(masks).astype(np.float32))


def _unit_triangular_inverse(systems, eye, masks):
    ts = [eye - a * masks[0] for a in systems]
    for lvl in range(1, masks.shape[0]):
        tbs = [t.astype(BF16) for t in ts]
        tls = [jnp.dot(tb, (a * masks[lvl]).astype(BF16), preferred_element_type=F32)
               for tb, a in zip(tbs, systems)]
        ts = [t - jnp.dot(tl.astype(BF16), tb, preferred_element_type=F32) for t, tl, tb in zip(ts, tls, tbs)]
    return ts


def _block_diag(blocks):
    n = len(blocks)
    z_half = jnp.zeros((CHUNK, CHUNK), F32)
    z_tile = jnp.zeros((CHUNK, 2 * CHUNK), F32)
    out = []
    for p, blk in enumerate(blocks):
        pair = jnp.concatenate([blk, z_half] if p % 2 == 0 else [z_half, blk], axis=1)
        out.append(jnp.concatenate([pair if t == p // 2 else z_tile for t in range(n // 2)], axis=1))
    return jnp.concatenate(out, axis=0)


def _gdn_kernel(xq_ref, xk_ref, xv_ref, z_ref, g_ref, gp_ref, cw_ref, ng_ref, tri_ref, eye_ref, bm_ref, y_ref,
                q_s, k_s, v_s, u_s, wq_s, at_s, kt_s, cd_s, acc_ref):
    tri, eye, masks = tri_ref[...], eye_ref[...], bm_ref[...]
    cw = cw_ref[0]
    _conv_silu(xq_ref, cw[:, :G_DK], q_s, G_DK ** -0.5)
    _conv_silu(xk_ref, cw[:, G_DK:2 * G_DK], k_s, 1.0)
    _conv_silu(xv_ref, cw[:, 2 * G_DK:], v_s, None)
    n_pair = SOLVE_GROUP // 2
    kind = lax.broadcasted_iota(jnp.int32, (n_pair * GATE_ROWS, CHUNK), 0) & 1
    neg_a = jnp.concatenate([gp_ref[0, 0]] * n_pair, axis=0)
    dt_bias = jnp.concatenate([gp_ref[0, 1]] * n_pair, axis=0)
    n_bd = SOLVE_GROUP * CHUNK
    eye_bd = (lax.broadcasted_iota(jnp.int32, (n_bd, n_bd), 0)
              == lax.broadcasted_iota(jnp.int32, (n_bd, n_bd), 1)).astype(F32)
    z_half = jnp.zeros((CHUNK, CHUNK), F32)

    def chunk_rows(c):
        return pl.ds(pl.multiple_of(c * CHUNK, CHUNK), CHUNK)

    def wq_rows(c):
        return pl.ds(pl.multiple_of(c * 2 * CHUNK, 2 * CHUNK), 2 * CHUNK)

    def prep(g, carry):
        systems, rhs_all, q_dec_all = [], [], []
        for sgrp in range(PREP_GROUPS):
            c0 = (g * PREP_GROUPS + sgrp) * n_pair
            raw = jnp.concatenate([g_ref[0, 0, c0 + i] for i in range(n_pair)], axis=0)
            vals = jnp.where(kind == 1, neg_a * _softplus(raw + dt_bias), _sigmoid(raw))
            g_rows, g_cols, totals = _gate_forms(vals, tri, eye)
            blocks, rhs, q_dec = [], [], []
            for i in range(n_pair):
                c = c0 + i
                q, k, v = q_s[chunk_rows(c), :], k_s[chunk_rows(c), :], v_s[chunk_rows(c), :]
                kf, vf, qf = k.astype(F32), v.astype(F32), q.astype(F32)
                kq = _nt_dot(jnp.concatenate([k, q], axis=0), k)
                kk, qk = kq[:CHUNK], kq[CHUNK:]
                for d in range(2):
                    gam_r, beta_c, gam_c = _run_row(g_rows, d, i), _x_col(g_cols, d, i), _run_col(g_cols, d, i)
                    tot = totals[GATE_ROWS * i + 2 * d + 1:GATE_ROWS * i + 2 * d + 2]
                    incl, strict = _dir_mask(d)
                    decay = jnp.exp(jnp.where(incl, gam_c - gam_r, NEG_INF))
                    blocks.append(jnp.where(strict, kk * beta_c * decay, 0.0))
                    e_gam = jnp.exp(gam_c)
                    rhs.append(jnp.concatenate([vf * beta_c, kf * (beta_c * e_gam)], axis=-1))
                    q_dec.append(qf * e_gam)
                    attn = qk * decay
                    at_s[d, chunk_rows(c), :] = jnp.concatenate([attn, z_half] if d == 0 else [z_half, attn],
                                                                axis=1).astype(BF16)
                    kt_s[d, chunk_rows(c), :] = (kf * jnp.exp(tot - gam_c)).astype(BF16)
                    cd_s[d, c] = jnp.broadcast_to(jnp.exp(tot), cd_s.shape[2:])
            systems.append(_block_diag(blocks))
            rhs_all.append(jnp.concatenate(rhs, axis=0).astype(BF16))
            q_dec_all.append(q_dec)
        t_invs = _unit_triangular_inverse(systems, eye_bd, masks)
        for sgrp in range(PREP_GROUPS):
            c0 = (g * PREP_GROUPS + sgrp) * n_pair
            x = jnp.dot(t_invs[sgrp].astype(BF16), rhs_all[sgrp], preferred_element_type=F32)
            for i in range(n_pair):
                for d in range(2):
                    p = 2 * i + d
                    xp = x[p * CHUNK:(p + 1) * CHUNK]
                    u_s[d, chunk_rows(c0 + i), :] = xp[:, :G_DV]
                    wq_s[d, wq_rows(c0 + i), :] = jnp.concatenate([xp[:, G_DV:], q_dec_all[sgrp][p]],
                                                                  axis=0).astype(BF16)
        return carry

    lax.fori_loop(0, N_CHUNKS // (n_pair * PREP_GROUPS), prep, 0)

    z_v = jnp.zeros((CHUNK, G_DV), BF16)

    def step(j, s_prev):
        cs = (j, _rev_chunk(j))
        ws_qs = [jnp.dot(wq_s[d, wq_rows(cs[d]), :], s_prev[d].astype(BF16), preferred_element_type=F32)
                 for d in range(2)]
        v_new = [(u_s[d, chunk_rows(cs[d]), :] - ws_qs[d][:CHUNK]).astype(BF16) for d in range(2)]
        o = [jnp.dot(at_s[d, chunk_rows(cs[d]), :],
                     jnp.concatenate([v_new[d], z_v] if d == 0 else [z_v, v_new[d]], axis=0),
                     preferred_element_type=F32) for d in range(2)]
        ktv = [_tn_dot(kt_s[d, chunk_rows(cs[d]), :], v_new[d]) for d in range(2)]
        for d in range(2):
            acc_ref[d, chunk_rows(cs[d]), :] = ws_qs[d][CHUNK:] + o[d]
        return tuple(cd_s[d, cs[d]][:1, :] * s_prev[d] + ktv[d] for d in range(2))

    s_zero = jnp.zeros((G_DK, G_DV), F32)
    lax.fori_loop(0, N_CHUNKS, step, (s_zero, s_zero))

    def finish(c, carry):
        rows = pl.ds(pl.multiple_of(c * CHUNK, CHUNK), CHUNK)
        h = acc_ref[0, rows, :] + acc_ref[1, rows, :]
        h = h * lax.rsqrt(jnp.mean(h * h, axis=-1, keepdims=True) + EPS) * ng_ref[...]
        z = z_ref[0, rows, :].astype(F32)
        y_ref[0, rows, :] = (h * (z * _sigmoid(z))).astype(y_ref.dtype)
        return carry

    lax.fori_loop(0, N_CHUNKS, finish, 0, unroll=4)


def gdn_mixer(big, small, conv_w, a_log, dt_bias, norm_g):
    bsz = big.shape[0]
    gates = _head_gate_rows(small, 4 * M_HEADS, G_HEADS, True)
    gp = jnp.zeros((G_HEADS, 2, GATE_ROWS, CHUNK), F32)
    for d in range(2):
        gp = gp.at[:, 0, 2 * d + 1].set(-jnp.exp(a_log[d])[:, None]).at[:, 1, 2 * d + 1].set(dt_bias[d][:, None])
    cw = conv_w.reshape(CONV_W, 3, G_HEADS, G_DK).transpose(2, 0, 1, 3).reshape(G_HEADS, CONV_W, 3 * G_DK)
    cw = jnp.pad(cw, ((0, 0), (0, 8 - CONV_W), (0, 0)))
    tri, eye = _scan_constants()
    masks = _block_masks()
    qb, kb, vb, zb = C_GQ // G_DK, C_GK // G_DK, C_GV // G_DV, C_GZ // G_DV
    return pl.pallas_call(
        _gdn_kernel,
        out_shape=jax.ShapeDtypeStruct((bsz, S_ALL, BRANCH_W), BF16),
        grid=(bsz, G_HEADS),
        in_specs=[pl.BlockSpec((1, S_ALL, G_DK), lambda b, h: (b, 0, qb + h)),
                  pl.BlockSpec((1, S_ALL, G_DK), lambda b, h: (b, 0, kb + h)),
                  pl.BlockSpec((1, S_ALL, G_DV), lambda b, h: (b, 0, vb + h)),
                  pl.BlockSpec((1, S_ALL, G_DV), lambda b, h: (b, 0, zb + h)),
                  pl.BlockSpec((1, 1, N_CHUNKS, GATE_ROWS, CHUNK), lambda b, h: (b, h, 0, 0, 0)),
                  pl.BlockSpec((1, 2, GATE_ROWS, CHUNK), lambda b, h: (h, 0, 0, 0)),
                  pl.BlockSpec((1, 8, 3 * G_DK), lambda b, h: (h, 0, 0)),
                  pl.BlockSpec((1, G_DV), lambda b, h: (0, 0)),
                  pl.BlockSpec((CHUNK, 3 * CHUNK), lambda b, h: (0, 0)),
                  pl.BlockSpec((CHUNK, CHUNK), lambda b, h: (0, 0)),
                  pl.BlockSpec(masks.shape, lambda b, h: (0, 0, 0))],
        out_specs=pl.BlockSpec((1, S_ALL, G_DV), lambda b, h: (b, 0, h)),
        scratch_shapes=[pltpu.VMEM((S_ALL, G_DK), BF16), pltpu.VMEM((S_ALL, G_DK), BF16),
                        pltpu.VMEM((S_ALL, G_DV), BF16),
                        pltpu.VMEM((2, S_ALL, G_DV), F32),
                        pltpu.VMEM((2, 2 * S_ALL, G_DK), BF16),
                        pltpu.VMEM((2, S_ALL, 2 * CHUNK), BF16),
                        pltpu.VMEM((2, S_ALL, G_DK), BF16),
                        pltpu.VMEM((2, N_CHUNKS, 8, G_DV), F32),
                        pltpu.VMEM((2, S_ALL, G_DV), F32)],
        compiler_params=_cparams(("arbitrary", "arbitrary")),
        name="gdn_mixer",
    )(big, big, big, big, gates, gp, cw, norm_g.reshape(1, G_DV), tri, eye, masks)


GRID_H = SEQ // GRID_W
LRU_BLK = 256


def _shift_rows(a, s):
    n = a.shape[0]
    r = lax.broadcasted_iota(jnp.int32, a.shape, 0)
    return jnp.where((r >= s) & (r < n + s), pltpu.roll(a, s % n, 0), 0.0)


def _gelu(v):
    return 0.5 * v * (1.0 + jnp.tanh(0.7978845608028654 * (v + 0.044715 * v * v * v)))


def _lru_kernel(x_ref, y_ref, cw_ref, cb_ref, gw_ref, gb_ref, lam_ref, o_ref,
                xb_s, a_s, u_s, h_s, p_s, end_s, cin_s):
    cw = cw_ref[...]
    xf = x_ref[0].astype(F32)
    xc = xf[:CTX_LEN]
    lat = lambda r0, r1: xf[CTX_LEN + r0 * GRID_W:CTX_LEN + r1 * GRID_W]
    prev1 = jnp.concatenate([_shift_rows(lat(GRID_H - 1, GRID_H), 1), lat(0, GRID_H - 1)], axis=0)
    prev2 = jnp.concatenate([_shift_rows(lat(GRID_H - 2, GRID_H - 1), 1), _shift_rows(lat(GRID_H - 1, GRID_H), 1),
                             lat(0, GRID_H - 2)], axis=0)
    next1 = jnp.concatenate([lat(1, GRID_H), _shift_rows(lat(0, 1), -1)], axis=0)
    taps_c = (_shift_rows(xc, 2), _shift_rows(xc, 1), xc, _shift_rows(xc, -1))
    taps_l = (prev2, prev1, lat(0, GRID_H), next1)
    xb_s[:CTX_LEN, :] = sum(t * cw[j:j + 1, :] for j, t in enumerate(taps_c)) + cb_ref[...]
    xb_s[CTX_LEN:, :] = sum(t * cw[j:j + 1, :] for j, t in enumerate(taps_l)) + cb_ref[...]

    neg_c_sp = -LRU_C * _softplus(-lam_ref[...])

    def gates(i, carry):
        rows = pl.ds(pl.multiple_of(i * LRU_BLK, LRU_BLK), LRU_BLK)
        xb = xb_s[rows, :]
        pre = jnp.dot(xb.astype(BF16), gw_ref[0], preferred_element_type=F32) + gb_ref[0]
        for d in range(2):
            r = _sigmoid(pre[:, (2 * d) * LANE:(2 * d + 1) * LANE])
            g_in = _sigmoid(pre[:, (2 * d + 1) * LANE:(2 * d + 2) * LANE])
            log_a = neg_c_sp[d:d + 1, :] * r
            a = jnp.exp(log_a)
            a_s[d, rows, :] = a
            u_s[d, rows, :] = jnp.sqrt(jnp.maximum(1.0 - a * a, 0.0)) * g_in * xb
        return carry

    lax.fori_loop(0, S_ALL // LRU_BLK, gates, 0)

    def ctx_step(t, carry):
        hf, hb = carry
        tb = CTX_LEN - 1 - t
        hf = a_s[0, pl.ds(t, 1), :] * hf + u_s[0, pl.ds(t, 1), :]
        hb = a_s[1, pl.ds(tb, 1), :] * hb + u_s[1, pl.ds(tb, 1), :]
        h_s[0, pl.ds(t, 1), :] = hf
        h_s[1, pl.ds(tb, 1), :] = hb
        return hf, hb

    zero_row = jnp.zeros((1, LANE), F32)
    hf0, hb0 = lax.fori_loop(0, CTX_LEN, ctx_step, (zero_row, zero_row))

    def col_step(i, carry):
        hf, pf, hb, pb = carry
        rf = pl.ds(pl.multiple_of(CTX_LEN + i * GRID_W, GRID_W), GRID_W)
        rb = pl.ds(pl.multiple_of(CTX_LEN + (GRID_H - 1 - i) * GRID_W, GRID_W), GRID_W)
        af, ab = a_s[0, rf, :], a_s[1, rb, :]
        hf = af * hf + u_s[0, rf, :]
        pf = af * pf
        hb = ab * hb + u_s[1, rb, :]
        pb = ab * pb
        h_s[0, rf, :] = hf
        p_s[0, pl.ds(pl.multiple_of(i * GRID_W, GRID_W), GRID_W), :] = pf
        h_s[1, rb, :] = hb
        p_s[1, pl.ds(pl.multiple_of((GRID_H - 1 - i) * GRID_W, GRID_W), GRID_W), :] = pb
        return hf, pf, hb, pb

    zeros = jnp.zeros((GRID_W, LANE), F32)
    ones = jnp.ones((GRID_W, LANE), F32)
    hf, pf, hb, pb = lax.fori_loop(0, GRID_H, col_step, (zeros, ones, zeros, ones))
    end_s[0], end_s[1], end_s[2], end_s[3] = hf, pf, hb, pb

    def chain_step(w, carry):
        cf, cb = carry
        wb = GRID_W - 1 - w
        cin_s[0, pl.ds(w, 1), :] = cf
        cin_s[1, pl.ds(wb, 1), :] = cb
        cf = end_s[0, pl.ds(w, 1), :] + end_s[1, pl.ds(w, 1), :] * cf
        cb = end_s[2, pl.ds(wb, 1), :] + end_s[3, pl.ds(wb, 1), :] * cb
        return cf, cb

    lax.fori_loop(0, GRID_W, chain_step, (hf0, hb0))

    o_ref[0, :CTX_LEN, :] = ((h_s[0, :CTX_LEN, :] + h_s[1, :CTX_LEN, :])
                             * _gelu(y_ref[0, :CTX_LEN, :].astype(F32))).astype(o_ref.dtype)

    def out_step(i, carry):
        rows = pl.ds(pl.multiple_of(CTX_LEN + i * GRID_W, GRID_W), GRID_W)
        prow = pl.ds(pl.multiple_of(i * GRID_W, GRID_W), GRID_W)
        h = (h_s[0, rows, :] + p_s[0, prow, :] * cin_s[0] + h_s[1, rows, :] + p_s[1, prow, :] * cin_s[1])
        o_ref[0, rows, :] = (h * _gelu(y_ref[0, rows, :].astype(F32))).astype(o_ref.dtype)
        return carry

    lax.fori_loop(0, GRID_H, out_step, 0)


def lru_mixer(big, conv_w, conv_b, gate_w, gate_b, lam):
    bsz = big.shape[0]
    gw = gate_w.transpose(2, 3, 0, 1, 4).reshape(R_BLOCKS, R_BDIM, 4 * R_BDIM).astype(BF16)
    gb = gate_b.reshape(2, 2, R_BLOCKS, R_BDIM).transpose(2, 0, 1, 3).reshape(R_BLOCKS, 1, 4 * R_BDIM)
    cw = jnp.pad(conv_w, ((0, 8 - CONV_W), (0, 0)))
    xb, yb = C_RX // R_BDIM, C_RY // R_BDIM
    return pl.pallas_call(
        _lru_kernel,
        out_shape=jax.ShapeDtypeStruct((bsz, S_ALL, BRANCH_W), BF16),
        grid=(bsz, R_BLOCKS),
        in_specs=[pl.BlockSpec((1, S_ALL, R_BDIM), lambda b, n: (b, 0, xb + n)),
                  pl.BlockSpec((1, S_ALL, R_BDIM), lambda b, n: (b, 0, yb + n)),
                  pl.BlockSpec((8, R_BDIM), lambda b, n: (0, n)),
                  pl.BlockSpec((1, R_BDIM), lambda b, n: (0, n)),
                  pl.BlockSpec((1, R_BDIM, 4 * R_BDIM), lambda b, n: (n, 0, 0)),
                  pl.BlockSpec((1, 1, 4 * R_BDIM), lambda b, n: (n, 0, 0)),
                  pl.BlockSpec((2, R_BDIM), lambda b, n: (0, n))],
        out_specs=pl.BlockSpec((1, S_ALL, R_BDIM), lambda b, n: (b, 0, n)),
        scratch_shapes=[pltpu.VMEM((S_ALL, R_BDIM), F32), pltpu.VMEM((2, S_ALL, R_BDIM), F32),
                        pltpu.VMEM((2, S_ALL, R_BDIM), F32), pltpu.VMEM((2, S_ALL, R_BDIM), F32),
                        pltpu.VMEM((2, SEQ, R_BDIM), F32), pltpu.VMEM((4, GRID_W, R_BDIM), F32),
                        pltpu.VMEM((2, GRID_W, R_BDIM), F32)],
        compiler_params=_cparams(("arbitrary", "arbitrary")),
        name="lru_mixer",
    )(big, big, cw, conv_b.reshape(1, BRANCH_W), gw, gb, lam)


def _pack_in_weights(w, b):
    segs_w = [w[:, IN_OFFS[i]:IN_OFFS[i + 1]] for i in range(len(IN_COLS))]
    segs_b = [b[IN_OFFS[i]:IN_OFFS[i + 1]] for i in range(len(IN_COLS))]
    w_big = jnp.concatenate([segs_w[i] for i in BIG_SEGS], axis=1).astype(BF16)
    b_big = jnp.concatenate([segs_b[i] for i in BIG_SEGS]).reshape(1, N_BIG)
    pad = N_SMALL - sum(IN_COLS[i] for i in SMALL_SEGS)
    w_small = jnp.pad(jnp.concatenate([segs_w[i] for i in SMALL_SEGS], axis=1), ((0, 0), (0, pad))).astype(BF16)
    b_small = jnp.pad(jnp.concatenate([segs_b[i] for i in SMALL_SEGS]), (0, pad)).reshape(1, N_SMALL)
    return w_big, b_big, w_small, b_small


def kernel(x, c, ctx, c_ctx, mod_w, mod_b, norm1_g, norm2_g, final_g, w_in, b_in, m_fbias, m_norm_g,
           g_conv, g_a_log, g_dt_bias, g_norm_g, r_conv, r_conv_b, r_gate_w, r_gate_b, r_lambda,
           w_branch, w_out, ffn_w1, ffn_w3, ffn_w2, router_w, router_b, moe_w1, moe_w3, moe_w2):
    bsz = x.shape[0]
    xs = jnp.concatenate([ctx, x], axis=1)
    src = jnp.concatenate([c, c_ctx[None]], axis=0)
    mods = modulation(src, mod_w, mod_b)
    for layer in range(DEPTH):
        mod = mods[layer]
        w_big, b_big, w_small, b_small = _pack_in_weights(w_in[layer], b_in[layer])
        big, small = in_projection(xs, norm1_g[layer], mod, w_big, b_big, w_small, b_small)
        ym = mlstm_mixer(big, small, m_fbias[layer], m_norm_g[layer])
        yg = gdn_mixer(big, small, g_conv[layer], g_a_log[layer], g_dt_bias[layer], g_norm_g[layer])
        yr = lru_mixer(big, r_conv[layer], r_conv_b[layer], r_gate_w[layer], r_gate_b[layer], r_lambda[layer])
        xs = merge_branches(ym, yg, yr, big, xs, mod, w_branch[layer].astype(BF16), w_out[layer].astype(BF16))
        j = layer // 2
        if layer % 2 == 0:
            xs = dense_ffn(xs, norm2_g[layer], mod, ffn_w1[j].astype(BF16), ffn_w3[j].astype(BF16),
                           ffn_w2[j].astype(BF16))
        else:
            gt2 = jnp.concatenate([jnp.broadcast_to(mod[bsz:, None, 5 * D_MODEL:], (bsz, CTX_LEN, D_MODEL)),
                                   jnp.broadcast_to(mod[:bsz, None, 5 * D_MODEL:], (bsz, SEQ, D_MODEL))], axis=1)
            xs = moe_ffn(xs, norm2_g[layer], mod, gt2, router_w[j], router_b[j], moe_w1[j].astype(BF16),
                         moe_w3[j].astype(BF16), moe_w2[j].astype(BF16))
    return final_norm(xs, final_g)
```

```python
import jax
import jax.numpy as jnp
import numpy as np
from jax import lax
from jax.experimental import pallas as pl
from jax.experimental.pallas import tpu as pltpu

D_MODEL = 1024
SEQ = 2048
DEPTH = 4
GRID_W = 64
CTX_LEN = 256
S_ALL = CTX_LEN + SEQ
BRANCH_W = D_MODEL
N_BRANCH = 3
CHUNK = 64
CONV_W = 4
M_HEADS = 4
M_DQK = 128
M_DV = BRANCH_W // M_HEADS
G_HEADS = 8
G_DK = 128
G_DV = BRANCH_W // G_HEADS
R_BLOCKS = 8
R_BDIM = BRANCH_W // R_BLOCKS
LRU_C = 8.0
D_FF = 2816
N_EXPERTS = 8
TOP_K = 2
MOE_BLOCK = 512
EPS = 1e-6
IN_COLS = (M_HEADS * M_DQK, M_HEADS * M_DQK, BRANCH_W, BRANCH_W, 4 * M_HEADS,
           2 * G_HEADS * G_DK + BRANCH_W, BRANCH_W, 2 * G_HEADS, 2 * G_HEADS,
           BRANCH_W, BRANCH_W, N_BRANCH * D_MODEL)
IN_OFFS = tuple(int(v) for v in np.cumsum((0,) + IN_COLS))
SMALL_SEGS = (4, 7, 8)
BIG_SEGS = tuple(i for i in range(len(IN_COLS)) if i not in SMALL_SEGS)
N_BIG = sum(IN_COLS[i] for i in BIG_SEGS)
N_SMALL = 128
C_MQ, C_MK, C_MV, C_MO = 0, 512, 1024, 2048
C_GQ, C_GK, C_GV, C_GZ = 3072, 4096, 5120, 6144
C_RX, C_RY, C_MG = 7168, 8192, 9216

LANE = 128
VMEM_LIMIT = 56 * 1024 * 1024
F32 = jnp.float32
BF16 = jnp.bfloat16


def _cparams(sem):
    return pltpu.CompilerParams(dimension_semantics=sem, vmem_limit_bytes=VMEM_LIMIT)


def _sigmoid(v):
    return 0.5 * jnp.tanh(0.5 * v) + 0.5


def _ada_norm(x, g, mod_l, mod_c, first_row, which):
    y = x * lax.rsqrt(jnp.mean(x * x, axis=-1, keepdims=True) + EPS) * g
    sh_l, sc_l = _mod_slice(mod_l, 3 * which), _mod_slice(mod_l, 3 * which + 1)
    sh_c, sc_c = _mod_slice(mod_c, 3 * which), _mod_slice(mod_c, 3 * which + 1)
    is_ctx = _is_ctx(x.shape[0], first_row)
    return y * (1.0 + jnp.where(is_ctx, sc_c, sc_l)) + jnp.where(is_ctx, sh_c, sh_l)


def _mod_slice(mod, k):
    return mod[:, k * D_MODEL:(k + 1) * D_MODEL]


def _is_ctx(rows, first_row):
    return (lax.broadcasted_iota(jnp.int32, (rows, 1), 0) + first_row) < CTX_LEN


def _mod_kernel(src_ref, w_ref, b_ref, o_ref):
    s = src_ref[...]
    s = (s * _sigmoid(s)).astype(BF16)
    o_ref[0] = jnp.dot(s, w_ref[0].astype(BF16), preferred_element_type=F32) + b_ref[0]


def modulation(src, mod_w, mod_b):
    rows = src.shape[0]
    tn = 1536
    return pl.pallas_call(
        _mod_kernel,
        out_shape=jax.ShapeDtypeStruct((DEPTH, rows, 6 * D_MODEL), F32),
        grid=(DEPTH, 6 * D_MODEL // tn),
        in_specs=[pl.BlockSpec((rows, D_MODEL), lambda l, n: (0, 0)),
                  pl.BlockSpec((1, D_MODEL, tn), lambda l, n: (l, 0, n)),
                  pl.BlockSpec((1, 1, tn), lambda l, n: (l, 0, n))],
        out_specs=pl.BlockSpec((1, rows, tn), lambda l, n: (l, 0, n)),
        compiler_params=_cparams(("arbitrary", "arbitrary")),
        name="modulation",
    )(src, mod_w, mod_b.reshape(DEPTH, 1, 6 * D_MODEL))


IN_TM = 1152
IN_TN = 1024


def _in_kernel(x_ref, g_ref, ml_ref, mc_ref, w_ref, b_ref, ws_ref, bs_ref, o_ref, os_ref, h_ref):
    n = pl.program_id(2)

    @pl.when(n == 0)
    def _():
        h = _ada_norm(x_ref[0], g_ref[...], ml_ref[0], mc_ref[0], pl.program_id(1) * IN_TM, 0)
        h_ref[...] = h.astype(BF16)
        os_ref[0] = jnp.dot(h_ref[...], ws_ref[...], preferred_element_type=F32) + bs_ref[...]

    o_ref[0] = (jnp.dot(h_ref[...], w_ref[...], preferred_element_type=F32) + b_ref[...]).astype(o_ref.dtype)


def in_projection(xs, g, mod, w_big, b_big, w_small, b_small):
    bsz = xs.shape[0]
    mod3 = mod.reshape(bsz + 1, 1, 6 * D_MODEL)
    return pl.pallas_call(
        _in_kernel,
        out_shape=(jax.ShapeDtypeStruct((bsz, S_ALL, N_BIG), BF16),
                   jax.ShapeDtypeStruct((bsz, S_ALL, N_SMALL), F32)),
        grid=(bsz, S_ALL // IN_TM, N_BIG // IN_TN),
        in_specs=[pl.BlockSpec((1, IN_TM, D_MODEL), lambda b, s, n: (b, s, 0)),
                  pl.BlockSpec((1, D_MODEL), lambda b, s, n: (0, 0)),
                  pl.BlockSpec((1, 1, 6 * D_MODEL), lambda b, s, n: (b, 0, 0)),
                  pl.BlockSpec((1, 1, 6 * D_MODEL), lambda b, s, n: (bsz, 0, 0)),
                  pl.BlockSpec((D_MODEL, IN_TN), lambda b, s, n: (0, n)),
                  pl.BlockSpec((1, IN_TN), lambda b, s, n: (0, n)),
                  pl.BlockSpec((D_MODEL, N_SMALL), lambda b, s, n: (0, 0)),
                  pl.BlockSpec((1, N_SMALL), lambda b, s, n: (0, 0))],
        out_specs=(pl.BlockSpec((1, IN_TM, IN_TN), lambda b, s, n: (b, s, n)),
                   pl.BlockSpec((1, IN_TM, N_SMALL), lambda b, s, n: (b, s, 0))),
        scratch_shapes=[pltpu.VMEM((IN_TM, D_MODEL), BF16)],
        compiler_params=_cparams(("arbitrary", "arbitrary", "arbitrary")),
        name="in_projection",
    )(xs, g.reshape(1, D_MODEL), mod3, mod3, w_big, b_big, w_small, b_small)


MERGE_TM = 768


def _merge_kernel(ym_ref, yg_ref, yr_ref, g0_ref, g1_ref, g2_ref, x_ref, ml_ref, mc_ref, wb_ref, wo_ref, o_ref):
    acc = None
    for n, (y_ref, gp_ref) in enumerate(((ym_ref, g0_ref), (yg_ref, g1_ref), (yr_ref, g2_ref))):
        p = jnp.dot(y_ref[0].astype(BF16), wb_ref[n], preferred_element_type=F32)
        p = p * _sigmoid(gp_ref[0].astype(F32))
        acc = p if acc is None else acc + p
    out = jnp.dot(acc.astype(BF16), wo_ref[...], preferred_element_type=F32)
    is_ctx = _is_ctx(MERGE_TM, pl.program_id(1) * MERGE_TM)
    gt = jnp.where(is_ctx, _mod_slice(mc_ref[0], 2), _mod_slice(ml_ref[0], 2))
    o_ref[0] = x_ref[0] + gt * out


def merge_branches(ym, yg, yr, big, xs, mod, w_branch, w_out):
    bsz = xs.shape[0]
    mod3 = mod.reshape(bsz + 1, 1, 6 * D_MODEL)
    row = lambda b, s: (b, s, 0)
    gate_blk = C_MG // D_MODEL
    const = pl.Buffered(1)
    return pl.pallas_call(
        _merge_kernel,
        out_shape=jax.ShapeDtypeStruct(xs.shape, F32),
        grid=(bsz, S_ALL // MERGE_TM),
        in_specs=[pl.BlockSpec((1, MERGE_TM, D_MODEL), row),
                  pl.BlockSpec((1, MERGE_TM, D_MODEL), row),
                  pl.BlockSpec((1, MERGE_TM, D_MODEL), row),
                  pl.BlockSpec((1, MERGE_TM, D_MODEL), lambda b, s: (b, s, gate_blk)),
                  pl.BlockSpec((1, MERGE_TM, D_MODEL), lambda b, s: (b, s, gate_blk + 1)),
                  pl.BlockSpec((1, MERGE_TM, D_MODEL), lambda b, s: (b, s, gate_blk + 2)),
                  pl.BlockSpec((1, MERGE_TM, D_MODEL), row),
                  pl.BlockSpec((1, 1, 6 * D_MODEL), lambda b, s: (b, 0, 0)),
                  pl.BlockSpec((1, 1, 6 * D_MODEL), lambda b, s: (bsz, 0, 0)),
                  pl.BlockSpec((N_BRANCH, BRANCH_W, D_MODEL), lambda b, s: (0, 0, 0), pipeline_mode=const),
                  pl.BlockSpec((D_MODEL, D_MODEL), lambda b, s: (0, 0), pipeline_mode=const)],
        out_specs=pl.BlockSpec((1, MERGE_TM, D_MODEL), row),
        compiler_params=_cparams(("arbitrary", "arbitrary")),
        name="merge_branches",
    )(ym, yg, yr, big, big, big, xs, mod3, mod3, w_branch, w_out)


FFN_TM = 768
FFN_TF = 256


def _swiglu_chunk(h, w1, w3, w2):
    a = jnp.dot(h, w1, preferred_element_type=F32)
    b = jnp.dot(h, w3, preferred_element_type=F32)
    return jnp.dot((a * _sigmoid(a) * b).astype(BF16), w2, preferred_element_type=F32)


def _ffn_kernel(x_ref, g_ref, ml_ref, mc_ref, w1_ref, w3_ref, w2_ref, o_ref):
    first_row = pl.program_id(1) * FFN_TM
    h = _ada_norm(x_ref[0], g_ref[...], ml_ref[0], mc_ref[0], first_row, 1).astype(BF16)
    for f in range(D_FF // FFN_TF):
        cols = slice(f * FFN_TF, (f + 1) * FFN_TF)
        y = _swiglu_chunk(h, w1_ref[:, cols], w3_ref[:, cols], w2_ref[cols, :])
        if f == 0:
            o_ref[0] = y
        else:
            o_ref[0] += y
    gt = jnp.where(_is_ctx(FFN_TM, first_row), _mod_slice(mc_ref[0], 5), _mod_slice(ml_ref[0], 5))
    o_ref[0] = x_ref[0] + gt * o_ref[0]


def dense_ffn(xs, g, mod, w1, w3, w2):
    bsz = xs.shape[0]
    mod3 = mod.reshape(bsz + 1, 1, 6 * D_MODEL)
    const = pl.Buffered(1)
    return pl.pallas_call(
        _ffn_kernel,
        out_shape=jax.ShapeDtypeStruct(xs.shape, F32),
        grid=(bsz, S_ALL // FFN_TM),
        in_specs=[pl.BlockSpec((1, FFN_TM, D_MODEL), lambda b, s: (b, s, 0)),
                  pl.BlockSpec((1, D_MODEL), lambda b, s: (0, 0)),
                  pl.BlockSpec((1, 1, 6 * D_MODEL), lambda b, s: (b, 0, 0)),
                  pl.BlockSpec((1, 1, 6 * D_MODEL), lambda b, s: (bsz, 0, 0)),
                  pl.BlockSpec((D_MODEL, D_FF), lambda b, s: (0, 0), pipeline_mode=const),
                  pl.BlockSpec((D_MODEL, D_FF), lambda b, s: (0, 0), pipeline_mode=const),
                  pl.BlockSpec((D_FF, D_MODEL), lambda b, s: (0, 0), pipeline_mode=const)],
        out_specs=pl.BlockSpec((1, FFN_TM, D_MODEL), lambda b, s: (b, s, 0)),
        compiler_params=_cparams(("arbitrary", "arbitrary")),
        name="dense_ffn",
    )(xs, g.reshape(1, D_MODEL), mod3, mod3, w1, w3, w2)


NORM_TM = 768


def _norm_router_kernel(x_ref, g_ref, ml_ref, mc_ref, rw_ref, rb_ref, h_ref, lg_ref):
    h = _ada_norm(x_ref[0], g_ref[...], ml_ref[0], mc_ref[0], pl.program_id(1) * NORM_TM, 1)
    h_ref[0] = h.astype(h_ref.dtype)
    lg_ref[0] = jnp.dot(h, rw_ref[...], preferred_element_type=F32, precision=lax.Precision.HIGHEST) + rb_ref[...]


def norm_router(xs, g, mod, router_w, router_b):
    bsz = xs.shape[0]
    mod3 = mod.reshape(bsz + 1, 1, 6 * D_MODEL)
    rw = jnp.zeros((D_MODEL, LANE), F32).at[:, :N_EXPERTS].set(router_w)
    rb = jnp.zeros((1, LANE), F32).at[0, :N_EXPERTS].set(router_b)
    return pl.pallas_call(
        _norm_router_kernel,
        out_shape=(jax.ShapeDtypeStruct(xs.shape, BF16), jax.ShapeDtypeStruct((bsz, S_ALL, LANE), F32)),
        grid=(bsz, S_ALL // NORM_TM),
        in_specs=[pl.BlockSpec((1, NORM_TM, D_MODEL), lambda b, s: (b, s, 0)),
                  pl.BlockSpec((1, D_MODEL), lambda b, s: (0, 0)),
                  pl.BlockSpec((1, 1, 6 * D_MODEL), lambda b, s: (b, 0, 0)),
                  pl.BlockSpec((1, 1, 6 * D_MODEL), lambda b, s: (bsz, 0, 0)),
                  pl.BlockSpec((D_MODEL, LANE), lambda b, s: (0, 0)),
                  pl.BlockSpec((1, LANE), lambda b, s: (0, 0))],
        out_specs=(pl.BlockSpec((1, NORM_TM, D_MODEL), lambda b, s: (b, s, 0)),
                   pl.BlockSpec((1, NORM_TM, LANE), lambda b, s: (b, s, 0))),
        compiler_params=_cparams(("arbitrary", "arbitrary")),
        name="norm_router",
    )(xs, g.reshape(1, D_MODEL), mod3, mod3, rw, rb)


def _expert_kernel(be_ref, x_ref, w1_ref, w3_ref, w2_ref, o_ref):
    del be_ref
    x = x_ref[...]
    for f in range(D_FF // FFN_TF):
        cols = slice(f * FFN_TF, (f + 1) * FFN_TF)
        y = _swiglu_chunk(x, w1_ref[0, :, cols], w3_ref[0, :, cols], w2_ref[0, cols, :])
        if f == 0:
            o_ref[...] = y
        else:
            o_ref[...] += y


def grouped_experts(xb, block_e, w1, w3, w2):
    n_rows = xb.shape[0]
    const = pl.Buffered(1)
    grid_spec = pltpu.PrefetchScalarGridSpec(
        num_scalar_prefetch=1,
        grid=(n_rows // MOE_BLOCK,),
        in_specs=[pl.BlockSpec((MOE_BLOCK, D_MODEL), lambda i, be: (i, 0)),
                  pl.BlockSpec((1, D_MODEL, D_FF), lambda i, be: (be[i], 0, 0), pipeline_mode=const),
                  pl.BlockSpec((1, D_MODEL, D_FF), lambda i, be: (be[i], 0, 0), pipeline_mode=const),
                  pl.BlockSpec((1, D_FF, D_MODEL), lambda i, be: (be[i], 0, 0), pipeline_mode=const)],
        out_specs=pl.BlockSpec((MOE_BLOCK, D_MODEL), lambda i, be: (i, 0)),
    )
    return pl.pallas_call(
        _expert_kernel,
        out_shape=jax.ShapeDtypeStruct((n_rows, D_MODEL), F32),
        grid_spec=grid_spec,
        compiler_params=_cparams(("arbitrary",)),
        name="grouped_experts",
    )(block_e, xb, w1, w3, w2)


def moe_ffn(xs, g, mod, gt2, router_w, router_b, w1, w3, w2):
    bsz = xs.shape[0]
    h, logits = norm_router(xs, g, mod, router_w, router_b)
    h = h.reshape(-1, D_MODEL)
    n_tok = h.shape[0]
    top_logit, top_e = lax.top_k(logits.reshape(n_tok, LANE)[:, :N_EXPERTS], TOP_K)
    gate = jax.nn.softmax(top_logit, axis=-1)
    n_assign = n_tok * TOP_K
    flat_e = top_e.reshape(-1)
    flat_tok = jnp.repeat(jnp.arange(n_tok, dtype=jnp.int32), TOP_K)
    onehot = (flat_e[:, None] == jnp.arange(N_EXPERTS, dtype=flat_e.dtype)[None, :]).astype(jnp.int32)
    seen = jnp.cumsum(onehot, axis=0)
    padded = (seen[-1] + MOE_BLOCK - 1) // MOE_BLOCK * MOE_BLOCK
    p_end = jnp.cumsum(padded)
    slot = jnp.sum(onehot * ((p_end - padded)[None, :] + seen - 1), axis=1)
    n_blocks = -(-n_assign // MOE_BLOCK) + N_EXPERTS
    n_rows = n_blocks * MOE_BLOCK
    row_tok = jnp.zeros((n_rows,), jnp.int32).at[slot].set(flat_tok)
    slot = slot.reshape(n_tok, TOP_K)
    block_e = jnp.minimum(jnp.searchsorted(p_end, jnp.arange(n_blocks) * MOE_BLOCK, side='right'),
                          N_EXPERTS - 1).astype(jnp.int32)
    yb = grouped_experts(h[row_tok], block_e, w1, w3, w2)
    y = sum(yb[slot[:, k]] * gate[:, k:k + 1] for k in range(TOP_K))
    return xs + gt2 * y.reshape(bsz, S_ALL, D_MODEL)


FINAL_TM = 256


def _final_kernel(x_ref, g_ref, o_ref):
    x = x_ref[0]
    o_ref[0] = x * lax.rsqrt(jnp.mean(x * x, axis=-1, keepdims=True) + EPS) * g_ref[...]


def final_norm(xs, g):
    bsz = xs.shape[0]
    skip = CTX_LEN // FINAL_TM
    return pl.pallas_call(
        _final_kernel,
        out_shape=jax.ShapeDtypeStruct((bsz, SEQ, D_MODEL), F32),
        grid=(bsz, SEQ // FINAL_TM),
        in_specs=[pl.BlockSpec((1, FINAL_TM, D_MODEL), lambda b, s: (b, s + skip, 0)),
                  pl.BlockSpec((1, D_MODEL), lambda b, s: (0, 0))],
        out_specs=pl.BlockSpec((1, FINAL_TM, D_MODEL), lambda b, s: (b, s, 0)),
        compiler_params=_cparams(("arbitrary", "arbitrary")),
        name="final_norm",
    )(xs, g.reshape(1, D_MODEL))


N_CHUNKS = S_ALL // CHUNK
CTX_CHUNKS = CTX_LEN // CHUNK
GATE_ROWS = 8
NEG_INF = float("-inf")


def _softplus(v):
    return jnp.maximum(v, 0.0) + jnp.log(1.0 + jnp.exp(-jnp.abs(v)))


def _scan_constants():
    s = np.arange(CHUNK)[:, None]
    t = np.arange(CHUNK)[None, :]
    tri = np.concatenate([(s <= t), (s >= t), np.ones((CHUNK, CHUNK), bool)], axis=1).astype(np.float32)
    return jnp.asarray(tri), jnp.asarray(np.eye(CHUNK, dtype=np.float32))


def _rev_chunk(j):
    return jnp.where(j < CTX_CHUNKS, CTX_CHUNKS - 1 - j, N_CHUNKS + CTX_CHUNKS - 1 - j)


def _head_gate_rows(small, c0, heads, kind_major):
    bsz = small.shape[0]
    a = small[:, :, c0:c0 + 4 * heads].reshape(bsz, N_CHUNKS, CHUNK, 2, 2, heads)
    a = a.transpose((0, 5, 1, 4, 3, 2) if kind_major else (0, 5, 1, 3, 4, 2))
    a = a.reshape(bsz, heads, N_CHUNKS, 4, CHUNK)
    return jnp.pad(a, ((0, 0), (0, 0), (0, 0), (0, GATE_ROWS - 4), (0, 0)))


def _tn_dot(a, b):
    return lax.dot_general(a, b, (((0,), (0,)), ((), ())), preferred_element_type=F32)


def _nt_dot(a, b, precision=None):
    return lax.dot_general(a, b, (((1,), (1,)), ((), ())), preferred_element_type=F32, precision=precision)


def _dir_mask(d):
    t = lax.broadcasted_iota(jnp.int32, (CHUNK, CHUNK), 0)
    s = lax.broadcasted_iota(jnp.int32, (CHUNK, CHUNK), 1)
    return (s <= t, s < t) if d == 0 else (s >= t, s > t)


def _split_dot(a, ones, dot_fn):
    b = ones.astype(BF16)
    hi = a.astype(BF16)
    lo = (a - hi.astype(F32)).astype(BF16)
    return dot_fn(hi, b, preferred_element_type=F32) + dot_fn(lo, b, preferred_element_type=F32)


def _gate_forms(vals, tri, eye):
    cum = _split_dot(vals, tri, jnp.dot)
    kind = lax.broadcasted_iota(jnp.int32, vals.shape, 0) & (GATE_ROWS - 1)
    rows = jnp.where(kind == 1, cum[:, :CHUNK], jnp.where(kind == 3, cum[:, CHUNK:2 * CHUNK], vals))
    cols = _split_dot(rows, eye, lambda a, b, **kw: _nt_dot(b, a))
    return rows, cols, cum[:, 2 * CHUNK:2 * CHUNK + 1]


def _x_row(rows, d, g=0):
    return rows[GATE_ROWS * g + 2 * d:GATE_ROWS * g + 2 * d + 1]


def _run_row(rows, d, g=0):
    return rows[GATE_ROWS * g + 2 * d + 1:GATE_ROWS * g + 2 * d + 2]


def _x_col(cols, d, g=0):
    return cols[:, GATE_ROWS * g + 2 * d:GATE_ROWS * g + 2 * d + 1]


def _run_col(cols, d, g=0):
    return cols[:, GATE_ROWS * g + 2 * d + 1:GATE_ROWS * g + 2 * d + 2]


PAIRS_PER_STEP = 3
OUT_PAIRS_PER_STEP = 1


def _mlstm_kernel(q_ref, k_ref, v_ref, o_ref, g_ref, fb_ref, ng_ref, tri_ref, eye_ref, y_ref,
                  c_s, st_s, gr_s, gc_s):
    tri, eye = tri_ref[...], eye_ref[...]
    fbias = fb_ref[0]
    scale = M_DQK ** -0.5
    kind = lax.broadcasted_iota(jnp.int32, (2 * GATE_ROWS, CHUNK), 0) & 1
    fbias2 = jnp.concatenate([fbias, fbias], axis=0)
    st_row = lax.broadcasted_iota(jnp.int32, (GATE_ROWS, M_DQK), 0)

    def pair_rows(c0):
        return pl.ds(pl.multiple_of(c0 * CHUNK, 2 * CHUNK), 2 * CHUNK)

    def state_rows(c):
        return pl.ds(pl.multiple_of(c * M_DQK, M_DQK), M_DQK)

    def pair_state_rows(c0):
        return pl.ds(pl.multiple_of(c0 * M_DQK, 2 * M_DQK), 2 * M_DQK)

    def local(it, carry):
        pairs = [it * PAIRS_PER_STEP + p for p in range(PAIRS_PER_STEP)]
        forms = []
        for g in pairs:
            raw = jnp.concatenate([g_ref[0, 0, 2 * g], g_ref[0, 0, 2 * g + 1]], axis=0)
            vals = jnp.where(kind == 1, -_softplus(-(raw + fbias2)), raw)
            forms.append(_gate_forms(vals, tri, eye))
        zero = jnp.zeros((CHUNK, 2 * M_DQK), F32)
        lhs_all = []
        for g, (rows, cols, totals) in zip(pairs, forms):
            gr_s[g] = rows
            gc_s[g] = cols
            kf2 = k_ref[0, pair_rows(2 * g), :].astype(F32)
            lhs = []
            for i in range(2):
                kf = kf2[i * CHUNK:(i + 1) * CHUNK]
                kws = []
                for d in range(2):
                    tot = totals[GATE_ROWS * i + 2 * d + 1:GATE_ROWS * i + 2 * d + 2]
                    m_loc = jnp.max(tot - _run_row(rows, d, i) + _x_row(rows, d, i), axis=-1, keepdims=True)
                    kw = kf * (jnp.exp(tot - _run_col(cols, d, i) + _x_col(cols, d, i) - m_loc) * scale)
                    kws.append(kw)
                    st_s[d, 2 * g + i] = jnp.where(st_row == 0, jnp.sum(kw, axis=0, keepdims=True),
                                                   jnp.where(st_row == 1, m_loc, tot))
                lhs.append(jnp.concatenate(kws + [zero] if i == 0 else [zero] + kws, axis=1))
            lhs_all.append(jnp.concatenate(lhs, axis=0).astype(BF16))
        for g, lhs in zip(pairs, lhs_all):
            c_loc = _tn_dot(lhs, v_ref[0, pair_rows(2 * g), :])
            for i in range(2):
                for d in range(2):
                    c_s[d, state_rows(2 * g + i), :] = c_loc[(2 * i + d) * M_DQK:(2 * i + d + 1) * M_DQK]
        return carry

    lax.fori_loop(0, N_CHUNKS // (2 * PAIRS_PER_STEP), local, 0)

    def scan(d):
        def body(j, carry):
            c_prev, n_prev, m_prev = carry
            c = j if d == 0 else _rev_chunk(j)
            st = st_s[d, c]
            n_loc, m_loc, tot = st[0:1], st[1:2, :1], st[2:3, :1]
            c_loc = c_s[d, state_rows(c), :]
            c_s[d, state_rows(c), :] = c_prev
            st_s[d, c] = jnp.where(st_row == 0, n_prev, m_prev)
            m_new = jnp.maximum(tot + m_prev, m_loc)
            s_prev = jnp.exp(tot + m_prev - m_new)
            s_loc = jnp.exp(m_loc - m_new)
            return s_prev * c_prev + s_loc * c_loc, s_prev * n_prev + s_loc * n_loc, m_new

        lax.fori_loop(0, N_CHUNKS, body, (jnp.zeros((M_DQK, M_DV), F32), jnp.zeros((1, M_DQK), F32),
                                          jnp.zeros((1, 1), F32)))

    scan(0)
    scan(1)

    def output(it, carry):
        pairs = [it * OUT_PAIRS_PER_STEP + p for p in range(OUT_PAIRS_PER_STEP)]
        qk_all = [_nt_dot(q_ref[0, pair_rows(2 * g), :], k_ref[0, pair_rows(2 * g), :]) * scale for g in pairs]
        z_intra = jnp.zeros((CHUNK, 2 * CHUNK), F32)
        z_inter = jnp.zeros((CHUNK, M_DQK), F32)
        operands = []
        for g, qk2 in zip(pairs, qk_all):
            c0 = 2 * g
            qf2, v2 = q_ref[0, pair_rows(c0), :].astype(F32), v_ref[0, pair_rows(c0), :]
            rows, cols = gr_s[g], gc_s[g]
            lhs_intra, lhs_inter, rhs_v = [], [], []
            for i in range(2):
                qk = qk2[i * CHUNK:(i + 1) * CHUNK, i * CHUNK:(i + 1) * CHUNK]
                qf = qf2[i * CHUNK:(i + 1) * CHUNK]
                intra, inter = [], []
                for d in range(2):
                    st = st_s[d, c0 + i]
                    b_c = _run_col(cols, d, i)
                    n0, m0 = st[0:1], st[1:2, :1]
                    incl, _ = _dir_mask(d)
                    d_log = jnp.where(incl, b_c - _run_row(rows, d, i) + _x_row(rows, d, i), NEG_INF)
                    m_inter = b_c + m0
                    m_comb = jnp.maximum(m_inter, jnp.max(d_log, axis=-1, keepdims=True))
                    s = qk * jnp.exp(d_log - m_comb)
                    e_inter = jnp.exp(m_inter - m_comb)
                    den = (jnp.sum(s, axis=-1, keepdims=True)
                           + e_inter * jnp.sum(qf * n0, axis=-1, keepdims=True))
                    inv = 1.0 / jnp.maximum(jnp.abs(den), jnp.exp(-m_comb))
                    intra.append(s * inv)
                    inter.append(qf * (e_inter * inv))
                pair = jnp.concatenate(intra, axis=1)
                lhs_intra.append(jnp.concatenate([pair, z_intra] if i == 0 else [z_intra, pair], axis=1))
                lhs_inter.append(jnp.concatenate([inter[0], z_inter, inter[1], z_inter] if i == 0
                                                 else [z_inter, inter[0], z_inter, inter[1]], axis=1))
                v = v2[i * CHUNK:(i + 1) * CHUNK]
                rhs_v += [v, v]
            c_in = jnp.concatenate([c_s[0, pair_state_rows(c0), :], c_s[1, pair_state_rows(c0), :]], axis=0)
            operands.append((jnp.concatenate(lhs_intra, axis=0).astype(BF16), jnp.concatenate(rhs_v, axis=0),
                             jnp.concatenate(lhs_inter, axis=0).astype(BF16), c_in.astype(BF16)))
        hs = [jnp.dot(li, rv, preferred_element_type=F32) + jnp.dot(le, ci, preferred_element_type=F32)
              for li, rv, le, ci in operands]
        for g, h in zip(pairs, hs):
            h = h * lax.rsqrt(jnp.mean(h * h, axis=-1, keepdims=True) + EPS) * ng_ref[...]
            y_ref[0, pair_rows(2 * g), :] = (h * _sigmoid(o_ref[0, pair_rows(2 * g), :].astype(F32))
                                             ).astype(y_ref.dtype)
        return carry

    lax.fori_loop(0, N_CHUNKS // (2 * OUT_PAIRS_PER_STEP), output, 0)


def mlstm_mixer(big, small, f_bias, norm_g):
    bsz = big.shape[0]
    gates = _head_gate_rows(small, 0, M_HEADS, False)
    fb = jnp.zeros((M_HEADS, GATE_ROWS, CHUNK), F32)
    fb = fb.at[:, 1].set(f_bias[0][:, None]).at[:, 3].set(f_bias[1][:, None])
    tri, eye = _scan_constants()
    qb, kb, vb, ob = C_MQ // M_DQK, C_MK // M_DQK, C_MV // M_DV, C_MO // M_DV
    return pl.pallas_call(
        _mlstm_kernel,
        out_shape=jax.ShapeDtypeStruct((bsz, S_ALL, BRANCH_W), BF16),
        grid=(bsz, M_HEADS),
        in_specs=[pl.BlockSpec((1, S_ALL, M_DQK), lambda b, h: (b, 0, qb + h)),
                  pl.BlockSpec((1, S_ALL, M_DQK), lambda b, h: (b, 0, kb + h)),
                  pl.BlockSpec((1, S_ALL, M_DV), lambda b, h: (b, 0, vb + h)),
                  pl.BlockSpec((1, S_ALL, M_DV), lambda b, h: (b, 0, ob + h)),
                  pl.BlockSpec((1, 1, N_CHUNKS, GATE_ROWS, CHUNK), lambda b, h: (b, h, 0, 0, 0)),
                  pl.BlockSpec((1, GATE_ROWS, CHUNK), lambda b, h: (h, 0, 0)),
                  pl.BlockSpec((1, M_DV), lambda b, h: (0, h)),
                  pl.BlockSpec((CHUNK, 3 * CHUNK), lambda b, h: (0, 0)),
                  pl.BlockSpec((CHUNK, CHUNK), lambda b, h: (0, 0))],
        out_specs=pl.BlockSpec((1, S_ALL, M_DV), lambda b, h: (b, 0, h)),
        scratch_shapes=[pltpu.VMEM((2, N_CHUNKS * M_DQK, M_DV), F32),
                        pltpu.VMEM((2, N_CHUNKS, GATE_ROWS, M_DQK), F32),
                        pltpu.VMEM((N_CHUNKS // 2, 2 * GATE_ROWS, CHUNK), F32),
                        pltpu.VMEM((N_CHUNKS // 2, CHUNK, 2 * GATE_ROWS), F32)],
        compiler_params=_cparams(("arbitrary", "arbitrary")),
        name="mlstm_mixer",
    )(big, big, big, big, gates, fb, norm_g.reshape(1, BRANCH_W), tri, eye)


CONV_BLK = 256
CONV_HALO = 16
SOLVE_BLK = 2
SOLVE_GROUP = 4
PREP_GROUPS = 3


def _conv_silu(src_ref, w, dst_ref, l2_scale):
    n_blk = S_ALL // CONV_BLK
    zeros = jnp.zeros((CONV_HALO, src_ref.shape[-1]), F32)
    for i in range(n_blk):
        lo = i * CONV_BLK
        first = i == 0 or lo == CTX_LEN
        last = lo + CONV_BLK in (CTX_LEN, S_ALL)
        body = src_ref[0, lo - (0 if first else CONV_HALO):lo + CONV_BLK + (0 if last else CONV_HALO), :]
        win = jnp.concatenate(([zeros] if first else []) + [body.astype(F32)] + ([zeros] if last else []), axis=0)
        n = win.shape[0]
        y = None
        for j in range(CONV_W):
            tap = win if j == CONV_W // 2 else pltpu.roll(win, (CONV_W // 2 - j) % n, 0)
            term = tap[CONV_HALO:CONV_HALO + CONV_BLK] * w[j:j + 1, :]
            y = term if y is None else y + term
        y = y * _sigmoid(y)
        if l2_scale is not None:
            y = y * (lax.rsqrt(jnp.sum(y * y, axis=-1, keepdims=True) + EPS) * l2_scale)
        dst_ref[lo:lo + CONV_BLK, :] = y.astype(dst_ref.dtype)


def _block_masks():
    n = SOLVE_GROUP * CHUNK
    t = np.arange(n)[:, None]
    s = np.arange(n)[None, :]
    masks = [(t // SOLVE_BLK) == (s // SOLVE_BLK)]
    k = SOLVE_BLK
    while k < CHUNK:
        masks.append(((t // (2 * k)) == (s // (2 * k))) & ((t // k) != (s // k)))
        k *= 2
    return jnp.asarray(np.stack(masks).astype(np.float32), BF16)


def _unit_triangular_inverse(systems, eye, masks):
    abs_ = [a.astype(BF16) for a in systems]
    ts = [eye - (ab * masks[0]).astype(F32) for ab in abs_]
    for lvl in range(1, masks.shape[0]):
        tbs = [t.astype(BF16) for t in ts]
        tls = [jnp.dot(tb, ab * masks[lvl], preferred_element_type=F32) for tb, ab in zip(tbs, abs_)]
        ts = [t - jnp.dot(tl.astype(BF16), tb, preferred_element_type=F32) for t, tl, tb in zip(ts, tls, tbs)]
    return ts


def _block_diag(blocks):
    n = len(blocks)
    z_half = jnp.zeros((CHUNK, CHUNK), F32)
    z_tile = jnp.zeros((CHUNK, 2 * CHUNK), F32)
    out = []
    for p, blk in enumerate(blocks):
        pair = jnp.concatenate([blk, z_half] if p % 2 == 0 else [z_half, blk], axis=1)
        out.append(jnp.concatenate([pair if t == p // 2 else z_tile for t in range(n // 2)], axis=1))
    return jnp.concatenate(out, axis=0)


def _gdn_kernel(xq_ref, xk_ref, xv_ref, z_ref, g_ref, gp_ref, cw_ref, ng_ref, tri_ref, eye_ref, bm_ref, y_ref,
                q_s, k_s, v_s, u_s, wq_s, at_s, kt_s, cd_s, acc_ref):
    tri, eye, masks = tri_ref[...], eye_ref[...], bm_ref[...]
    cw = cw_ref[0]
    _conv_silu(xq_ref, cw[:, :G_DK], q_s, G_DK ** -0.5)
    _conv_silu(xk_ref, cw[:, G_DK:2 * G_DK], k_s, 1.0)
    _conv_silu(xv_ref, cw[:, 2 * G_DK:], v_s, None)
    n_pair = SOLVE_GROUP // 2
    kind = lax.broadcasted_iota(jnp.int32, (n_pair * GATE_ROWS, CHUNK), 0) & 1
    neg_a = jnp.concatenate([gp_ref[0, 0]] * n_pair, axis=0)
    dt_bias = jnp.concatenate([gp_ref[0, 1]] * n_pair, axis=0)
    n_bd = SOLVE_GROUP * CHUNK
    eye_bd = (lax.broadcasted_iota(jnp.int32, (n_bd, n_bd), 0)
              == lax.broadcasted_iota(jnp.int32, (n_bd, n_bd), 1)).astype(F32)
    z_half = jnp.zeros((CHUNK, CHUNK), F32)

    def chunk_rows(c):
        return pl.ds(pl.multiple_of(c * CHUNK, CHUNK), CHUNK)

    def wq_rows(c):
        return pl.ds(pl.multiple_of(c * 2 * CHUNK, 2 * CHUNK), 2 * CHUNK)

    def prep(g, carry):
        systems, rhs_all, q_dec_all = [], [], []
        for sgrp in range(PREP_GROUPS):
            c0 = (g * PREP_GROUPS + sgrp) * n_pair
            raw = jnp.concatenate([g_ref[0, 0, c0 + i] for i in range(n_pair)], axis=0)
            vals = jnp.where(kind == 1, neg_a * _softplus(raw + dt_bias), _sigmoid(raw))
            g_rows, g_cols, totals = _gate_forms(vals, tri, eye)
            blocks, rhs, q_dec = [], [], []
            for i in range(n_pair):
                c = c0 + i
                q, k, v = q_s[chunk_rows(c), :], k_s[chunk_rows(c), :], v_s[chunk_rows(c), :]
                kf, vf, qf = k.astype(F32), v.astype(F32), q.astype(F32)
                kq = _nt_dot(jnp.concatenate([k, q], axis=0), k)
                kk, qk = kq[:CHUNK], kq[CHUNK:]
                for d in range(2):
                    gam_r, beta_c, gam_c = _run_row(g_rows, d, i), _x_col(g_cols, d, i), _run_col(g_cols, d, i)
                    tot = totals[GATE_ROWS * i + 2 * d + 1:GATE_ROWS * i + 2 * d + 2]
                    incl, strict = _dir_mask(d)
                    decay = jnp.exp(jnp.where(incl, gam_c - gam_r, NEG_INF))
                    blocks.append(jnp.where(strict, kk * beta_c * decay, 0.0))
                    e_gam = jnp.exp(gam_c)
                    rhs.append(jnp.concatenate([vf * beta_c, kf * (beta_c * e_gam)], axis=-1))
                    q_dec.append(qf * e_gam)
                    attn = qk * decay
                    at_s[d, chunk_rows(c), :] = jnp.concatenate([attn, z_half] if d == 0 else [z_half, attn],
                                                                axis=1).astype(BF16)
                    kt_s[d, chunk_rows(c), :] = (kf * jnp.exp(tot - gam_c)).astype(BF16)
                    cd_s[d, c] = jnp.broadcast_to(jnp.exp(tot), cd_s.shape[2:])
            systems.append(_block_diag(blocks))
            rhs_all.append(jnp.concatenate(rhs, axis=0).astype(BF16))
            q_dec_all.append(q_dec)
        t_invs = _unit_triangular_inverse(systems, eye_bd, masks)
        for sgrp in range(PREP_GROUPS):
            c0 = (g * PREP_GROUPS + sgrp) * n_pair
            x = jnp.dot(t_invs[sgrp].astype(BF16), rhs_all[sgrp], preferred_element_type=F32)
            for i in range(n_pair):
                for d in range(2):
                    p = 2 * i + d
                    xp = x[p * CHUNK:(p + 1) * CHUNK]
                    u_s[d, chunk_rows(c0 + i), :] = xp[:, :G_DV]
                    wq_s[d, wq_rows(c0 + i), :] = jnp.concatenate([xp[:, G_DV:], q_dec_all[sgrp][p]],
                                                                  axis=0).astype(BF16)
        return carry

    lax.fori_loop(0, N_CHUNKS // (n_pair * PREP_GROUPS), prep, 0)

    z_v = jnp.zeros((CHUNK, G_DV), BF16)

    def step(j, s_prev):
        cs = (j, _rev_chunk(j))
        ws_qs = [jnp.dot(wq_s[d, wq_rows(cs[d]), :], s_prev[d].astype(BF16), preferred_element_type=F32)
                 for d in range(2)]
        v_new = [(u_s[d, chunk_rows(cs[d]), :] - ws_qs[d][:CHUNK]).astype(BF16) for d in range(2)]
        o = [jnp.dot(at_s[d, chunk_rows(cs[d]), :],
                     jnp.concatenate([v_new[d], z_v] if d == 0 else [z_v, v_new[d]], axis=0),
                     preferred_element_type=F32) for d in range(2)]
        ktv = [_tn_dot(kt_s[d, chunk_rows(cs[d]), :], v_new[d]) for d in range(2)]
        for d in range(2):
            acc_ref[d, chunk_rows(cs[d]), :] = ws_qs[d][CHUNK:] + o[d]
        return tuple(cd_s[d, cs[d]][:1, :] * s_prev[d] + ktv[d] for d in range(2))

    s_zero = jnp.zeros((G_DK, G_DV), F32)
    lax.fori_loop(0, N_CHUNKS, step, (s_zero, s_zero))

    def finish(c, carry):
        rows = pl.ds(pl.multiple_of(c * CHUNK, CHUNK), CHUNK)
        h = acc_ref[0, rows, :] + acc_ref[1, rows, :]
        h = h * lax.rsqrt(jnp.mean(h * h, axis=-1, keepdims=True) + EPS) * ng_ref[...]
        z = z_ref[0, rows, :].astype(F32)
        y_ref[0, rows, :] = (h * (z * _sigmoid(z))).astype(y_ref.dtype)
        return carry

    lax.fori_loop(0, N_CHUNKS, finish, 0, unroll=4)


def gdn_mixer(big, small, conv_w, a_log, dt_bias, norm_g):
    bsz = big.shape[0]
    gates = _head_gate_rows(small, 4 * M_HEADS, G_HEADS, True)
    gp = jnp.zeros((G_HEADS, 2, GATE_ROWS, CHUNK), F32)
    for d in range(2):
        gp = gp.at[:, 0, 2 * d + 1].set(-jnp.exp(a_log[d])[:, None]).at[:, 1, 2 * d + 1].set(dt_bias[d][:, None])
    cw = conv_w.reshape(CONV_W, 3, G_HEADS, G_DK).transpose(2, 0, 1, 3).reshape(G_HEADS, CONV_W, 3 * G_DK)
    cw = jnp.pad(cw, ((0, 0), (0, 8 - CONV_W), (0, 0)))
    tri, eye = _scan_constants()
    masks = _block_masks()
    qb, kb, vb, zb = C_GQ // G_DK, C_GK // G_DK, C_GV // G_DV, C_GZ // G_DV
    return pl.pallas_call(
        _gdn_kernel,
        out_shape=jax.ShapeDtypeStruct((bsz, S_ALL, BRANCH_W), BF16),
        grid=(bsz, G_HEADS),
        in_specs=[pl.BlockSpec((1, S_ALL, G_DK), lambda b, h: (b, 0, qb + h)),
                  pl.BlockSpec((1, S_ALL, G_DK), lambda b, h: (b, 0, kb + h)),
                  pl.BlockSpec((1, S_ALL, G_DV), lambda b, h: (b, 0, vb + h)),
                  pl.BlockSpec((1, S_ALL, G_DV), lambda b, h: (b, 0, zb + h)),
                  pl.BlockSpec((1, 1, N_CHUNKS, GATE_ROWS, CHUNK), lambda b, h: (b, h, 0, 0, 0)),
                  pl.BlockSpec((1, 2, GATE_ROWS, CHUNK), lambda b, h: (h, 0, 0, 0)),
                  pl.BlockSpec((1, 8, 3 * G_DK), lambda b, h: (h, 0, 0)),
                  pl.BlockSpec((1, G_DV), lambda b, h: (0, 0)),
                  pl.BlockSpec((CHUNK, 3 * CHUNK), lambda b, h: (0, 0)),
                  pl.BlockSpec((CHUNK, CHUNK), lambda b, h: (0, 0)),
                  pl.BlockSpec(masks.shape, lambda b, h: (0, 0, 0))],
        out_specs=pl.BlockSpec((1, S_ALL, G_DV), lambda b, h: (b, 0, h)),
        scratch_shapes=[pltpu.VMEM((S_ALL, G_DK), BF16), pltpu.VMEM((S_ALL, G_DK), BF16),
                        pltpu.VMEM((S_ALL, G_DV), BF16),
                        pltpu.VMEM((2, S_ALL, G_DV), F32),
                        pltpu.VMEM((2, 2 * S_ALL, G_DK), BF16),
                        pltpu.VMEM((2, S_ALL, 2 * CHUNK), BF16),
                        pltpu.VMEM((2, S_ALL, G_DK), BF16),
                        pltpu.VMEM((2, N_CHUNKS, 8, G_DV), F32),
                        pltpu.VMEM((2, S_ALL, G_DV), F32)],
        compiler_params=_cparams(("arbitrary", "arbitrary")),
        name="gdn_mixer",
    )(big, big, big, big, gates, gp, cw, norm_g.reshape(1, G_DV), tri, eye, masks)


GRID_H = SEQ // GRID_W
LRU_BLK = 256


def _shift_rows(a, s):
    n = a.shape[0]
    r = lax.broadcasted_iota(jnp.int32, a.shape, 0)
    return jnp.where((r >= s) & (r < n + s), pltpu.roll(a, s % n, 0), 0.0)


def _gelu(v):
    return 0.5 * v * (1.0 + jnp.tanh(0.7978845608028654 * (v + 0.044715 * v * v * v)))


def _lru_kernel(x_ref, y_ref, cw_ref, cb_ref, gw_ref, gb_ref, lam_ref, o_ref,
                xb_s, a_s, u_s, h_s, p_s, end_s, cin_s):
    cw = cw_ref[...]
    xf = x_ref[0].astype(F32)
    xc = xf[:CTX_LEN]
    lat = lambda r0, r1: xf[CTX_LEN + r0 * GRID_W:CTX_LEN + r1 * GRID_W]
    prev1 = jnp.concatenate([_shift_rows(lat(GRID_H - 1, GRID_H), 1), lat(0, GRID_H - 1)], axis=0)
    prev2 = jnp.concatenate([_shift_rows(lat(GRID_H - 2, GRID_H - 1), 1), _shift_rows(lat(GRID_H - 1, GRID_H), 1),
                             lat(0, GRID_H - 2)], axis=0)
    next1 = jnp.concatenate([lat(1, GRID_H), _shift_rows(lat(0, 1), -1)], axis=0)
    taps_c = (_shift_rows(xc, 2), _shift_rows(xc, 1), xc, _shift_rows(xc, -1))
    taps_l = (prev2, prev1, lat(0, GRID_H), next1)
    xb_s[:CTX_LEN, :] = sum(t * cw[j:j + 1, :] for j, t in enumerate(taps_c)) + cb_ref[...]
    xb_s[CTX_LEN:, :] = sum(t * cw[j:j + 1, :] for j, t in enumerate(taps_l)) + cb_ref[...]

    neg_c_sp = -LRU_C * _softplus(-lam_ref[...])

    def gates(i, carry):
        rows = pl.ds(pl.multiple_of(i * LRU_BLK, LRU_BLK), LRU_BLK)
        xb = xb_s[rows, :]
        pre = jnp.dot(xb.astype(BF16), gw_ref[0], preferred_element_type=F32) + gb_ref[0]
        for d in range(2):
            r = _sigmoid(pre[:, (2 * d) * LANE:(2 * d + 1) * LANE])
            g_in = _sigmoid(pre[:, (2 * d + 1) * LANE:(2 * d + 2) * LANE])
            log_a = neg_c_sp[d:d + 1, :] * r
            a = jnp.exp(log_a)
            a_s[d, rows, :] = a
            u_s[d, rows, :] = jnp.sqrt(jnp.maximum(1.0 - a * a, 0.0)) * g_in * xb
        return carry

    lax.fori_loop(0, S_ALL // LRU_BLK, gates, 0)

    def ctx_step(t, carry):
        hf, hb = carry
        tb = CTX_LEN - 1 - t
        hf = a_s[0, pl.ds(t, 1), :] * hf + u_s[0, pl.ds(t, 1), :]
        hb = a_s[1, pl.ds(tb, 1), :] * hb + u_s[1, pl.ds(tb, 1), :]
        h_s[0, pl.ds(t, 1), :] = hf
        h_s[1, pl.ds(tb, 1), :] = hb
        return hf, hb

    zero_row = jnp.zeros((1, LANE), F32)
    hf0, hb0 = lax.fori_loop(0, CTX_LEN, ctx_step, (zero_row, zero_row))

    def col_step(i, carry):
        hf, pf, hb, pb = carry
        rf = pl.ds(pl.multiple_of(CTX_LEN + i * GRID_W, GRID_W), GRID_W)
        rb = pl.ds(pl.multiple_of(CTX_LEN + (GRID_H - 1 - i) * GRID_W, GRID_W), GRID_W)
        af, ab = a_s[0, rf, :], a_s[1, rb, :]
        hf = af * hf + u_s[0, rf, :]
        pf = af * pf
        hb = ab * hb + u_s[1, rb, :]
        pb = ab * pb
        h_s[0, rf, :] = hf
        p_s[0, pl.ds(pl.multiple_of(i * GRID_W, GRID_W), GRID_W), :] = pf
        h_s[1, rb, :] = hb
        p_s[1, pl.ds(pl.multiple_of((GRID_H - 1 - i) * GRID_W, GRID_W), GRID_W), :] = pb
        return hf, pf, hb, pb

    zeros = jnp.zeros((GRID_W, LANE), F32)
    ones = jnp.ones((GRID_W, LANE), F32)
    hf, pf, hb, pb = lax.fori_loop(0, GRID_H, col_step, (zeros, ones, zeros, ones))
    end_s[0], end_s[1], end_s[2], end_s[3] = hf, pf, hb, pb

    def chain_step(w, carry):
        cf, cb = carry
        wb = GRID_W - 1 - w
        cin_s[0, pl.ds(w, 1), :] = cf
        cin_s[1, pl.ds(wb, 1), :] = cb
        cf = end_s[0, pl.ds(w, 1), :] + end_s[1, pl.ds(w, 1), :] * cf
        cb = end_s[2, pl.ds(wb, 1), :] + end_s[3, pl.ds(wb, 1), :] * cb
        return cf, cb

    lax.fori_loop(0, GRID_W, chain_step, (hf0, hb0))

    o_ref[0, :CTX_LEN, :] = ((h_s[0, :CTX_LEN, :] + h_s[1, :CTX_LEN, :])
                             * _gelu(y_ref[0, :CTX_LEN, :].astype(F32))).astype(o_ref.dtype)

    def out_step(i, carry):
        rows = pl.ds(pl.multiple_of(CTX_LEN + i * GRID_W, GRID_W), GRID_W)
        prow = pl.ds(pl.multiple_of(i * GRID_W, GRID_W), GRID_W)
        h = (h_s[0, rows, :] + p_s[0, prow, :] * cin_s[0] + h_s[1, rows, :] + p_s[1, prow, :] * cin_s[1])
        o_ref[0, rows, :] = (h * _gelu(y_ref[0, rows, :].astype(F32))).astype(o_ref.dtype)
        return carry

    lax.fori_loop(0, GRID_H, out_step, 0)


def lru_mixer(big, conv_w, conv_b, gate_w, gate_b, lam):
    bsz = big.shape[0]
    gw = gate_w.transpose(2, 3, 0, 1, 4).reshape(R_BLOCKS, R_BDIM, 4 * R_BDIM).astype(BF16)
    gb = gate_b.reshape(2, 2, R_BLOCKS, R_BDIM).transpose(2, 0, 1, 3).reshape(R_BLOCKS, 1, 4 * R_BDIM)
    cw = jnp.pad(conv_w, ((0, 8 - CONV_W), (0, 0)))
    xb, yb = C_RX // R_BDIM, C_RY // R_BDIM
    return pl.pallas_call(
        _lru_kernel,
        out_shape=jax.ShapeDtypeStruct((bsz, S_ALL, BRANCH_W), BF16),
        grid=(bsz, R_BLOCKS),
        in_specs=[pl.BlockSpec((1, S_ALL, R_BDIM), lambda b, n: (b, 0, xb + n)),
                  pl.BlockSpec((1, S_ALL, R_BDIM), lambda b, n: (b, 0, yb + n)),
                  pl.BlockSpec((8, R_BDIM), lambda b, n: (0, n)),
                  pl.BlockSpec((1, R_BDIM), lambda b, n: (0, n)),
                  pl.BlockSpec((1, R_BDIM, 4 * R_BDIM), lambda b, n: (n, 0, 0)),
                  pl.BlockSpec((1, 1, 4 * R_BDIM), lambda b, n: (n, 0, 0)),
                  pl.BlockSpec((2, R_BDIM), lambda b, n: (0, n))],
        out_specs=pl.BlockSpec((1, S_ALL, R_BDIM), lambda b, n: (b, 0, n)),
        scratch_shapes=[pltpu.VMEM((S_ALL, R_BDIM), F32), pltpu.VMEM((2, S_ALL, R_BDIM), F32),
                        pltpu.VMEM((2, S_ALL, R_BDIM), F32), pltpu.VMEM((2, S_ALL, R_BDIM), F32),
                        pltpu.VMEM((2, SEQ, R_BDIM), F32), pltpu.VMEM((4, GRID_W, R_BDIM), F32),
                        pltpu.VMEM((2, GRID_W, R_BDIM), F32)],
        compiler_params=_cparams(("arbitrary", "arbitrary")),
        name="lru_mixer",
    )(big, big, cw, conv_b.reshape(1, BRANCH_W), gw, gb, lam)


def _pack_in_weights(w, b):
    segs_w = [w[:, IN_OFFS[i]:IN_OFFS[i + 1]] for i in range(len(IN_COLS))]
    segs_b = [b[IN_OFFS[i]:IN_OFFS[i + 1]] for i in range(len(IN_COLS))]
    w_big = jnp.concatenate([segs_w[i] for i in BIG_SEGS], axis=1).astype(BF16)
    b_big = jnp.concatenate([segs_b[i] for i in BIG_SEGS]).reshape(1, N_BIG)
    pad = N_SMALL - sum(IN_COLS[i] for i in SMALL_SEGS)
    w_small = jnp.pad(jnp.concatenate([segs_w[i] for i in SMALL_SEGS], axis=1), ((0, 0), (0, pad))).astype(BF16)
    b_small = jnp.pad(jnp.concatenate([segs_b[i] for i in SMALL_SEGS]), (0, pad)).reshape(1, N_SMALL)
    return w_big, b_big, w_small, b_small


def kernel(x, c, ctx, c_ctx, mod_w, mod_b, norm1_g, norm2_g, final_g, w_in, b_in, m_fbias, m_norm_g,
           g_conv, g_a_log, g_dt_bias, g_norm_g, r_conv, r_conv_b, r_gate_w, r_gate_b, r_lambda,
           w_branch, w_out, ffn_w1, ffn_w3, ffn_w2, router_w, router_b, moe_w1, moe_w3, moe_w2):
    bsz = x.shape[0]
    xs = jnp.concatenate([ctx, x], axis=1)
    src = jnp.concatenate([c, c_ctx[None]], axis=0)
    mods = modulation(src, mod_w, mod_b)
    for layer in range(DEPTH):
        mod = mods[layer]
        w_big, b_big, w_small, b_small = _pack_in_weights(w_in[layer], b_in[layer])
        big, small = in_projection(xs, norm1_g[layer], mod, w_big, b_big, w_small, b_small)
        ym = mlstm_mixer(big, small, m_fbias[layer], m_norm_g[layer])
        yg = gdn_mixer(big, small, g_conv[layer], g_a_log[layer], g_dt_bias[layer], g_norm_g[layer])
        yr = lru_mixer(big, r_conv[layer], r_conv_b[layer], r_gate_w[layer], r_gate_b[layer], r_lambda[layer])
        xs = merge_branches(ym, yg, yr, big, xs, mod, w_branch[layer].astype(BF16), w_out[layer].astype(BF16))
        j = layer // 2
        if layer % 2 == 0:
            xs = dense_ffn(xs, norm2_g[layer], mod, ffn_w1[j].astype(BF16), ffn_w3[j].astype(BF16),
                           ffn_w2[j].astype(BF16))
        else:
            gt2 = jnp.concatenate([jnp.broadcast_to(mod[bsz:, None, 5 * D_MODEL:], (bsz, CTX_LEN, D_MODEL)),
                                   jnp.broadcast_to(mod[:bsz, None, 5 * D_MODEL:], (bsz, SEQ, D_MODEL))], axis=1)
            xs = moe_ffn(xs, norm2_g[layer], mod, gt2, router_w[j], router_b[j], moe_w1[j].astype(BF16),
                         moe_w3[j].astype(BF16), moe_w2[j].astype(BF16))
    return final_norm(xs, final_g)
```

```python
import jax
import jax.numpy as jnp
import numpy as np
from jax import lax
from jax.experimental import pallas as pl
from jax.experimental.pallas import tpu as pltpu

D_MODEL = 1024
SEQ = 2048
DEPTH = 4
GRID_W = 64
CTX_LEN = 256
S_ALL = CTX_LEN + SEQ
BRANCH_W = D_MODEL
N_BRANCH = 3
CHUNK = 64
CONV_W = 4
M_HEADS = 4
M_DQK = 128
M_DV = BRANCH_W // M_HEADS
G_HEADS = 8
G_DK = 128
G_DV = BRANCH_W // G_HEADS
R_BLOCKS = 8
R_BDIM = BRANCH_W // R_BLOCKS
LRU_C = 8.0
D_FF = 2816
N_EXPERTS = 8
TOP_K = 2
MOE_BLOCK = 512
EPS = 1e-6
IN_COLS = (M_HEADS * M_DQK, M_HEADS * M_DQK, BRANCH_W, BRANCH_W, 4 * M_HEADS,
           2 * G_HEADS * G_DK + BRANCH_W, BRANCH_W, 2 * G_HEADS, 2 * G_HEADS,
           BRANCH_W, BRANCH_W, N_BRANCH * D_MODEL)
IN_OFFS = tuple(int(v) for v in np.cumsum((0,) + IN_COLS))
SMALL_SEGS = (4, 7, 8)
BIG_SEGS = tuple(i for i in range(len(IN_COLS)) if i not in SMALL_SEGS)
N_BIG = sum(IN_COLS[i] for i in BIG_SEGS)
N_SMALL = 128
C_MQ, C_MK, C_MV, C_MO = 0, 512, 1024, 2048
C_GQ, C_GK, C_GV, C_GZ = 3072, 4096, 5120, 6144
C_RX, C_RY, C_MG = 7168, 8192, 9216

LANE = 128
VMEM_LIMIT = 56 * 1024 * 1024
F32 = jnp.float32
BF16 = jnp.bfloat16


def _cparams(sem):
    return pltpu.CompilerParams(dimension_semantics=sem, vmem_limit_bytes=VMEM_LIMIT)


def _sigmoid(v):
    return 0.5 * jnp.tanh(0.5 * v) + 0.5


def _ada_norm(x, g, mod_l, mod_c, first_row, which):
    y = x * lax.rsqrt(jnp.mean(x * x, axis=-1, keepdims=True) + EPS) * g
    sh_l, sc_l = _mod_slice(mod_l, 3 * which), _mod_slice(mod_l, 3 * which + 1)
    sh_c, sc_c = _mod_slice(mod_c, 3 * which), _mod_slice(mod_c, 3 * which + 1)
    is_ctx = _is_ctx(x.shape[0], first_row)
    return y * (1.0 + jnp.where(is_ctx, sc_c, sc_l)) + jnp.where(is_ctx, sh_c, sh_l)


def _mod_slice(mod, k):
    return mod[:, k * D_MODEL:(k + 1) * D_MODEL]


def _is_ctx(rows, first_row):
    return (lax.broadcasted_iota(jnp.int32, (rows, 1), 0) + first_row) < CTX_LEN


def _mod_kernel(src_ref, w_ref, b_ref, o_ref):
    s = src_ref[...]
    s = (s * _sigmoid(s)).astype(BF16)
    o_ref[0] = jnp.dot(s, w_ref[0].astype(BF16), preferred_element_type=F32) + b_ref[0]


def modulation(src, mod_w, mod_b):
    rows = src.shape[0]
    tn = 1536
    return pl.pallas_call(
        _mod_kernel,
        out_shape=jax.ShapeDtypeStruct((DEPTH, rows, 6 * D_MODEL), F32),
        grid=(DEPTH, 6 * D_MODEL // tn),
        in_specs=[pl.BlockSpec((rows, D_MODEL), lambda l, n: (0, 0)),
                  pl.BlockSpec((1, D_MODEL, tn), lambda l, n: (l, 0, n)),
                  pl.BlockSpec((1, 1, tn), lambda l, n: (l, 0, n))],
        out_specs=pl.BlockSpec((1, rows, tn), lambda l, n: (l, 0, n)),
        compiler_params=_cparams(("arbitrary", "arbitrary")),
        name="modulation",
    )(src, mod_w, mod_b.reshape(DEPTH, 1, 6 * D_MODEL))


IN_TM = 1152
IN_TN = 1024


def _in_kernel(x_ref, g_ref, ml_ref, mc_ref, w_ref, b_ref, ws_ref, bs_ref, o_ref, os_ref, h_ref):
    n = pl.program_id(2)

    @pl.when(n == 0)
    def _():
        h = _ada_norm(x_ref[0], g_ref[...], ml_ref[0], mc_ref[0], pl.program_id(1) * IN_TM, 0)
        h_ref[...] = h.astype(BF16)
        os_ref[0] = jnp.dot(h_ref[...], ws_ref[...], preferred_element_type=F32) + bs_ref[...]

    o_ref[0] = (jnp.dot(h_ref[...], w_ref[...], preferred_element_type=F32) + b_ref[...]).astype(o_ref.dtype)


def in_projection(xs, g, mod, w_big, b_big, w_small, b_small):
    bsz = xs.shape[0]
    mod3 = mod.reshape(bsz + 1, 1, 6 * D_MODEL)
    return pl.pallas_call(
        _in_kernel,
        out_shape=(jax.ShapeDtypeStruct((bsz, S_ALL, N_BIG), BF16),
                   jax.ShapeDtypeStruct((bsz, S_ALL, N_SMALL), F32)),
        grid=(bsz, S_ALL // IN_TM, N_BIG // IN_TN),
        in_specs=[pl.BlockSpec((1, IN_TM, D_MODEL), lambda b, s, n: (b, s, 0)),
                  pl.BlockSpec((1, D_MODEL), lambda b, s, n: (0, 0)),
                  pl.BlockSpec((1, 1, 6 * D_MODEL), lambda b, s, n: (b, 0, 0)),
                  pl.BlockSpec((1, 1, 6 * D_MODEL), lambda b, s, n: (bsz, 0, 0)),
                  pl.BlockSpec((D_MODEL, IN_TN), lambda b, s, n: (0, n)),
                  pl.BlockSpec((1, IN_TN), lambda b, s, n: (0, n)),
                  pl.BlockSpec((D_MODEL, N_SMALL), lambda b, s, n: (0, 0)),
                  pl.BlockSpec((1, N_SMALL), lambda b, s, n: (0, 0))],
        out_specs=(pl.BlockSpec((1, IN_TM, IN_TN), lambda b, s, n: (b, s, n)),
                   pl.BlockSpec((1, IN_TM, N_SMALL), lambda b, s, n: (b, s, 0))),
        scratch_shapes=[pltpu.VMEM((IN_TM, D_MODEL), BF16)],
        compiler_params=_cparams(("arbitrary", "arbitrary", "arbitrary")),
        name="in_projection",
    )(xs, g.reshape(1, D_MODEL), mod3, mod3, w_big, b_big, w_small, b_small)


MERGE_TM = 768


def _merge_kernel(ym_ref, yg_ref, yr_ref, g0_ref, g1_ref, g2_ref, x_ref, ml_ref, mc_ref, wb_ref, wo_ref, o_ref):
    acc = None
    for n, (y_ref, gp_ref) in enumerate(((ym_ref, g0_ref), (yg_ref, g1_ref), (yr_ref, g2_ref))):
        p = jnp.dot(y_ref[0].astype(BF16), wb_ref[n], preferred_element_type=F32)
        p = p * _sigmoid(gp_ref[0].astype(F32))
        acc = p if acc is None else acc + p
    out = jnp.dot(acc.astype(BF16), wo_ref[...], preferred_element_type=F32)
    is_ctx = _is_ctx(MERGE_TM, pl.program_id(1) * MERGE_TM)
    gt = jnp.where(is_ctx, _mod_slice(mc_ref[0], 2), _mod_slice(ml_ref[0], 2))
    o_ref[0] = x_ref[0] + gt * out


def merge_branches(ym, yg, yr, big, xs, mod, w_branch, w_out):
    bsz = xs.shape[0]
    mod3 = mod.reshape(bsz + 1, 1, 6 * D_MODEL)
    row = lambda b, s: (b, s, 0)
    gate_blk = C_MG // D_MODEL
    const = pl.Buffered(1)
    return pl.pallas_call(
        _merge_kernel,
        out_shape=jax.ShapeDtypeStruct(xs.shape, F32),
        grid=(bsz, S_ALL // MERGE_TM),
        in_specs=[pl.BlockSpec((1, MERGE_TM, D_MODEL), row),
                  pl.BlockSpec((1, MERGE_TM, D_MODEL), row),
                  pl.BlockSpec((1, MERGE_TM, D_MODEL), row),
                  pl.BlockSpec((1, MERGE_TM, D_MODEL), lambda b, s: (b, s, gate_blk)),
                  pl.BlockSpec((1, MERGE_TM, D_MODEL), lambda b, s: (b, s, gate_blk + 1)),
                  pl.BlockSpec((1, MERGE_TM, D_MODEL), lambda b, s: (b, s, gate_blk + 2)),
                  pl.BlockSpec((1, MERGE_TM, D_MODEL), row),
                  pl.BlockSpec((1, 1, 6 * D_MODEL), lambda b, s: (b, 0, 0)),
                  pl.BlockSpec((1, 1, 6 * D_MODEL), lambda b, s: (bsz, 0, 0)),
                  pl.BlockSpec((N_BRANCH, BRANCH_W, D_MODEL), lambda b, s: (0, 0, 0), pipeline_mode=const),
                  pl.BlockSpec((D_MODEL, D_MODEL), lambda b, s: (0, 0), pipeline_mode=const)],
        out_specs=pl.BlockSpec((1, MERGE_TM, D_MODEL), row),
        compiler_params=_cparams(("arbitrary", "arbitrary")),
        name="merge_branches",
    )(ym, yg, yr, big, big, big, xs, mod3, mod3, w_branch, w_out)


FFN_TM = 768
FFN_TF = 256


def _swiglu_chunk(h, w1, w3, w2):
    a = jnp.dot(h, w1, preferred_element_type=F32)
    b = jnp.dot(h, w3, preferred_element_type=F32)
    return jnp.dot((a * _sigmoid(a) * b).astype(BF16), w2, preferred_element_type=F32)


def _ffn_kernel(x_ref, g_ref, ml_ref, mc_ref, w1_ref, w3_ref, w2_ref, o_ref):
    first_row = pl.program_id(1) * FFN_TM
    h = _ada_norm(x_ref[0], g_ref[...], ml_ref[0], mc_ref[0], first_row, 1).astype(BF16)
    for f in range(D_FF // FFN_TF):
        cols = slice(f * FFN_TF, (f + 1) * FFN_TF)
        y = _swiglu_chunk(h, w1_ref[:, cols], w3_ref[:, cols], w2_ref[cols, :])
        if f == 0:
            o_ref[0] = y
        else:
            o_ref[0] += y
    gt = jnp.where(_is_ctx(FFN_TM, first_row), _mod_slice(mc_ref[0], 5), _mod_slice(ml_ref[0], 5))
    o_ref[0] = x_ref[0] + gt * o_ref[0]


def dense_ffn(xs, g, mod, w1, w3, w2):
    bsz = xs.shape[0]
    mod3 = mod.reshape(bsz + 1, 1, 6 * D_MODEL)
    const = pl.Buffered(1)
    return pl.pallas_call(
        _ffn_kernel,
        out_shape=jax.ShapeDtypeStruct(xs.shape, F32),
        grid=(bsz, S_ALL // FFN_TM),
        in_specs=[pl.BlockSpec((1, FFN_TM, D_MODEL), lambda b, s: (b, s, 0)),
                  pl.BlockSpec((1, D_MODEL), lambda b, s: (0, 0)),
                  pl.BlockSpec((1, 1, 6 * D_MODEL), lambda b, s: (b, 0, 0)),
                  pl.BlockSpec((1, 1, 6 * D_MODEL), lambda b, s: (bsz, 0, 0)),
                  pl.BlockSpec((D_MODEL, D_FF), lambda b, s: (0, 0), pipeline_mode=const),
                  pl.BlockSpec((D_MODEL, D_FF), lambda b, s: (0, 0), pipeline_mode=const),
                  pl.BlockSpec((D_FF, D_MODEL), lambda b, s: (0, 0), pipeline_mode=const)],
        out_specs=pl.BlockSpec((1, FFN_TM, D_MODEL), lambda b, s: (b, s, 0)),
        compiler_params=_cparams(("arbitrary", "arbitrary")),
        name="dense_ffn",
    )(xs, g.reshape(1, D_MODEL), mod3, mod3, w1, w3, w2)


NORM_TM = 768


def _norm_router_kernel(x_ref, g_ref, ml_ref, mc_ref, rw_ref, rb_ref, h_ref, lg_ref):
    h = _ada_norm(x_ref[0], g_ref[...], ml_ref[0], mc_ref[0], pl.program_id(1) * NORM_TM, 1)
    h_ref[0] = h.astype(h_ref.dtype)
    lg_ref[0] = jnp.dot(h, rw_ref[...], preferred_element_type=F32, precision=lax.Precision.HIGHEST) + rb_ref[...]


def norm_router(xs, g, mod, router_w, router_b):
    bsz = xs.shape[0]
    mod3 = mod.reshape(bsz + 1, 1, 6 * D_MODEL)
    rw = jnp.zeros((D_MODEL, LANE), F32).at[:, :N_EXPERTS].set(router_w)
    rb = jnp.zeros((1, LANE), F32).at[0, :N_EXPERTS].set(router_b)
    return pl.pallas_call(
        _norm_router_kernel,
        out_shape=(jax.ShapeDtypeStruct(xs.shape, BF16), jax.ShapeDtypeStruct((bsz, S_ALL, LANE), F32)),
        grid=(bsz, S_ALL // NORM_TM),
        in_specs=[pl.BlockSpec((1, NORM_TM, D_MODEL), lambda b, s: (b, s, 0)),
                  pl.BlockSpec((1, D_MODEL), lambda b, s: (0, 0)),
                  pl.BlockSpec((1, 1, 6 * D_MODEL), lambda b, s: (b, 0, 0)),
                  pl.BlockSpec((1, 1, 6 * D_MODEL), lambda b, s: (bsz, 0, 0)),
                  pl.BlockSpec((D_MODEL, LANE), lambda b, s: (0, 0)),
                  pl.BlockSpec((1, LANE), lambda b, s: (0, 0))],
        out_specs=(pl.BlockSpec((1, NORM_TM, D_MODEL), lambda b, s: (b, s, 0)),
                   pl.BlockSpec((1, NORM_TM, LANE), lambda b, s: (b, s, 0))),
        compiler_params=_cparams(("arbitrary", "arbitrary")),
        name="norm_router",
    )(xs, g.reshape(1, D_MODEL), mod3, mod3, rw, rb)


def _expert_kernel(be_ref, x_ref, w1_ref, w3_ref, w2_ref, o_ref):
    del be_ref
    x = x_ref[...]
    for f in range(D_FF // FFN_TF):
        cols = slice(f * FFN_TF, (f + 1) * FFN_TF)
        y = _swiglu_chunk(x, w1_ref[0, :, cols], w3_ref[0, :, cols], w2_ref[0, cols, :])
        if f == 0:
            o_ref[...] = y
        else:
            o_ref[...] += y


def grouped_experts(xb, block_e, w1, w3, w2):
    n_rows = xb.shape[0]
    const = pl.Buffered(1)
    grid_spec = pltpu.PrefetchScalarGridSpec(
        num_scalar_prefetch=1,
        grid=(n_rows // MOE_BLOCK,),
        in_specs=[pl.BlockSpec((MOE_BLOCK, D_MODEL), lambda i, be: (i, 0)),
                  pl.BlockSpec((1, D_MODEL, D_FF), lambda i, be: (be[i], 0, 0), pipeline_mode=const),
                  pl.BlockSpec((1, D_MODEL, D_FF), lambda i, be: (be[i], 0, 0), pipeline_mode=const),
                  pl.BlockSpec((1, D_FF, D_MODEL), lambda i, be: (be[i], 0, 0), pipeline_mode=const)],
        out_specs=pl.BlockSpec((MOE_BLOCK, D_MODEL), lambda i, be: (i, 0)),
    )
    return pl.pallas_call(
        _expert_kernel,
        out_shape=jax.ShapeDtypeStruct((n_rows, D_MODEL), F32),
        grid_spec=grid_spec,
        compiler_params=_cparams(("arbitrary",)),
        name="grouped_experts",
    )(block_e, xb, w1, w3, w2)


def moe_ffn(xs, g, mod, gt2, router_w, router_b, w1, w3, w2):
    bsz = xs.shape[0]
    h, logits = norm_router(xs, g, mod, router_w, router_b)
    h = h.reshape(-1, D_MODEL)
    n_tok = h.shape[0]
    top_logit, top_e = lax.top_k(logits.reshape(n_tok, LANE)[:, :N_EXPERTS], TOP_K)
    gate = jax.nn.softmax(top_logit, axis=-1)
    n_assign = n_tok * TOP_K
    flat_e = top_e.reshape(-1)
    flat_tok = jnp.repeat(jnp.arange(n_tok, dtype=jnp.int32), TOP_K)
    onehot = (flat_e[:, None] == jnp.arange(N_EXPERTS, dtype=flat_e.dtype)[None, :]).astype(jnp.int32)
    seen = jnp.cumsum(onehot, axis=0)
    padded = (seen[-1] + MOE_BLOCK - 1) // MOE_BLOCK * MOE_BLOCK
    p_end = jnp.cumsum(padded)
    slot = jnp.sum(onehot * ((p_end - padded)[None, :] + seen - 1), axis=1)
    n_blocks = -(-n_assign // MOE_BLOCK) + N_EXPERTS
    n_rows = n_blocks * MOE_BLOCK
    row_tok = jnp.zeros((n_rows,), jnp.int32).at[slot].set(flat_tok)
    slot = slot.reshape(n_tok, TOP_K)
    block_e = jnp.minimum(jnp.searchsorted(p_end, jnp.arange(n_blocks) * MOE_BLOCK, side='right'),
                          N_EXPERTS - 1).astype(jnp.int32)
    yb = grouped_experts(h[row_tok], block_e, w1, w3, w2)
    y = sum(yb[slot[:, k]] * gate[:, k:k + 1] for k in range(TOP_K))
    return xs + gt2 * y.reshape(bsz, S_ALL, D_MODEL)


FINAL_TM = 256


def _final_kernel(x_ref, g_ref, o_ref):
    x = x_ref[0]
    o_ref[0] = x * lax.rsqrt(jnp.mean(x * x, axis=-1, keepdims=True) + EPS) * g_ref[...]


def final_norm(xs, g):
    bsz = xs.shape[0]
    skip = CTX_LEN // FINAL_TM
    return pl.pallas_call(
        _final_kernel,
        out_shape=jax.ShapeDtypeStruct((bsz, SEQ, D_MODEL), F32),
        grid=(bsz, SEQ // FINAL_TM),
        in_specs=[pl.BlockSpec((1, FINAL_TM, D_MODEL), lambda b, s: (b, s + skip, 0)),
                  pl.BlockSpec((1, D_MODEL), lambda b, s: (0, 0))],
        out_specs=pl.BlockSpec((1, FINAL_TM, D_MODEL), lambda b, s: (b, s, 0)),
        compiler_params=_cparams(("arbitrary", "arbitrary")),
        name="final_norm",
    )(xs, g.reshape(1, D_MODEL))


N_CHUNKS = S_ALL // CHUNK
CTX_CHUNKS = CTX_LEN // CHUNK
GATE_ROWS = 8
NEG_INF = float("-inf")


def _softplus(v):
    return jnp.maximum(v, 0.0) + jnp.log(1.0 + jnp.exp(-jnp.abs(v)))


def _scan_constants():
    s = np.arange(CHUNK)[:, None]
    t = np.arange(CHUNK)[None, :]
    tri = np.concatenate([(s <= t), (s >= t), np.ones((CHUNK, CHUNK), bool)], axis=1).astype(np.float32)
    return jnp.asarray(tri), jnp.asarray(np.eye(CHUNK, dtype=np.float32))


def _rev_chunk(j):
    return jnp.where(j < CTX_CHUNKS, CTX_CHUNKS - 1 - j, N_CHUNKS + CTX_CHUNKS - 1 - j)


def _head_gate_rows(small, c0, heads, kind_major):
    bsz = small.shape[0]
    a = small[:, :, c0:c0 + 4 * heads].reshape(bsz, N_CHUNKS, CHUNK, 2, 2, heads)
    a = a.transpose((0, 5, 1, 4, 3, 2) if kind_major else (0, 5, 1, 3, 4, 2))
    a = a.reshape(bsz, heads, N_CHUNKS, 4, CHUNK)
    return jnp.pad(a, ((0, 0), (0, 0), (0, 0), (0, GATE_ROWS - 4), (0, 0)))


def _tn_dot(a, b):
    return lax.dot_general(a, b, (((0,), (0,)), ((), ())), preferred_element_type=F32)


def _nt_dot(a, b, precision=None):
    return lax.dot_general(a, b, (((1,), (1,)), ((), ())), preferred_element_type=F32, precision=precision)


def _dir_mask(d):
    t = lax.broadcasted_iota(jnp.int32, (CHUNK, CHUNK), 0)
    s = lax.broadcasted_iota(jnp.int32, (CHUNK, CHUNK), 1)
    return (s <= t, s < t) if d == 0 else (s >= t, s > t)


def _split_dot(a, ones, dot_fn):
    b = ones.astype(BF16)
    hi = a.astype(BF16)
    lo = (a - hi.astype(F32)).astype(BF16)
    return dot_fn(hi, b, preferred_element_type=F32) + dot_fn(lo, b, preferred_element_type=F32)


def _gate_forms(vals, tri, eye):
    cum = _split_dot(vals, tri, jnp.dot)
    kind = lax.broadcasted_iota(jnp.int32, vals.shape, 0) & (GATE_ROWS - 1)
    rows = jnp.where(kind == 1, cum[:, :CHUNK], jnp.where(kind == 3, cum[:, CHUNK:2 * CHUNK], vals))
    cols = _split_dot(rows, eye, lambda a, b, **kw: _nt_dot(b, a))
    return rows, cols, cum[:, 2 * CHUNK:2 * CHUNK + 1]


def _x_row(rows, d, g=0):
    return rows[GATE_ROWS * g + 2 * d:GATE_ROWS * g + 2 * d + 1]


def _run_row(rows, d, g=0):
    return rows[GATE_ROWS * g + 2 * d + 1:GATE_ROWS * g + 2 * d + 2]


def _x_col(cols, d, g=0):
    return cols[:, GATE_ROWS * g + 2 * d:GATE_ROWS * g + 2 * d + 1]


def _run_col(cols, d, g=0):
    return cols[:, GATE_ROWS * g + 2 * d + 1:GATE_ROWS * g + 2 * d + 2]


PAIRS_PER_STEP = 3
OUT_PAIRS_PER_STEP = 1


def _mlstm_kernel(q_ref, k_ref, v_ref, o_ref, g_ref, fb_ref, ng_ref, tri_ref, eye_ref, y_ref,
                  c_s, st_s, gr_s, gc_s):
    tri, eye = tri_ref[...], eye_ref[...]
    fbias = fb_ref[0]
    scale = M_DQK ** -0.5
    kind = lax.broadcasted_iota(jnp.int32, (2 * GATE_ROWS, CHUNK), 0) & 1
    fbias2 = jnp.concatenate([fbias, fbias], axis=0)
    st_row = lax.broadcasted_iota(jnp.int32, (GATE_ROWS, M_DQK), 0)

    def pair_rows(c0):
        return pl.ds(pl.multiple_of(c0 * CHUNK, 2 * CHUNK), 2 * CHUNK)

    def state_rows(c):
        return pl.ds(pl.multiple_of(c * M_DQK, M_DQK), M_DQK)

    def pair_state_rows(c0):
        return pl.ds(pl.multiple_of(c0 * M_DQK, 2 * M_DQK), 2 * M_DQK)

    def local(it, carry):
        pairs = [it * PAIRS_PER_STEP + p for p in range(PAIRS_PER_STEP)]
        forms = []
        for g in pairs:
            raw = jnp.concatenate([g_ref[0, 0, 2 * g], g_ref[0, 0, 2 * g + 1]], axis=0)
            vals = jnp.where(kind == 1, -_softplus(-(raw + fbias2)), raw)
            forms.append(_gate_forms(vals, tri, eye))
        zero = jnp.zeros((CHUNK, 2 * M_DQK), F32)
        lhs_all = []
        for g, (rows, cols, totals) in zip(pairs, forms):
            gr_s[g] = rows
            gc_s[g] = cols
            kf2 = k_ref[0, pair_rows(2 * g), :].astype(F32)
            lhs = []
            for i in range(2):
                kf = kf2[i * CHUNK:(i + 1) * CHUNK]
                kws = []
                for d in range(2):
                    tot = totals[GATE_ROWS * i + 2 * d + 1:GATE_ROWS * i + 2 * d + 2]
                    m_loc = jnp.max(tot - _run_row(rows, d, i) + _x_row(rows, d, i), axis=-1, keepdims=True)
                    kw = kf * (jnp.exp(tot - _run_col(cols, d, i) + _x_col(cols, d, i) - m_loc) * scale)
                    kws.append(kw)
                    st_s[d, 2 * g + i] = jnp.where(st_row == 0, jnp.sum(kw, axis=0, keepdims=True),
                                                   jnp.where(st_row == 1, m_loc, tot))
                lhs.append(jnp.concatenate(kws + [zero] if i == 0 else [zero] + kws, axis=1))
            lhs_all.append(jnp.concatenate(lhs, axis=0).astype(BF16))
        for g, lhs in zip(pairs, lhs_all):
            c_loc = _tn_dot(lhs, v_ref[0, pair_rows(2 * g), :])
            for i in range(2):
                for d in range(2):
                    c_s[d, state_rows(2 * g + i), :] = c_loc[(2 * i + d) * M_DQK:(2 * i + d + 1) * M_DQK]
        return carry

    lax.fori_loop(0, N_CHUNKS // (2 * PAIRS_PER_STEP), local, 0)

    def scan(d):
        def body(j, carry):
            c_prev, n_prev, m_prev = carry
            c = j if d == 0 else _rev_chunk(j)
            st = st_s[d, c]
            n_loc, m_loc, tot = st[0:1], st[1:2, :1], st[2:3, :1]
            c_loc = c_s[d, state_rows(c), :]
            c_s[d, state_rows(c), :] = c_prev
            st_s[d, c] = jnp.where(st_row == 0, n_prev, m_prev)
            m_new = jnp.maximum(tot + m_prev, m_loc)
            s_prev = jnp.exp(tot + m_prev - m_new)
            s_loc = jnp.exp(m_loc - m_new)
            return s_prev * c_prev + s_loc * c_loc, s_prev * n_prev + s_loc * n_loc, m_new

        lax.fori_loop(0, N_CHUNKS, body, (jnp.zeros((M_DQK, M_DV), F32), jnp.zeros((1, M_DQK), F32),
                                          jnp.zeros((1, 1), F32)))

    scan(0)
    scan(1)

    def output(it, carry):
        pairs = [it * OUT_PAIRS_PER_STEP + p for p in range(OUT_PAIRS_PER_STEP)]
        qk_all = [_nt_dot(q_ref[0, pair_rows(2 * g), :], k_ref[0, pair_rows(2 * g), :]) * scale for g in pairs]
        z_intra = jnp.zeros((CHUNK, 2 * CHUNK), F32)
        z_inter = jnp.zeros((CHUNK, M_DQK), F32)
        operands = []
        for g, qk2 in zip(pairs, qk_all):
            c0 = 2 * g
            qf2, v2 = q_ref[0, pair_rows(c0), :].astype(F32), v_ref[0, pair_rows(c0), :]
            rows, cols = gr_s[g], gc_s[g]
            lhs_intra, lhs_inter, rhs_v = [], [], []
            for i in range(2):
                qk = qk2[i * CHUNK:(i + 1) * CHUNK, i * CHUNK:(i + 1) * CHUNK]
                qf = qf2[i * CHUNK:(i + 1) * CHUNK]
                intra, inter = [], []
                for d in range(2):
                    st = st_s[d, c0 + i]
                    b_c = _run_col(cols, d, i)
                    n0, m0 = st[0:1], st[1:2, :1]
                    incl, _ = _dir_mask(d)
                    d_log = jnp.where(incl, b_c - _run_row(rows, d, i) + _x_row(rows, d, i), NEG_INF)
                    m_inter = b_c + m0
                    m_comb = jnp.maximum(m_inter, jnp.max(d_log, axis=-1, keepdims=True))
                    s = qk * jnp.exp(d_log - m_comb)
                    e_inter = jnp.exp(m_inter - m_comb)
                    den = (jnp.sum(s, axis=-1, keepdims=True)
                           + e_inter * jnp.sum(qf * n0, axis=-1, keepdims=True))
                    inv = 1.0 / jnp.maximum(jnp.abs(den), jnp.exp(-m_comb))
                    intra.append(s * inv)
                    inter.append(qf * (e_inter * inv))
                pair = jnp.concatenate(intra, axis=1)
                lhs_intra.append(jnp.concatenate([pair, z_intra] if i == 0 else [z_intra, pair], axis=1))
                lhs_inter.append(jnp.concatenate([inter[0], z_inter, inter[1], z_inter] if i == 0
                                                 else [z_inter, inter[0], z_inter, inter[1]], axis=1))
                v = v2[i * CHUNK:(i + 1) * CHUNK]
                rhs_v += [v, v]
            c_in = jnp.concatenate([c_s[0, pair_state_rows(c0), :], c_s[1, pair_state_rows(c0), :]], axis=0)
            operands.append((jnp.concatenate(lhs_intra, axis=0).astype(BF16), jnp.concatenate(rhs_v, axis=0),
                             jnp.concatenate(lhs_inter, axis=0).astype(BF16), c_in.astype(BF16)))
        hs = [jnp.dot(li, rv, preferred_element_type=F32) + jnp.dot(le, ci, preferred_element_type=F32)
              for li, rv, le, ci in operands]
        for g, h in zip(pairs, hs):
            h = h * lax.rsqrt(jnp.mean(h * h, axis=-1, keepdims=True) + EPS) * ng_ref[...]
            y_ref[0, pair_rows(2 * g), :] = (h * _sigmoid(o_ref[0, pair_rows(2 * g), :].astype(F32))
                                             ).astype(y_ref.dtype)
        return carry

    lax.fori_loop(0, N_CHUNKS // (2 * OUT_PAIRS_PER_STEP), output, 0)


def mlstm_mixer(big, small, f_bias, norm_g):
    bsz = big.shape[0]
    gates = _head_gate_rows(small, 0, M_HEADS, False)
    fb = jnp.zeros((M_HEADS, GATE_ROWS, CHUNK), F32)
    fb = fb.at[:, 1].set(f_bias[0][:, None]).at[:, 3].set(f_bias[1][:, None])
    tri, eye = _scan_constants()
    qb, kb, vb, ob = C_MQ // M_DQK, C_MK // M_DQK, C_MV // M_DV, C_MO // M_DV
    return pl.pallas_call(
        _mlstm_kernel,
        out_shape=jax.ShapeDtypeStruct((bsz, S_ALL, BRANCH_W), BF16),
        grid=(bsz, M_HEADS),
        in_specs=[pl.BlockSpec((1, S_ALL, M_DQK), lambda b, h: (b, 0, qb + h)),
                  pl.BlockSpec((1, S_ALL, M_DQK), lambda b, h: (b, 0, kb + h)),
                  pl.BlockSpec((1, S_ALL, M_DV), lambda b, h: (b, 0, vb + h)),
                  pl.BlockSpec((1, S_ALL, M_DV), lambda b, h: (b, 0, ob + h)),
                  pl.BlockSpec((1, 1, N_CHUNKS, GATE_ROWS, CHUNK), lambda b, h: (b, h, 0, 0, 0)),
                  pl.BlockSpec((1, GATE_ROWS, CHUNK), lambda b, h: (h, 0, 0)),
                  pl.BlockSpec((1, M_DV), lambda b, h: (0, h)),
                  pl.BlockSpec((CHUNK, 3 * CHUNK), lambda b, h: (0, 0)),
                  pl.BlockSpec((CHUNK, CHUNK), lambda b, h: (0, 0))],
        out_specs=pl.BlockSpec((1, S_ALL, M_DV), lambda b, h: (b, 0, h)),
        scratch_shapes=[pltpu.VMEM((2, N_CHUNKS * M_DQK, M_DV), F32),
                        pltpu.VMEM((2, N_CHUNKS, GATE_ROWS, M_DQK), F32),
                        pltpu.VMEM((N_CHUNKS // 2, 2 * GATE_ROWS, CHUNK), F32),
                        pltpu.VMEM((N_CHUNKS // 2, CHUNK, 2 * GATE_ROWS), F32)],
        compiler_params=_cparams(("arbitrary", "arbitrary")),
        name="mlstm_mixer",
    )(big, big, big, big, gates, fb, norm_g.reshape(1, BRANCH_W), tri, eye)


CONV_BLK = 256
CONV_HALO = 16
SOLVE_BLK = 2
SOLVE_GROUP = 4
PREP_GROUPS = 3


def _conv_silu(src_ref, w, dst_ref, l2_scale):
    n_blk = S_ALL // CONV_BLK
    zeros = jnp.zeros((CONV_HALO, src_ref.shape[-1]), F32)
    for i in range(n_blk):
        lo = i * CONV_BLK
        first = i == 0 or lo == CTX_LEN
        last = lo + CONV_BLK in (CTX_LEN, S_ALL)
        body = src_ref[0, lo - (0 if first else CONV_HALO):lo + CONV_BLK + (0 if last else CONV_HALO), :]
        win = jnp.concatenate(([zeros] if first else []) + [body.astype(F32)] + ([zeros] if last else []), axis=0)
        n = win.shape[0]
        y = None
        for j in range(CONV_W):
            tap = win if j == CONV_W // 2 else pltpu.roll(win, (CONV_W // 2 - j) % n, 0)
            term = tap[CONV_HALO:CONV_HALO + CONV_BLK] * w[j:j + 1, :]
            y = term if y is None else y + term
        y = y * _sigmoid(y)
        if l2_scale is not None:
            y = y * (lax.rsqrt(jnp.sum(y * y, axis=-1, keepdims=True) + EPS) * l2_scale)
        dst_ref[lo:lo + CONV_BLK, :] = y.astype(dst_ref.dtype)


def _block_masks():
    n = SOLVE_GROUP * CHUNK
    t = np.arange(n)[:, None]
    s = np.arange(n)[None, :]
    masks = [(t // SOLVE_BLK) == (s // SOLVE_BLK)]
    k = SOLVE_BLK
    while k < CHUNK:
        masks.append(((t // (2 * k)) == (s // (2 * k))) & ((t // k) != (s // k)))
        k *= 2
    return jnp.asarray(np.stack(masks).astype(np.float32), BF16)


def _unit_triangular_inverse(systems, eye, masks):
    abs_ = [a.astype(BF16) for a in systems]
    ts = [eye - (ab * masks[0]).astype(F32) for ab in abs_]
    for lvl in range(1, masks.shape[0]):
        tbs = [t.astype(BF16) for t in ts]
        tls = [jnp.dot(tb, ab * masks[lvl], preferred_element_type=F32) for tb, ab in zip(tbs, abs_)]
        ts = [t - jnp.dot(tl.astype(BF16), tb, preferred_element_type=F32) for t, tl, tb in zip(ts, tls, tbs)]
    return ts


def _block_diag(blocks):
    n = len(blocks)
    z_half = jnp.zeros((CHUNK, CHUNK), F32)
    z_tile = jnp.zeros((CHUNK, 2 * CHUNK), F32)
    out = []
    for p, blk in enumerate(blocks):
        pair = jnp.concatenate([blk, z_half] if p % 2 == 0 else [z_half, blk], axis=1)
        out.append(jnp.concatenate([pair if t == p // 2 else z_tile for t in range(n // 2)], axis=1))
    return jnp.concatenate(out, axis=0)


def _gdn_kernel(xq_ref, xk_ref, xv_ref, z_ref, g_ref, gp_ref, cw_ref, ng_ref, tri_ref, eye_ref, bm_ref, y_ref,
                q_s, k_s, v_s, u_s, wq_s, at_s, kt_s, cd_s, acc_ref):
    tri, eye, masks = tri_ref[...], eye_ref[...], bm_ref[...]
    cw = cw_ref[0]
    _conv_silu(xq_ref, cw[:, :G_DK], q_s, G_DK ** -0.5)
    _conv_silu(xk_ref, cw[:, G_DK:2 * G_DK], k_s, 1.0)
    _conv_silu(xv_ref, cw[:, 2 * G_DK:], v_s, None)
    n_pair = SOLVE_GROUP // 2
    kind = lax.broadcasted_iota(jnp.int32, (n_pair * GATE_ROWS, CHUNK), 0) & 1
    neg_a = jnp.concatenate([gp_ref[0, 0]] * n_pair, axis=0)
    dt_bias = jnp.concatenate([gp_ref[0, 1]] * n_pair, axis=0)
    n_bd = SOLVE_GROUP * CHUNK
    eye_bd = (lax.broadcasted_iota(jnp.int32, (n_bd, n_bd), 0)
              == lax.broadcasted_iota(jnp.int32, (n_bd, n_bd), 1)).astype(F32)
    z_half = jnp.zeros((CHUNK, CHUNK), F32)

    def chunk_rows(c):
        return pl.ds(pl.multiple_of(c * CHUNK, CHUNK), CHUNK)

    def wq_rows(c):
        return pl.ds(pl.multiple_of(c * 2 * CHUNK, 2 * CHUNK), 2 * CHUNK)

    def prep(g, carry):
        systems, rhs_all, q_dec_all = [], [], []
        for sgrp in range(PREP_GROUPS):
            c0 = (g * PREP_GROUPS + sgrp) * n_pair
            raw = jnp.concatenate([g_ref[0, 0, c0 + i] for i in range(n_pair)], axis=0)
            vals = jnp.where(kind == 1, neg_a * _softplus(raw + dt_bias), _sigmoid(raw))
            g_rows, g_cols, totals = _gate_forms(vals, tri, eye)
            blocks, rhs, q_dec = [], [], []
            for i in range(n_pair):
                c = c0 + i
                q, k, v = q_s[chunk_rows(c), :], k_s[chunk_rows(c), :], v_s[chunk_rows(c), :]
                kf, vf, qf = k.astype(F32), v.astype(F32), q.astype(F32)
                kq = _nt_dot(jnp.concatenate([k, q], axis=0), k)
                kk, qk = kq[:CHUNK], kq[CHUNK:]
                for d in range(2):
                    gam_r, beta_c, gam_c = _run_row(g_rows, d, i), _x_col(g_cols, d, i), _run_col(g_cols, d, i)
                    tot = totals[GATE_ROWS * i + 2 * d + 1:GATE_ROWS * i + 2 * d + 2]
                    incl, strict = _dir_mask(d)
                    decay = jnp.exp(jnp.where(incl, gam_c - gam_r, NEG_INF))
                    blocks.append(jnp.where(strict, kk * beta_c * decay, 0.0))
                    e_gam = jnp.exp(gam_c)
                    rhs.append(jnp.concatenate([vf * beta_c, kf * (beta_c * e_gam)], axis=-1))
                    q_dec.append(qf * e_gam)
                    attn = qk * decay
                    at_s[d, chunk_rows(c), :] = jnp.concatenate([attn, z_half] if d == 0 else [z_half, attn],
                                                                axis=1).astype(BF16)
                    kt_s[d, chunk_rows(c), :] = (kf * jnp.exp(tot - gam_c)).astype(BF16)
                    cd_s[d, c] = jnp.broadcast_to(jnp.exp(tot), cd_s.shape[2:])
            systems.append(_block_diag(blocks))
            rhs_all.append(jnp.concatenate(rhs, axis=0).astype(BF16))
            q_dec_all.append(q_dec)
        t_invs = _unit_triangular_inverse(systems, eye_bd, masks)
        for sgrp in range(PREP_GROUPS):
            c0 = (g * PREP_GROUPS + sgrp) * n_pair
            x = jnp.dot(t_invs[sgrp].astype(BF16), rhs_all[sgrp], preferred_element_type=F32)
            for i in range(n_pair):
                for d in range(2):
                    p = 2 * i + d
                    xp = x[p * CHUNK:(p + 1) * CHUNK]
                    u_s[d, chunk_rows(c0 + i), :] = xp[:, :G_DV]
                    wq_s[d, wq_rows(c0 + i), :] = jnp.concatenate([xp[:, G_DV:], q_dec_all[sgrp][p]],
                                                                  axis=0).astype(BF16)
        return carry

    lax.fori_loop(0, N_CHUNKS // (n_pair * PREP_GROUPS), prep, 0)

    z_v = jnp.zeros((CHUNK, G_DV), BF16)

    def step(j, s_prev):
        cs = (j, _rev_chunk(j))
        ws_qs = [jnp.dot(wq_s[d, wq_rows(cs[d]), :], s_prev[d].astype(BF16), preferred_element_type=F32)
                 for d in range(2)]
        v_new = [(u_s[d, chunk_rows(cs[d]), :] - ws_qs[d][:CHUNK]).astype(BF16) for d in range(2)]
        o = [jnp.dot(at_s[d, chunk_rows(cs[d]), :],
                     jnp.concatenate([v_new[d], z_v] if d == 0 else [z_v, v_new[d]], axis=0),
                     preferred_element_type=F32) for d in range(2)]
        ktv = [_tn_dot(kt_s[d, chunk_rows(cs[d]), :], v_new[d]) for d in range(2)]
        for d in range(2):
            acc_ref[d, chunk_rows(cs[d]), :] = ws_qs[d][CHUNK:] + o[d]
        return tuple(cd_s[d, cs[d]][:1, :] * s_prev[d] + ktv[d] for d in range(2))

    s_zero = jnp.zeros((G_DK, G_DV), F32)
    lax.fori_loop(0, N_CHUNKS, step, (s_zero, s_zero))

    def finish(c, carry):
        rows = pl.ds(pl.multiple_of(c * CHUNK, CHUNK), CHUNK)
        h = acc_ref[0, rows, :] + acc_ref[1, rows, :]
        h = h * lax.rsqrt(jnp.mean(h * h, axis=-1, keepdims=True) + EPS) * ng_ref[...]
        z = z_ref[0, rows, :].astype(F32)
        y_ref[0, rows, :] = (h * (z * _sigmoid(z))).astype(y_ref.dtype)
        return carry

    lax.fori_loop(0, N_CHUNKS, finish, 0, unroll=4)


def gdn_mixer(big, small, conv_w, a_log, dt_bias, norm_g):
    bsz = big.shape[0]
    gates = _head_gate_rows(small, 4 * M_HEADS, G_HEADS, True)
    gp = jnp.zeros((G_HEADS, 2, GATE_ROWS, CHUNK), F32)
    for d in range(2):
        gp = gp.at[:, 0, 2 * d + 1].set(-jnp.exp(a_log[d])[:, None]).at[:, 1, 2 * d + 1].set(dt_bias[d][:, None])
    cw = conv_w.reshape(CONV_W, 3, G_HEADS, G_DK).transpose(2, 0, 1, 3).reshape(G_HEADS, CONV_W, 3 * G_DK)
    cw = jnp.pad(cw, ((0, 0), (0, 8 - CONV_W), (0, 0)))
    tri, eye = _scan_constants()
    masks = _block_masks()
    qb, kb, vb, zb = C_GQ // G_DK, C_GK // G_DK, C_GV // G_DV, C_GZ // G_DV
    return pl.pallas_call(
        _gdn_kernel,
        out_shape=jax.ShapeDtypeStruct((bsz, S_ALL, BRANCH_W), BF16),
        grid=(bsz, G_HEADS),
        in_specs=[pl.BlockSpec((1, S_ALL, G_DK), lambda b, h: (b, 0, qb + h)),
                  pl.BlockSpec((1, S_ALL, G_DK), lambda b, h: (b, 0, kb + h)),
                  pl.BlockSpec((1, S_ALL, G_DV), lambda b, h: (b, 0, vb + h)),
                  pl.BlockSpec((1, S_ALL, G_DV), lambda b, h: (b, 0, zb + h)),
                  pl.BlockSpec((1, 1, N_CHUNKS, GATE_ROWS, CHUNK), lambda b, h: (b, h, 0, 0, 0)),
                  pl.BlockSpec((1, 2, GATE_ROWS, CHUNK), lambda b, h: (h, 0, 0, 0)),
                  pl.BlockSpec((1, 8, 3 * G_DK), lambda b, h: (h, 0, 0)),
                  pl.BlockSpec((1, G_DV), lambda b, h: (0, 0)),
                  pl.BlockSpec((CHUNK, 3 * CHUNK), lambda b, h: (0, 0)),
                  pl.BlockSpec((CHUNK, CHUNK), lambda b, h: (0, 0)),
                  pl.BlockSpec(masks.shape, lambda b, h: (0, 0, 0))],
        out_specs=pl.BlockSpec((1, S_ALL, G_DV), lambda b, h: (b, 0, h)),
        scratch_shapes=[pltpu.VMEM((S_ALL, G_DK), BF16), pltpu.VMEM((S_ALL, G_DK), BF16),
                        pltpu.VMEM((S_ALL, G_DV), BF16),
                        pltpu.VMEM((2, S_ALL, G_DV), F32),
                        pltpu.VMEM((2, 2 * S_ALL, G_DK), BF16),
                        pltpu.VMEM((2, S_ALL, 2 * CHUNK), BF16),
                        pltpu.VMEM((2, S_ALL, G_DK), BF16),
                        pltpu.VMEM((2, N_CHUNKS, 8, G_DV), F32),
                        pltpu.VMEM((2, S_ALL, G_DV), F32)],
        compiler_params=_cparams(("arbitrary", "arbitrary")),
        name="gdn_mixer",
    )(big, big, big, big, gates, gp, cw, norm_g.reshape(1, G_DV), tri, eye, masks)


GRID_H = SEQ // GRID_W
LRU_BLK = 256


def _shift_rows(a, s):
    n = a.shape[0]
    r = lax.broadcasted_iota(jnp.int32, a.shape, 0)
    return jnp.where((r >= s) & (r < n + s), pltpu.roll(a, s % n, 0), 0.0)


def _gelu(v):
    return 0.5 * v * (1.0 + jnp.tanh(0.7978845608028654 * (v + 0.044715 * v * v * v)))


def _lru_kernel(x_ref, y_ref, cw_ref, cb_ref, gw_ref, gb_ref, lam_ref, o_ref,
                xb_s, a_s, u_s, h_s, p_s, end_s, cin_s):
    cw = cw_ref[...]
    xf = x_ref[0].astype(F32)
    xc = xf[:CTX_LEN]
    lat = lambda r0, r1: xf[CTX_LEN + r0 * GRID_W:CTX_LEN + r1 * GRID_W]
    prev1 = jnp.concatenate([_shift_rows(lat(GRID_H - 1, GRID_H), 1), lat(0, GRID_H - 1)], axis=0)
    prev2 = jnp.concatenate([_shift_rows(lat(GRID_H - 2, GRID_H - 1), 1), _shift_rows(lat(GRID_H - 1, GRID_H), 1),
                             lat(0, GRID_H - 2)], axis=0)
    next1 = jnp.concatenate([lat(1, GRID_H), _shift_rows(lat(0, 1), -1)], axis=0)
    taps_c = (_shift_rows(xc, 2), _shift_rows(xc, 1), xc, _shift_rows(xc, -1))
    taps_l = (prev2, prev1, lat(0, GRID_H), next1)
    xb_s[:CTX_LEN, :] = sum(t * cw[j:j + 1, :] for j, t in enumerate(taps_c)) + cb_ref[...]
    xb_s[CTX_LEN:, :] = sum(t * cw[j:j + 1, :] for j, t in enumerate(taps_l)) + cb_ref[...]

    neg_c_sp = -LRU_C * _softplus(-lam_ref[...])

    def gates(i, carry):
        rows = pl.ds(pl.multiple_of(i * LRU_BLK, LRU_BLK), LRU_BLK)
        xb = xb_s[rows, :]
        pre = jnp.dot(xb.astype(BF16), gw_ref[0], preferred_element_type=F32) + gb_ref[0]
        for d in range(2):
            r = _sigmoid(pre[:, (2 * d) * LANE:(2 * d + 1) * LANE])
            g_in = _sigmoid(pre[:, (2 * d + 1) * LANE:(2 * d + 2) * LANE])
            log_a = neg_c_sp[d:d + 1, :] * r
            a = jnp.exp(log_a)
            a_s[d, rows, :] = a
            u_s[d, rows, :] = jnp.sqrt(jnp.maximum(1.0 - a * a, 0.0)) * g_in * xb
        return carry

    lax.fori_loop(0, S_ALL // LRU_BLK, gates, 0)

    def ctx_step(t, carry):
        hf, hb = carry
        tb = CTX_LEN - 1 - t
        hf = a_s[0, pl.ds(t, 1), :] * hf + u_s[0, pl.ds(t, 1), :]
        hb = a_s[1, pl.ds(tb, 1), :] * hb + u_s[1, pl.ds(tb, 1), :]
        h_s[0, pl.ds(t, 1), :] = hf
        h_s[1, pl.ds(tb, 1), :] = hb
        return hf, hb

    zero_row = jnp.zeros((1, LANE), F32)
    hf0, hb0 = lax.fori_loop(0, CTX_LEN, ctx_step, (zero_row, zero_row), unroll=8)

    def col_step(i, carry):
        hf, pf, hb, pb = carry
        rf = pl.ds(pl.multiple_of(CTX_LEN + i * GRID_W, GRID_W), GRID_W)
        rb = pl.ds(pl.multiple_of(CTX_LEN + (GRID_H - 1 - i) * GRID_W, GRID_W), GRID_W)
        af, ab = a_s[0, rf, :], a_s[1, rb, :]
        hf = af * hf + u_s[0, rf, :]
        pf = af * pf
        hb = ab * hb + u_s[1, rb, :]
        pb = ab * pb
        h_s[0, rf, :] = hf
        p_s[0, pl.ds(pl.multiple_of(i * GRID_W, GRID_W), GRID_W), :] = pf
        h_s[1, rb, :] = hb
        p_s[1, pl.ds(pl.multiple_of((GRID_H - 1 - i) * GRID_W, GRID_W), GRID_W), :] = pb
        return hf, pf, hb, pb

    zeros = jnp.zeros((GRID_W, LANE), F32)
    ones = jnp.ones((GRID_W, LANE), F32)
    hf, pf, hb, pb = lax.fori_loop(0, GRID_H, col_step, (zeros, ones, zeros, ones))
    end_s[0], end_s[1], end_s[2], end_s[3] = hf, pf, hb, pb

    def chain_step(w, carry):
        cf, cb = carry
        wb = GRID_W - 1 - w
        cin_s[0, pl.ds(w, 1), :] = cf
        cin_s[1, pl.ds(wb, 1), :] = cb
        cf = end_s[0, pl.ds(w, 1), :] + end_s[1, pl.ds(w, 1), :] * cf
        cb = end_s[2, pl.ds(wb, 1), :] + end_s[3, pl.ds(wb, 1), :] * cb
        return cf, cb

    lax.fori_loop(0, GRID_W, chain_step, (hf0, hb0), unroll=8)

    o_ref[0, :CTX_LEN, :] = ((h_s[0, :CTX_LEN, :] + h_s[1, :CTX_LEN, :])
                             * _gelu(y_ref[0, :CTX_LEN, :].astype(F32))).astype(o_ref.dtype)

    def out_step(i, carry):
        rows = pl.ds(pl.multiple_of(CTX_LEN + i * GRID_W, GRID_W), GRID_W)
        prow = pl.ds(pl.multiple_of(i * GRID_W, GRID_W), GRID_W)
        h = (h_s[0, rows, :] + p_s[0, prow, :] * cin_s[0] + h_s[1, rows, :] + p_s[1, prow, :] * cin_s[1])
        o_ref[0, rows, :] = (h * _gelu(y_ref[0, rows, :].astype(F32))).astype(o_ref.dtype)
        return carry

    lax.fori_loop(0, GRID_H, out_step, 0, unroll=2)


def lru_mixer(big, conv_w, conv_b, gate_w, gate_b, lam):
    bsz = big.shape[0]
    gw = gate_w.transpose(2, 3, 0, 1, 4).reshape(R_BLOCKS, R_BDIM, 4 * R_BDIM).astype(BF16)
    gb = gate_b.reshape(2, 2, R_BLOCKS, R_BDIM).transpose(2, 0, 1, 3).reshape(R_BLOCKS, 1, 4 * R_BDIM)
    cw = jnp.pad(conv_w, ((0, 8 - CONV_W), (0, 0)))
    xb, yb = C_RX // R_BDIM, C_RY // R_BDIM
    return pl.pallas_call(
        _lru_kernel,
        out_shape=jax.ShapeDtypeStruct((bsz, S_ALL, BRANCH_W), BF16),
        grid=(bsz, R_BLOCKS),
        in_specs=[pl.BlockSpec((1, S_ALL, R_BDIM), lambda b, n: (b, 0, xb + n)),
                  pl.BlockSpec((1, S_ALL, R_BDIM), lambda b, n: (b, 0, yb + n)),
                  pl.BlockSpec((8, R_BDIM), lambda b, n: (0, n)),
                  pl.BlockSpec((1, R_BDIM), lambda b, n: (0, n)),
                  pl.BlockSpec((1, R_BDIM, 4 * R_BDIM), lambda b, n: (n, 0, 0)),
                  pl.BlockSpec((1, 1, 4 * R_BDIM), lambda b, n: (n, 0, 0)),
                  pl.BlockSpec((2, R_BDIM), lambda b, n: (0, n))],
        out_specs=pl.BlockSpec((1, S_ALL, R_BDIM), lambda b, n: (b, 0, n)),
        scratch_shapes=[pltpu.VMEM((S_ALL, R_BDIM), F32), pltpu.VMEM((2, S_ALL, R_BDIM), F32),
                        pltpu.VMEM((2, S_ALL, R_BDIM), F32), pltpu.VMEM((2, S_ALL, R_BDIM), F32),
                        pltpu.VMEM((2, SEQ, R_BDIM), F32), pltpu.VMEM((4, GRID_W, R_BDIM), F32),
                        pltpu.VMEM((2, GRID_W, R_BDIM), F32)],
        compiler_params=_cparams(("arbitrary", "arbitrary")),
        name="lru_mixer",
    )(big, big, cw, conv_b.reshape(1, BRANCH_W), gw, gb, lam)


def _pack_in_weights(w, b):
    segs_w = [w[:, IN_OFFS[i]:IN_OFFS[i + 1]] for i in range(len(IN_COLS))]
    segs_b = [b[IN_OFFS[i]:IN_OFFS[i + 1]] for i in range(len(IN_COLS))]
    w_big = jnp.concatenate([segs_w[i] for i in BIG_SEGS], axis=1).astype(BF16)
    b_big = jnp.concatenate([segs_b[i] for i in BIG_SEGS]).reshape(1, N_BIG)
    pad = N_SMALL - sum(IN_COLS[i] for i in SMALL_SEGS)
    w_small = jnp.pad(jnp.concatenate([segs_w[i] for i in SMALL_SEGS], axis=1), ((0, 0), (0, pad))).astype(BF16)
    b_small = jnp.pad(jnp.concatenate([segs_b[i] for i in SMALL_SEGS]), (0, pad)).reshape(1, N_SMALL)
    return w_big, b_big, w_small, b_small


def kernel(x, c, ctx, c_ctx, mod_w, mod_b, norm1_g, norm2_g, final_g, w_in, b_in, m_fbias, m_norm_g,
           g_conv, g_a_log, g_dt_bias, g_norm_g, r_conv, r_conv_b, r_gate_w, r_gate_b, r_lambda,
           w_branch, w_out, ffn_w1, ffn_w3, ffn_w2, router_w, router_b, moe_w1, moe_w3, moe_w2):
    bsz = x.shape[0]
    xs = jnp.concatenate([ctx, x], axis=1)
    src = jnp.concatenate([c, c_ctx[None]], axis=0)
    mods = modulation(src, mod_w, mod_b)
    for layer in range(DEPTH):
        mod = mods[layer]
        w_big, b_big, w_small, b_small = _pack_in_weights(w_in[layer], b_in[layer])
        big, small = in_projection(xs, norm1_g[layer], mod, w_big, b_big, w_small, b_small)
        ym = mlstm_mixer(big, small, m_fbias[layer], m_norm_g[layer])
        yg = gdn_mixer(big, small, g_conv[layer], g_a_log[layer], g_dt_bias[layer], g_norm_g[layer])
        yr = lru_mixer(big, r_conv[layer], r_conv_b[layer], r_gate_w[layer], r_gate_b[layer], r_lambda[layer])
        xs = merge_branches(ym, yg, yr, big, xs, mod, w_branch[layer].astype(BF16), w_out[layer].astype(BF16))
        j = layer // 2
        if layer % 2 == 0:
            xs = dense_ffn(xs, norm2_g[layer], mod, ffn_w1[j].astype(BF16), ffn_w3[j].astype(BF16),
                           ffn_w2[j].astype(BF16))
        else:
            gt2 = jnp.concatenate([jnp.broadcast_to(mod[bsz:, None, 5 * D_MODEL:], (bsz, CTX_LEN, D_MODEL)),
                                   jnp.broadcast_to(mod[:bsz, None, 5 * D_MODEL:], (bsz, SEQ, D_MODEL))], axis=1)
            xs = moe_ffn(xs, norm2_g[layer], mod, gt2, router_w[j], router_b[j], moe_w1[j].astype(BF16),
                         moe_w3[j].astype(BF16), moe_w2[j].astype(BF16))
    return final_norm(xs, final_g)
```

```python
import jax
import jax.numpy as jnp
import numpy as np
from jax import lax
from jax.experimental import pallas as pl
from jax.experimental.pallas import tpu as pltpu

D_MODEL = 1024
SEQ = 2048
DEPTH = 4
GRID_W = 64
CTX_LEN = 256
S_ALL = CTX_LEN + SEQ
BRANCH_W = D_MODEL
N_BRANCH = 3
CHUNK = 64
CONV_W = 4
M_HEADS = 4
M_DQK = 128
M_DV = BRANCH_W // M_HEADS
G_HEADS = 8
G_DK = 128
G_DV = BRANCH_W // G_HEADS
R_BLOCKS = 8
R_BDIM = BRANCH_W // R_BLOCKS
LRU_C = 8.0
D_FF = 2816
N_EXPERTS = 8
TOP_K = 2
MOE_BLOCK = 512
EPS = 1e-6
IN_COLS = (M_HEADS * M_DQK, M_HEADS * M_DQK, BRANCH_W, BRANCH_W, 4 * M_HEADS,
           2 * G_HEADS * G_DK + BRANCH_W, BRANCH_W, 2 * G_HEADS, 2 * G_HEADS,
           BRANCH_W, BRANCH_W, N_BRANCH * D_MODEL)
IN_OFFS = tuple(int(v) for v in np.cumsum((0,) + IN_COLS))
SMALL_SEGS = (4, 7, 8)
BIG_SEGS = tuple(i for i in range(len(IN_COLS)) if i not in SMALL_SEGS)
N_BIG = sum(IN_COLS[i] for i in BIG_SEGS)
N_SMALL = 128
C_MQ, C_MK, C_MV, C_MO = 0, 512, 1024, 2048
C_GQ, C_GK, C_GV, C_GZ = 3072, 4096, 5120, 6144
C_RX, C_RY, C_MG = 7168, 8192, 9216

LANE = 128
VMEM_LIMIT = 56 * 1024 * 1024
F32 = jnp.float32
BF16 = jnp.bfloat16


def _cparams(sem):
    return pltpu.CompilerParams(dimension_semantics=sem, vmem_limit_bytes=VMEM_LIMIT)


def _sigmoid(v):
    return 0.5 * jnp.tanh(0.5 * v) + 0.5


def _ada_norm(x, g, mod_l, mod_c, first_row, which):
    y = x * lax.rsqrt(jnp.mean(x * x, axis=-1, keepdims=True) + EPS) * g
    sh_l, sc_l = _mod_slice(mod_l, 3 * which), _mod_slice(mod_l, 3 * which + 1)
    sh_c, sc_c = _mod_slice(mod_c, 3 * which), _mod_slice(mod_c, 3 * which + 1)
    is_ctx = _is_ctx(x.shape[0], first_row)
    return y * (1.0 + jnp.where(is_ctx, sc_c, sc_l)) + jnp.where(is_ctx, sh_c, sh_l)


def _mod_slice(mod, k):
    return mod[:, k * D_MODEL:(k + 1) * D_MODEL]


def _is_ctx(rows, first_row):
    return (lax.broadcasted_iota(jnp.int32, (rows, 1), 0) + first_row) < CTX_LEN


def _mod_kernel(src_ref, w_ref, b_ref, o_ref):
    s = src_ref[...]
    s = (s * _sigmoid(s)).astype(BF16)
    o_ref[0] = jnp.dot(s, w_ref[0].astype(BF16), preferred_element_type=F32) + b_ref[0]


def modulation(src, mod_w, mod_b):
    rows = src.shape[0]
    tn = 1536
    return pl.pallas_call(
        _mod_kernel,
        out_shape=jax.ShapeDtypeStruct((DEPTH, rows, 6 * D_MODEL), F32),
        grid=(DEPTH, 6 * D_MODEL // tn),
        in_specs=[pl.BlockSpec((rows, D_MODEL), lambda l, n: (0, 0)),
                  pl.BlockSpec((1, D_MODEL, tn), lambda l, n: (l, 0, n)),
                  pl.BlockSpec((1, 1, tn), lambda l, n: (l, 0, n))],
        out_specs=pl.BlockSpec((1, rows, tn), lambda l, n: (l, 0, n)),
        compiler_params=_cparams(("arbitrary", "arbitrary")),
        name="modulation",
    )(src, mod_w, mod_b.reshape(DEPTH, 1, 6 * D_MODEL))


IN_TM = 1152
IN_TN = 1024


def _in_kernel(x_ref, g_ref, ml_ref, mc_ref, w_ref, b_ref, ws_ref, bs_ref, o_ref, os_ref, h_ref):
    n = pl.program_id(2)

    @pl.when(n == 0)
    def _():
        h = _ada_norm(x_ref[0], g_ref[...], ml_ref[0], mc_ref[0], pl.program_id(1) * IN_TM, 0)
        h_ref[...] = h.astype(BF16)
        os_ref[0] = jnp.dot(h_ref[...], ws_ref[...], preferred_element_type=F32) + bs_ref[...]

    o_ref[0] = (jnp.dot(h_ref[...], w_ref[...], preferred_element_type=F32) + b_ref[...]).astype(o_ref.dtype)


def in_projection(xs, g, mod, w_big, b_big, w_small, b_small):
    bsz = xs.shape[0]
    mod3 = mod.reshape(bsz + 1, 1, 6 * D_MODEL)
    return pl.pallas_call(
        _in_kernel,
        out_shape=(jax.ShapeDtypeStruct((bsz, S_ALL, N_BIG), BF16),
                   jax.ShapeDtypeStruct((bsz, S_ALL, N_SMALL), F32)),
        grid=(bsz, S_ALL // IN_TM, N_BIG // IN_TN),
        in_specs=[pl.BlockSpec((1, IN_TM, D_MODEL), lambda b, s, n: (b, s, 0)),
                  pl.BlockSpec((1, D_MODEL), lambda b, s, n: (0, 0)),
                  pl.BlockSpec((1, 1, 6 * D_MODEL), lambda b, s, n: (b, 0, 0)),
                  pl.BlockSpec((1, 1, 6 * D_MODEL), lambda b, s, n: (bsz, 0, 0)),
                  pl.BlockSpec((D_MODEL, IN_TN), lambda b, s, n: (0, n)),
                  pl.BlockSpec((1, IN_TN), lambda b, s, n: (0, n)),
                  pl.BlockSpec((D_MODEL, N_SMALL), lambda b, s, n: (0, 0)),
                  pl.BlockSpec((1, N_SMALL), lambda b, s, n: (0, 0))],
        out_specs=(pl.BlockSpec((1, IN_TM, IN_TN), lambda b, s, n: (b, s, n)),
                   pl.BlockSpec((1, IN_TM, N_SMALL), lambda b, s, n: (b, s, 0))),
        scratch_shapes=[pltpu.VMEM((IN_TM, D_MODEL), BF16)],
        compiler_params=_cparams(("arbitrary", "arbitrary", "arbitrary")),
        name="in_projection",
    )(xs, g.reshape(1, D_MODEL), mod3, mod3, w_big, b_big, w_small, b_small)


MERGE_TM = 768


def _merge_kernel(ym_ref, yg_ref, yr_ref, g0_ref, g1_ref, g2_ref, x_ref, ml_ref, mc_ref, wb_ref, wo_ref, o_ref):
    acc = None
    for n, (y_ref, gp_ref) in enumerate(((ym_ref, g0_ref), (yg_ref, g1_ref), (yr_ref, g2_ref))):
        p = jnp.dot(y_ref[0].astype(BF16), wb_ref[n], preferred_element_type=F32)
        p = p * _sigmoid(gp_ref[0].astype(F32))
        acc = p if acc is None else acc + p
    out = jnp.dot(acc.astype(BF16), wo_ref[...], preferred_element_type=F32)
    is_ctx = _is_ctx(MERGE_TM, pl.program_id(1) * MERGE_TM)
    gt = jnp.where(is_ctx, _mod_slice(mc_ref[0], 2), _mod_slice(ml_ref[0], 2))
    o_ref[0] = x_ref[0] + gt * out


def merge_branches(ym, yg, yr, big, xs, mod, w_branch, w_out):
    bsz = xs.shape[0]
    mod3 = mod.reshape(bsz + 1, 1, 6 * D_MODEL)
    row = lambda b, s: (b, s, 0)
    gate_blk = C_MG // D_MODEL
    const = pl.Buffered(1)
    return pl.pallas_call(
        _merge_kernel,
        out_shape=jax.ShapeDtypeStruct(xs.shape, F32),
        grid=(bsz, S_ALL // MERGE_TM),
        in_specs=[pl.BlockSpec((1, MERGE_TM, D_MODEL), row),
                  pl.BlockSpec((1, MERGE_TM, D_MODEL), row),
                  pl.BlockSpec((1, MERGE_TM, D_MODEL), row),
                  pl.BlockSpec((1, MERGE_TM, D_MODEL), lambda b, s: (b, s, gate_blk)),
                  pl.BlockSpec((1, MERGE_TM, D_MODEL), lambda b, s: (b, s, gate_blk + 1)),
                  pl.BlockSpec((1, MERGE_TM, D_MODEL), lambda b, s: (b, s, gate_blk + 2)),
                  pl.BlockSpec((1, MERGE_TM, D_MODEL), row),
                  pl.BlockSpec((1, 1, 6 * D_MODEL), lambda b, s: (b, 0, 0)),
                  pl.BlockSpec((1, 1, 6 * D_MODEL), lambda b, s: (bsz, 0, 0)),
                  pl.BlockSpec((N_BRANCH, BRANCH_W, D_MODEL), lambda b, s: (0, 0, 0), pipeline_mode=const),
                  pl.BlockSpec((D_MODEL, D_MODEL), lambda b, s: (0, 0), pipeline_mode=const)],
        out_specs=pl.BlockSpec((1, MERGE_TM, D_MODEL), row),
        compiler_params=_cparams(("arbitrary", "arbitrary")),
        name="merge_branches",
    )(ym, yg, yr, big, big, big, xs, mod3, mod3, w_branch, w_out)


FFN_TM = 768
FFN_TF = 256


def _swiglu_chunk(h, w1, w3, w2):
    a = jnp.dot(h, w1, preferred_element_type=F32)
    b = jnp.dot(h, w3, preferred_element_type=F32)
    return jnp.dot((a * _sigmoid(a) * b).astype(BF16), w2, preferred_element_type=F32)


def _ffn_kernel(x_ref, g_ref, ml_ref, mc_ref, w1_ref, w3_ref, w2_ref, o_ref):
    first_row = pl.program_id(1) * FFN_TM
    h = _ada_norm(x_ref[0], g_ref[...], ml_ref[0], mc_ref[0], first_row, 1).astype(BF16)
    for f in range(D_FF // FFN_TF):
        cols = slice(f * FFN_TF, (f + 1) * FFN_TF)
        y = _swiglu_chunk(h, w1_ref[:, cols], w3_ref[:, cols], w2_ref[cols, :])
        if f == 0:
            o_ref[0] = y
        else:
            o_ref[0] += y
    gt = jnp.where(_is_ctx(FFN_TM, first_row), _mod_slice(mc_ref[0], 5), _mod_slice(ml_ref[0], 5))
    o_ref[0] = x_ref[0] + gt * o_ref[0]


def dense_ffn(xs, g, mod, w1, w3, w2):
    bsz = xs.shape[0]
    mod3 = mod.reshape(bsz + 1, 1, 6 * D_MODEL)
    const = pl.Buffered(1)
    return pl.pallas_call(
        _ffn_kernel,
        out_shape=jax.ShapeDtypeStruct(xs.shape, F32),
        grid=(bsz, S_ALL // FFN_TM),
        in_specs=[pl.BlockSpec((1, FFN_TM, D_MODEL), lambda b, s: (b, s, 0)),
                  pl.BlockSpec((1, D_MODEL), lambda b, s: (0, 0)),
                  pl.BlockSpec((1, 1, 6 * D_MODEL), lambda b, s: (b, 0, 0)),
                  pl.BlockSpec((1, 1, 6 * D_MODEL), lambda b, s: (bsz, 0, 0)),
                  pl.BlockSpec((D_MODEL, D_FF), lambda b, s: (0, 0), pipeline_mode=const),
                  pl.BlockSpec((D_MODEL, D_FF), lambda b, s: (0, 0), pipeline_mode=const),
                  pl.BlockSpec((D_FF, D_MODEL), lambda b, s: (0, 0), pipeline_mode=const)],
        out_specs=pl.BlockSpec((1, FFN_TM, D_MODEL), lambda b, s: (b, s, 0)),
        compiler_params=_cparams(("arbitrary", "arbitrary")),
        name="dense_ffn",
    )(xs, g.reshape(1, D_MODEL), mod3, mod3, w1, w3, w2)


NORM_TM = 768


def _norm_router_kernel(x_ref, g_ref, ml_ref, mc_ref, rw_ref, rb_ref, h_ref, lg_ref):
    h = _ada_norm(x_ref[0], g_ref[...], ml_ref[0], mc_ref[0], pl.program_id(1) * NORM_TM, 1)
    h_ref[0] = h.astype(h_ref.dtype)
    lg_ref[0] = jnp.dot(h, rw_ref[...], preferred_element_type=F32, precision=lax.Precision.HIGHEST) + rb_ref[...]


def norm_router(xs, g, mod, router_w, router_b):
    bsz = xs.shape[0]
    mod3 = mod.reshape(bsz + 1, 1, 6 * D_MODEL)
    rw = jnp.zeros((D_MODEL, LANE), F32).at[:, :N_EXPERTS].set(router_w)
    rb = jnp.zeros((1, LANE), F32).at[0, :N_EXPERTS].set(router_b)
    return pl.pallas_call(
        _norm_router_kernel,
        out_shape=(jax.ShapeDtypeStruct(xs.shape, BF16), jax.ShapeDtypeStruct((bsz, S_ALL, LANE), F32)),
        grid=(bsz, S_ALL // NORM_TM),
        in_specs=[pl.BlockSpec((1, NORM_TM, D_MODEL), lambda b, s: (b, s, 0)),
                  pl.BlockSpec((1, D_MODEL), lambda b, s: (0, 0)),
                  pl.BlockSpec((1, 1, 6 * D_MODEL), lambda b, s: (b, 0, 0)),
                  pl.BlockSpec((1, 1, 6 * D_MODEL), lambda b, s: (bsz, 0, 0)),
                  pl.BlockSpec((D_MODEL, LANE), lambda b, s: (0, 0)),
                  pl.BlockSpec((1, LANE), lambda b, s: (0, 0))],
        out_specs=(pl.BlockSpec((1, NORM_TM, D_MODEL), lambda b, s: (b, s, 0)),
                   pl.BlockSpec((1, NORM_TM, LANE), lambda b, s: (b, s, 0))),
        compiler_params=_cparams(("arbitrary", "arbitrary")),
        name="norm_router",
    )(xs, g.reshape(1, D_MODEL), mod3, mod3, rw, rb)


def _expert_kernel(be_ref, x_ref, w1_ref, w3_ref, w2_ref, o_ref):
    del be_ref
    x = x_ref[...]
    for f in range(D_FF // FFN_TF):
        cols = slice(f * FFN_TF, (f + 1) * FFN_TF)
        y = _swiglu_chunk(x, w1_ref[0, :, cols], w3_ref[0, :, cols], w2_ref[0, cols, :])
        if f == 0:
            o_ref[...] = y
        else:
            o_ref[...] += y


def grouped_experts(xb, block_e, w1, w3, w2):
    n_rows = xb.shape[0]
    const = pl.Buffered(1)
    grid_spec = pltpu.PrefetchScalarGridSpec(
        num_scalar_prefetch=1,
        grid=(n_rows // MOE_BLOCK,),
        in_specs=[pl.BlockSpec((MOE_BLOCK, D_MODEL), lambda i, be: (i, 0)),
                  pl.BlockSpec((1, D_MODEL, D_FF), lambda i, be: (be[i], 0, 0), pipeline_mode=const),
                  pl.BlockSpec((1, D_MODEL, D_FF), lambda i, be: (be[i], 0, 0), pipeline_mode=const),
                  pl.BlockSpec((1, D_FF, D_MODEL), lambda i, be: (be[i], 0, 0), pipeline_mode=const)],
        out_specs=pl.BlockSpec((MOE_BLOCK, D_MODEL), lambda i, be: (i, 0)),
    )
    return pl.pallas_call(
        _expert_kernel,
        out_shape=jax.ShapeDtypeStruct((n_rows, D_MODEL), F32),
        grid_spec=grid_spec,
        compiler_params=_cparams(("arbitrary",)),
        name="grouped_experts",
    )(block_e, xb, w1, w3, w2)


def moe_ffn(xs, g, mod, gt2, router_w, router_b, w1, w3, w2):
    bsz = xs.shape[0]
    h, logits = norm_router(xs, g, mod, router_w, router_b)
    h = h.reshape(-1, D_MODEL)
    n_tok = h.shape[0]
    top_logit, top_e = lax.top_k(logits.reshape(n_tok, LANE)[:, :N_EXPERTS], TOP_K)
    gate = jax.nn.softmax(top_logit, axis=-1)
    n_assign = n_tok * TOP_K
    flat_e = top_e.reshape(-1)
    flat_tok = jnp.repeat(jnp.arange(n_tok, dtype=jnp.int32), TOP_K)
    onehot = (flat_e[:, None] == jnp.arange(N_EXPERTS, dtype=flat_e.dtype)[None, :]).astype(jnp.int32)
    seen = jnp.cumsum(onehot, axis=0)
    padded = (seen[-1] + MOE_BLOCK - 1) // MOE_BLOCK * MOE_BLOCK
    p_end = jnp.cumsum(padded)
    slot = jnp.sum(onehot * ((p_end - padded)[None, :] + seen - 1), axis=1)
    n_blocks = -(-n_assign // MOE_BLOCK) + N_EXPERTS
    n_rows = n_blocks * MOE_BLOCK
    row_tok = jnp.zeros((n_rows,), jnp.int32).at[slot].set(flat_tok)
    slot = slot.reshape(n_tok, TOP_K)
    block_e = jnp.minimum(jnp.searchsorted(p_end, jnp.arange(n_blocks) * MOE_BLOCK, side='right'),
                          N_EXPERTS - 1).astype(jnp.int32)
    yb = grouped_experts(h[row_tok], block_e, w1, w3, w2)
    y = sum(yb[slot[:, k]] * gate[:, k:k + 1] for k in range(TOP_K))
    return xs + gt2 * y.reshape(bsz, S_ALL, D_MODEL)


FINAL_TM = 256


def _final_kernel(x_ref, g_ref, o_ref):
    x = x_ref[0]
    o_ref[0] = x * lax.rsqrt(jnp.mean(x * x, axis=-1, keepdims=True) + EPS) * g_ref[...]


def final_norm(xs, g):
    bsz = xs.shape[0]
    skip = CTX_LEN // FINAL_TM
    return pl.pallas_call(
        _final_kernel,
        out_shape=jax.ShapeDtypeStruct((bsz, SEQ, D_MODEL), F32),
        grid=(bsz, SEQ // FINAL_TM),
        in_specs=[pl.BlockSpec((1, FINAL_TM, D_MODEL), lambda b, s: (b, s + skip, 0)),
                  pl.BlockSpec((1, D_MODEL), lambda b, s: (0, 0))],
        out_specs=pl.BlockSpec((1, FINAL_TM, D_MODEL), lambda b, s: (b, s, 0)),
        compiler_params=_cparams(("arbitrary", "arbitrary")),
        name="final_norm",
    )(xs, g.reshape(1, D_MODEL))


N_CHUNKS = S_ALL // CHUNK
CTX_CHUNKS = CTX_LEN // CHUNK
GATE_ROWS = 8
NEG_INF = float("-inf")


def _softplus(v):
    return jnp.maximum(v, 0.0) + jnp.log(1.0 + jnp.exp(-jnp.abs(v)))


def _scan_constants():
    s = np.arange(CHUNK)[:, None]
    t = np.arange(CHUNK)[None, :]
    tri = np.concatenate([(s <= t), (s >= t), np.ones((CHUNK, CHUNK), bool)], axis=1).astype(np.float32)
    return jnp.asarray(tri), jnp.asarray(np.eye(CHUNK, dtype=np.float32))


def _rev_chunk(j):
    return jnp.where(j < CTX_CHUNKS, CTX_CHUNKS - 1 - j, N_CHUNKS + CTX_CHUNKS - 1 - j)


def _head_gate_rows(small, c0, heads, kind_major):
    bsz = small.shape[0]
    a = small[:, :, c0:c0 + 4 * heads].reshape(bsz, N_CHUNKS, CHUNK, 2, 2, heads)
    a = a.transpose((0, 5, 1, 4, 3, 2) if kind_major else (0, 5, 1, 3, 4, 2))
    a = a.reshape(bsz, heads, N_CHUNKS, 4, CHUNK)
    return jnp.pad(a, ((0, 0), (0, 0), (0, 0), (0, GATE_ROWS - 4), (0, 0)))


def _tn_dot(a, b):
    return lax.dot_general(a, b, (((0,), (0,)), ((), ())), preferred_element_type=F32)


def _nt_dot(a, b, precision=None):
    return lax.dot_general(a, b, (((1,), (1,)), ((), ())), preferred_element_type=F32, precision=precision)


def _dir_mask(d):
    t = lax.broadcasted_iota(jnp.int32, (CHUNK, CHUNK), 0)
    s = lax.broadcasted_iota(jnp.int32, (CHUNK, CHUNK), 1)
    return (s <= t, s < t) if d == 0 else (s >= t, s > t)


def _split_dot(a, ones, dot_fn):
    b = ones.astype(BF16)
    hi = a.astype(BF16)
    lo = (a - hi.astype(F32)).astype(BF16)
    return dot_fn(hi, b, preferred_element_type=F32) + dot_fn(lo, b, preferred_element_type=F32)


def _gate_forms(vals, tri, eye):
    cum = _split_dot(vals, tri, jnp.dot)
    kind = lax.broadcasted_iota(jnp.int32, vals.shape, 0) & (GATE_ROWS - 1)
    rows = jnp.where(kind == 1, cum[:, :CHUNK], jnp.where(kind == 3, cum[:, CHUNK:2 * CHUNK], vals))
    cols = _split_dot(rows, eye, lambda a, b, **kw: _nt_dot(b, a))
    return rows, cols, cum[:, 2 * CHUNK:2 * CHUNK + 1]


def _x_row(rows, d, g=0):
    return rows[GATE_ROWS * g + 2 * d:GATE_ROWS * g + 2 * d + 1]


def _run_row(rows, d, g=0):
    return rows[GATE_ROWS * g + 2 * d + 1:GATE_ROWS * g + 2 * d + 2]


def _x_col(cols, d, g=0):
    return cols[:, GATE_ROWS * g + 2 * d:GATE_ROWS * g + 2 * d + 1]


def _run_col(cols, d, g=0):
    return cols[:, GATE_ROWS * g + 2 * d + 1:GATE_ROWS * g + 2 * d + 2]


PAIRS_PER_STEP = 3
OUT_PAIRS_PER_STEP = 1


def _mlstm_kernel(q_ref, k_ref, v_ref, o_ref, g_ref, fb_ref, ng_ref, tri_ref, eye_ref, y_ref,
                  c_s, st_s, gr_s, gc_s, run_s):
    tri, eye = tri_ref[...], eye_ref[...]
    fbias = fb_ref[0]
    scale = M_DQK ** -0.5
    kind = lax.broadcasted_iota(jnp.int32, (2 * GATE_ROWS, CHUNK), 0) & 1
    fbias2 = jnp.concatenate([fbias, fbias], axis=0)
    st_row = lax.broadcasted_iota(jnp.int32, (GATE_ROWS, M_DQK), 0)

    def pair_rows(c0):
        return pl.ds(pl.multiple_of(c0 * CHUNK, 2 * CHUNK), 2 * CHUNK)

    def state_rows(c):
        return pl.ds(pl.multiple_of(c * M_DQK, M_DQK), M_DQK)

    def pair_state_rows(c0):
        return pl.ds(pl.multiple_of(c0 * M_DQK, 2 * M_DQK), 2 * M_DQK)

    def local(it, carry):
        pairs = [it * PAIRS_PER_STEP + p for p in range(PAIRS_PER_STEP)]
        forms = []
        for g in pairs:
            raw = jnp.concatenate([g_ref[0, 0, 2 * g], g_ref[0, 0, 2 * g + 1]], axis=0)
            vals = jnp.where(kind == 1, -_softplus(-(raw + fbias2)), raw)
            forms.append(_gate_forms(vals, tri, eye))
        zero = jnp.zeros((CHUNK, 2 * M_DQK), F32)
        lhs_all = []
        for g, (rows, cols, totals) in zip(pairs, forms):
            gr_s[g] = rows
            gc_s[g] = cols
            kf2 = k_ref[0, pair_rows(2 * g), :].astype(F32)
            lhs = []
            for i in range(2):
                kf = kf2[i * CHUNK:(i + 1) * CHUNK]
                kws = []
                for d in range(2):
                    tot = totals[GATE_ROWS * i + 2 * d + 1:GATE_ROWS * i + 2 * d + 2]
                    m_loc = jnp.max(tot - _run_row(rows, d, i) + _x_row(rows, d, i), axis=-1, keepdims=True)
                    kw = kf * (jnp.exp(tot - _run_col(cols, d, i) + _x_col(cols, d, i) - m_loc) * scale)
                    kws.append(kw)
                    st_s[d, 2 * g + i] = jnp.where(st_row == 0, jnp.sum(kw, axis=0, keepdims=True),
                                                   jnp.where(st_row == 1, m_loc, tot))
                lhs.append(jnp.concatenate(kws + [zero] if i == 0 else [zero] + kws, axis=1))
            lhs_all.append(jnp.concatenate(lhs, axis=0).astype(BF16))
        for g, lhs in zip(pairs, lhs_all):
            c_loc = _tn_dot(lhs, v_ref[0, pair_rows(2 * g), :])
            for i in range(2):
                for d in range(2):
                    c_s[d, state_rows(2 * g + i), :] = c_loc[(2 * i + d) * M_DQK:(2 * i + d + 1) * M_DQK]
        return carry

    lax.fori_loop(0, N_CHUNKS // (2 * PAIRS_PER_STEP), local, 0)

    run_s[...] = jnp.zeros_like(run_s)

    def scan(j, carry):
        out = []
        for d in range(2):
            n_prev, m_prev = carry[d]
            c = j if d == 0 else _rev_chunk(j)
            st = st_s[d, c]
            n_loc, m_loc, tot = st[0:1], st[1:2, :1], st[2:3, :1]
            m_new = jnp.maximum(tot + m_prev, m_loc)
            s_prev = jnp.exp(tot + m_prev - m_new)
            s_loc = jnp.exp(m_loc - m_new)
            c_prev = run_s[d]
            run_s[d] = s_prev * c_prev + s_loc * c_s[d, state_rows(c), :]
            c_s[d, state_rows(c), :] = c_prev
            st_s[d, c] = jnp.where(st_row == 0, n_prev, m_prev)
            out.append((s_prev * n_prev + s_loc * n_loc, m_new))
        return tuple(out)

    zero_nm = (jnp.zeros((1, M_DQK), F32), jnp.zeros((1, 1), F32))
    lax.fori_loop(0, N_CHUNKS, scan, (zero_nm, zero_nm))

    def output(it, carry):
        pairs = [it * OUT_PAIRS_PER_STEP + p for p in range(OUT_PAIRS_PER_STEP)]
        qk_all = [_nt_dot(q_ref[0, pair_rows(2 * g), :], k_ref[0, pair_rows(2 * g), :]) * scale for g in pairs]
        z_intra = jnp.zeros((CHUNK, 2 * CHUNK), F32)
        z_inter = jnp.zeros((CHUNK, M_DQK), F32)
        operands = []
        for g, qk2 in zip(pairs, qk_all):
            c0 = 2 * g
            qf2, v2 = q_ref[0, pair_rows(c0), :].astype(F32), v_ref[0, pair_rows(c0), :]
            rows, cols = gr_s[g], gc_s[g]
            lhs_intra, lhs_inter, rhs_v = [], [], []
            for i in range(2):
                qk = qk2[i * CHUNK:(i + 1) * CHUNK, i * CHUNK:(i + 1) * CHUNK]
                qf = qf2[i * CHUNK:(i + 1) * CHUNK]
                intra, inter = [], []
                for d in range(2):
                    st = st_s[d, c0 + i]
                    b_c = _run_col(cols, d, i)
                    n0, m0 = st[0:1], st[1:2, :1]
                    incl, _ = _dir_mask(d)
                    d_log = jnp.where(incl, b_c - _run_row(rows, d, i) + _x_row(rows, d, i), NEG_INF)
                    m_inter = b_c + m0
                    m_comb = jnp.maximum(m_inter, jnp.max(d_log, axis=-1, keepdims=True))
                    s = qk * jnp.exp(d_log - m_comb)
                    e_inter = jnp.exp(m_inter - m_comb)
                    den = (jnp.sum(s, axis=-1, keepdims=True)
                           + e_inter * jnp.sum(qf * n0, axis=-1, keepdims=True))
                    inv = 1.0 / jnp.maximum(jnp.abs(den), jnp.exp(-m_comb))
                    intra.append(s * inv)
                    inter.append(qf * (e_inter * inv))
                pair = jnp.concatenate(intra, axis=1)
                lhs_intra.append(jnp.concatenate([pair, z_intra] if i == 0 else [z_intra, pair], axis=1))
                lhs_inter.append(jnp.concatenate([inter[0], z_inter, inter[1], z_inter] if i == 0
                                                 else [z_inter, inter[0], z_inter, inter[1]], axis=1))
                v = v2[i * CHUNK:(i + 1) * CHUNK]
                rhs_v += [v, v]
            c_in = jnp.concatenate([c_s[0, pair_state_rows(c0), :], c_s[1, pair_state_rows(c0), :]], axis=0)
            operands.append((jnp.concatenate(lhs_intra, axis=0).astype(BF16), jnp.concatenate(rhs_v, axis=0),
                             jnp.concatenate(lhs_inter, axis=0).astype(BF16), c_in.astype(BF16)))
        hs = [jnp.dot(li, rv, preferred_element_type=F32) + jnp.dot(le, ci, preferred_element_type=F32)
              for li, rv, le, ci in operands]
        for g, h in zip(pairs, hs):
            h = h * lax.rsqrt(jnp.mean(h * h, axis=-1, keepdims=True) + EPS) * ng_ref[...]
            y_ref[0, pair_rows(2 * g), :] = (h * _sigmoid(o_ref[0, pair_rows(2 * g), :].astype(F32))
                                             ).astype(y_ref.dtype)
        return carry

    lax.fori_loop(0, N_CHUNKS // (2 * OUT_PAIRS_PER_STEP), output, 0)


def mlstm_mixer(big, small, f_bias, norm_g):
    bsz = big.shape[0]
    gates = _head_gate_rows(small, 0, M_HEADS, False)
    fb = jnp.zeros((M_HEADS, GATE_ROWS, CHUNK), F32)
    fb = fb.at[:, 1].set(f_bias[0][:, None]).at[:, 3].set(f_bias[1][:, None])
    tri, eye = _scan_constants()
    qb, kb, vb, ob = C_MQ // M_DQK, C_MK // M_DQK, C_MV // M_DV, C_MO // M_DV
    return pl.pallas_call(
        _mlstm_kernel,
        out_shape=jax.ShapeDtypeStruct((bsz, S_ALL, BRANCH_W), BF16),
        grid=(bsz, M_HEADS),
        in_specs=[pl.BlockSpec((1, S_ALL, M_DQK), lambda b, h: (b, 0, qb + h)),
                  pl.BlockSpec((1, S_ALL, M_DQK), lambda b, h: (b, 0, kb + h)),
                  pl.BlockSpec((1, S_ALL, M_DV), lambda b, h: (b, 0, vb + h)),
                  pl.BlockSpec((1, S_ALL, M_DV), lambda b, h: (b, 0, ob + h)),
                  pl.BlockSpec((1, 1, N_CHUNKS, GATE_ROWS, CHUNK), lambda b, h: (b, h, 0, 0, 0)),
                  pl.BlockSpec((1, GATE_ROWS, CHUNK), lambda b, h: (h, 0, 0)),
                  pl.BlockSpec((1, M_DV), lambda b, h: (0, h)),
                  pl.BlockSpec((CHUNK, 3 * CHUNK), lambda b, h: (0, 0)),
                  pl.BlockSpec((CHUNK, CHUNK), lambda b, h: (0, 0))],
        out_specs=pl.BlockSpec((1, S_ALL, M_DV), lambda b, h: (b, 0, h)),
        scratch_shapes=[pltpu.VMEM((2, N_CHUNKS * M_DQK, M_DV), F32),
                        pltpu.VMEM((2, N_CHUNKS, GATE_ROWS, M_DQK), F32),
                        pltpu.VMEM((N_CHUNKS // 2, 2 * GATE_ROWS, CHUNK), F32),
                        pltpu.VMEM((N_CHUNKS // 2, CHUNK, 2 * GATE_ROWS), F32),
                        pltpu.VMEM((2, M_DQK, M_DV), F32)],
        compiler_params=_cparams(("arbitrary", "arbitrary")),
        name="mlstm_mixer",
    )(big, big, big, big, gates, fb, norm_g.reshape(1, BRANCH_W), tri, eye)


CONV_BLK = 256
CONV_HALO = 16
SOLVE_BLK = 2
SOLVE_GROUP = 4
PREP_GROUPS = 3


def _conv_silu(src_ref, w, dst_ref, l2_scale):
    n_blk = S_ALL // CONV_BLK
    zeros = jnp.zeros((CONV_HALO, src_ref.shape[-1]), F32)
    for i in range(n_blk):
        lo = i * CONV_BLK
        first = i == 0 or lo == CTX_LEN
        last = lo + CONV_BLK in (CTX_LEN, S_ALL)
        body = src_ref[0, lo - (0 if first else CONV_HALO):lo + CONV_BLK + (0 if last else CONV_HALO), :]
        win = jnp.concatenate(([zeros] if first else []) + [body.astype(F32)] + ([zeros] if last else []), axis=0)
        n = win.shape[0]
        y = None
        for j in range(CONV_W):
            tap = win if j == CONV_W // 2 else pltpu.roll(win, (CONV_W // 2 - j) % n, 0)
            term = tap[CONV_HALO:CONV_HALO + CONV_BLK] * w[j:j + 1, :]
            y = term if y is None else y + term
        y = y * _sigmoid(y)
        if l2_scale is not None:
            y = y * (lax.rsqrt(jnp.sum(y * y, axis=-1, keepdims=True) + EPS) * l2_scale)
        dst_ref[lo:lo + CONV_BLK, :] = y.astype(dst_ref.dtype)


def _block_masks():
    n = SOLVE_GROUP * CHUNK
    t = np.arange(n)[:, None]
    s = np.arange(n)[None, :]
    masks = [(t // SOLVE_BLK) == (s // SOLVE_BLK)]
    k = SOLVE_BLK
    while k < CHUNK:
        masks.append(((t // (2 * k)) == (s // (2 * k))) & ((t // k) != (s // k)))
        k *= 2
    return jnp.asarray(np.stack(masks).astype(np.float32), BF16)


def _unit_triangular_inverse(systems, eye, masks):
    abs_ = [a.astype(BF16) for a in systems]
    ts = [eye - (ab * masks[0]).astype(F32) for ab in abs_]
    for lvl in range(1, masks.shape[0]):
        tbs = [t.astype(BF16) for t in ts]
        tls = [jnp.dot(tb, ab * masks[lvl], preferred_element_type=F32) for tb, ab in zip(tbs, abs_)]
        ts = [t - jnp.dot(tl.astype(BF16), tb, preferred_element_type=F32) for t, tl, tb in zip(ts, tls, tbs)]
    return ts


def _block_diag(blocks):
    n = len(blocks)
    z_half = jnp.zeros((CHUNK, CHUNK), F32)
    z_tile = jnp.zeros((CHUNK, 2 * CHUNK), F32)
    out = []
    for p, blk in enumerate(blocks):
        pair = jnp.concatenate([blk, z_half] if p % 2 == 0 else [z_half, blk], axis=1)
        out.append(jnp.concatenate([pair if t == p // 2 else z_tile for t in range(n // 2)], axis=1))
    return jnp.concatenate(out, axis=0)


def _gdn_kernel(xq_ref, xk_ref, xv_ref, z_ref, g_ref, gp_ref, cw_ref, ng_ref, tri_ref, eye_ref, bm_ref, y_ref,
                q_s, k_s, v_s, u_s, wq_s, at_s, kt_s, cd_s, acc_ref):
    tri, eye, masks = tri_ref[...], eye_ref[...], bm_ref[...]
    cw = cw_ref[0]
    _conv_silu(xq_ref, cw[:, :G_DK], q_s, G_DK ** -0.5)
    _conv_silu(xk_ref, cw[:, G_DK:2 * G_DK], k_s, 1.0)
    _conv_silu(xv_ref, cw[:, 2 * G_DK:], v_s, None)
    n_pair = SOLVE_GROUP // 2
    kind = lax.broadcasted_iota(jnp.int32, (n_pair * GATE_ROWS, CHUNK), 0) & 1
    neg_a = jnp.concatenate([gp_ref[0, 0]] * n_pair, axis=0)
    dt_bias = jnp.concatenate([gp_ref[0, 1]] * n_pair, axis=0)
    n_bd = SOLVE_GROUP * CHUNK
    eye_bd = (lax.broadcasted_iota(jnp.int32, (n_bd, n_bd), 0)
              == lax.broadcasted_iota(jnp.int32, (n_bd, n_bd), 1)).astype(F32)
    z_half = jnp.zeros((CHUNK, CHUNK), F32)

    def chunk_rows(c):
        return pl.ds(pl.multiple_of(c * CHUNK, CHUNK), CHUNK)

    def wq_rows(c):
        return pl.ds(pl.multiple_of(c * 2 * CHUNK, 2 * CHUNK), 2 * CHUNK)

    def prep(g, carry):
        systems, rhs_all, q_dec_all = [], [], []
        for sgrp in range(PREP_GROUPS):
            c0 = (g * PREP_GROUPS + sgrp) * n_pair
            raw = jnp.concatenate([g_ref[0, 0, c0 + i] for i in range(n_pair)], axis=0)
            vals = jnp.where(kind == 1, neg_a * _softplus(raw + dt_bias), _sigmoid(raw))
            g_rows, g_cols, totals = _gate_forms(vals, tri, eye)
            blocks, rhs, q_dec = [], [], []
            for i in range(n_pair):
                c = c0 + i
                q, k, v = q_s[chunk_rows(c), :], k_s[chunk_rows(c), :], v_s[chunk_rows(c), :]
                kf, vf, qf = k.astype(F32), v.astype(F32), q.astype(F32)
                kq = _nt_dot(jnp.concatenate([k, q], axis=0), k)
                kk, qk = kq[:CHUNK], kq[CHUNK:]
                for d in range(2):
                    gam_r, beta_c, gam_c = _run_row(g_rows, d, i), _x_col(g_cols, d, i), _run_col(g_cols, d, i)
                    tot = totals[GATE_ROWS * i + 2 * d + 1:GATE_ROWS * i + 2 * d + 2]
                    incl, strict = _dir_mask(d)
                    decay = jnp.exp(jnp.where(incl, gam_c - gam_r, NEG_INF))
                    blocks.append(jnp.where(strict, kk * beta_c * decay, 0.0))
                    e_gam = jnp.exp(gam_c)
                    rhs.append(jnp.concatenate([vf * beta_c, kf * (beta_c * e_gam)], axis=-1))
                    q_dec.append(qf * e_gam)
                    attn = qk * decay
                    at_s[d, chunk_rows(c), :] = jnp.concatenate([attn, z_half] if d == 0 else [z_half, attn],
                                                                axis=1).astype(BF16)
                    kt_s[d, chunk_rows(c), :] = (kf * jnp.exp(tot - gam_c)).astype(BF16)
                    cd_s[d, c] = jnp.broadcast_to(jnp.exp(tot), cd_s.shape[2:])
            systems.append(_block_diag(blocks))
            rhs_all.append(jnp.concatenate(rhs, axis=0).astype(BF16))
            q_dec_all.append(q_dec)
        t_invs = _unit_triangular_inverse(systems, eye_bd, masks)
        for sgrp in range(PREP_GROUPS):
            c0 = (g * PREP_GROUPS + sgrp) * n_pair
            x = jnp.dot(t_invs[sgrp].astype(BF16), rhs_all[sgrp], preferred_element_type=F32)
            for i in range(n_pair):
                for d in range(2):
                    p = 2 * i + d
                    xp = x[p * CHUNK:(p + 1) * CHUNK]
                    u_s[d, chunk_rows(c0 + i), :] = xp[:, :G_DV]
                    wq_s[d, wq_rows(c0 + i), :] = jnp.concatenate([xp[:, G_DV:], q_dec_all[sgrp][p]],
                                                                  axis=0).astype(BF16)
        return carry

    lax.fori_loop(0, N_CHUNKS // (n_pair * PREP_GROUPS), prep, 0)

    z_v = jnp.zeros((CHUNK, G_DV), BF16)

    def step(j, s_prev):
        cs = (j, _rev_chunk(j))
        ws_qs = [jnp.dot(wq_s[d, wq_rows(cs[d]), :], s_prev[d].astype(BF16), preferred_element_type=F32)
                 for d in range(2)]
        v_new = [(u_s[d, chunk_rows(cs[d]), :] - ws_qs[d][:CHUNK]).astype(BF16) for d in range(2)]
        o = [jnp.dot(at_s[d, chunk_rows(cs[d]), :],
                     jnp.concatenate([v_new[d], z_v] if d == 0 else [z_v, v_new[d]], axis=0),
                     preferred_element_type=F32) for d in range(2)]
        ktv = [_tn_dot(kt_s[d, chunk_rows(cs[d]), :], v_new[d]) for d in range(2)]
        for d in range(2):
            acc_ref[d, chunk_rows(cs[d]), :] = ws_qs[d][CHUNK:] + o[d]
        return tuple(cd_s[d, cs[d]][:1, :] * s_prev[d] + ktv[d] for d in range(2))

    s_zero = jnp.zeros((G_DK, G_DV), F32)
    lax.fori_loop(0, N_CHUNKS, step, (s_zero, s_zero))

    def finish(c, carry):
        rows = pl.ds(pl.multiple_of(c * CHUNK, CHUNK), CHUNK)
        h = acc_ref[0, rows, :] + acc_ref[1, rows, :]
        h = h * lax.rsqrt(jnp.mean(h * h, axis=-1, keepdims=True) + EPS) * ng_ref[...]
        z = z_ref[0, rows, :].astype(F32)
        y_ref[0, rows, :] = (h * (z * _sigmoid(z))).astype(y_ref.dtype)
        return carry

    lax.fori_loop(0, N_CHUNKS, finish, 0, unroll=4)


def gdn_mixer(big, small, conv_w, a_log, dt_bias, norm_g):
    bsz = big.shape[0]
    gates = _head_gate_rows(small, 4 * M_HEADS, G_HEADS, True)
    gp = jnp.zeros((G_HEADS, 2, GATE_ROWS, CHUNK), F32)
    for d in range(2):
        gp = gp.at[:, 0, 2 * d + 1].set(-jnp.exp(a_log[d])[:, None]).at[:, 1, 2 * d + 1].set(dt_bias[d][:, None])
    cw = conv_w.reshape(CONV_W, 3, G_HEADS, G_DK).transpose(2, 0, 1, 3).reshape(G_HEADS, CONV_W, 3 * G_DK)
    cw = jnp.pad(cw, ((0, 0), (0, 8 - CONV_W), (0, 0)))
    tri, eye = _scan_constants()
    masks = _block_masks()
    qb, kb, vb, zb = C_GQ // G_DK, C_GK // G_DK, C_GV // G_DV, C_GZ // G_DV
    return pl.pallas_call(
        _gdn_kernel,
        out_shape=jax.ShapeDtypeStruct((bsz, S_ALL, BRANCH_W), BF16),
        grid=(bsz, G_HEADS),
        in_specs=[pl.BlockSpec((1, S_ALL, G_DK), lambda b, h: (b, 0, qb + h)),
                  pl.BlockSpec((1, S_ALL, G_DK), lambda b, h: (b, 0, kb + h)),
                  pl.BlockSpec((1, S_ALL, G_DV), lambda b, h: (b, 0, vb + h)),
                  pl.BlockSpec((1, S_ALL, G_DV), lambda b, h: (b, 0, zb + h)),
                  pl.BlockSpec((1, 1, N_CHUNKS, GATE_ROWS, CHUNK), lambda b, h: (b, h, 0, 0, 0)),
                  pl.BlockSpec((1, 2, GATE_ROWS, CHUNK), lambda b, h: (h, 0, 0, 0)),
                  pl.BlockSpec((1, 8, 3 * G_DK), lambda b, h: (h, 0, 0)),
                  pl.BlockSpec((1, G_DV), lambda b, h: (0, 0)),
                  pl.BlockSpec((CHUNK, 3 * CHUNK), lambda b, h: (0, 0)),
                  pl.BlockSpec((CHUNK, CHUNK), lambda b, h: (0, 0)),
                  pl.BlockSpec(masks.shape, lambda b, h: (0, 0, 0))],
        out_specs=pl.BlockSpec((1, S_ALL, G_DV), lambda b, h: (b, 0, h)),
        scratch_shapes=[pltpu.VMEM((S_ALL, G_DK), BF16), pltpu.VMEM((S_ALL, G_DK), BF16),
                        pltpu.VMEM((S_ALL, G_DV), BF16),
                        pltpu.VMEM((2, S_ALL, G_DV), F32),
                        pltpu.VMEM((2, 2 * S_ALL, G_DK), BF16),
                        pltpu.VMEM((2, S_ALL, 2 * CHUNK), BF16),
                        pltpu.VMEM((2, S_ALL, G_DK), BF16),
                        pltpu.VMEM((2, N_CHUNKS, 8, G_DV), F32),
                        pltpu.VMEM((2, S_ALL, G_DV), F32)],
        compiler_params=_cparams(("arbitrary", "arbitrary")),
        name="gdn_mixer",
    )(big, big, big, big, gates, gp, cw, norm_g.reshape(1, G_DV), tri, eye, masks)


GRID_H = SEQ // GRID_W
LRU_BLK = 256


def _shift_rows(a, s):
    n = a.shape[0]
    r = lax.broadcasted_iota(jnp.int32, a.shape, 0)
    return jnp.where((r >= s) & (r < n + s), pltpu.roll(a, s % n, 0), 0.0)


def _gelu(v):
    return 0.5 * v * (1.0 + jnp.tanh(0.7978845608028654 * (v + 0.044715 * v * v * v)))


def _lru_kernel(x_ref, y_ref, cw_ref, cb_ref, gw_ref, gb_ref, lam_ref, o_ref,
                xb_s, a_s, u_s, h_s, p_s, end_s, cin_s):
    cw = cw_ref[...]
    xf = x_ref[0].astype(F32)
    xc = xf[:CTX_LEN]
    lat = lambda r0, r1: xf[CTX_LEN + r0 * GRID_W:CTX_LEN + r1 * GRID_W]
    prev1 = jnp.concatenate([_shift_rows(lat(GRID_H - 1, GRID_H), 1), lat(0, GRID_H - 1)], axis=0)
    prev2 = jnp.concatenate([_shift_rows(lat(GRID_H - 2, GRID_H - 1), 1), _shift_rows(lat(GRID_H - 1, GRID_H), 1),
                             lat(0, GRID_H - 2)], axis=0)
    next1 = jnp.concatenate([lat(1, GRID_H), _shift_rows(lat(0, 1), -1)], axis=0)
    taps_c = (_shift_rows(xc, 2), _shift_rows(xc, 1), xc, _shift_rows(xc, -1))
    taps_l = (prev2, prev1, lat(0, GRID_H), next1)
    xb_s[:CTX_LEN, :] = sum(t * cw[j:j + 1, :] for j, t in enumerate(taps_c)) + cb_ref[...]
    xb_s[CTX_LEN:, :] = sum(t * cw[j:j + 1, :] for j, t in enumerate(taps_l)) + cb_ref[...]

    neg_c_sp = -LRU_C * _softplus(-lam_ref[...])

    def gates(i, carry):
        rows = pl.ds(pl.multiple_of(i * LRU_BLK, LRU_BLK), LRU_BLK)
        xb = xb_s[rows, :]
        pre = jnp.dot(xb.astype(BF16), gw_ref[0], preferred_element_type=F32) + gb_ref[0]
        for d in range(2):
            r = _sigmoid(pre[:, (2 * d) * LANE:(2 * d + 1) * LANE])
            g_in = _sigmoid(pre[:, (2 * d + 1) * LANE:(2 * d + 2) * LANE])
            log_a = neg_c_sp[d:d + 1, :] * r
            a = jnp.exp(log_a)
            a_s[d, rows, :] = a
            u_s[d, rows, :] = jnp.sqrt(jnp.maximum(1.0 - a * a, 0.0)) * g_in * xb
        return carry

    lax.fori_loop(0, S_ALL // LRU_BLK, gates, 0)

    def ctx_step(t, carry):
        hf, hb = carry
        tb = CTX_LEN - 1 - t
        hf = a_s[0, pl.ds(t, 1), :] * hf + u_s[0, pl.ds(t, 1), :]
        hb = a_s[1, pl.ds(tb, 1), :] * hb + u_s[1, pl.ds(tb, 1), :]
        h_s[0, pl.ds(t, 1), :] = hf
        h_s[1, pl.ds(tb, 1), :] = hb
        return hf, hb

    zero_row = jnp.zeros((1, LANE), F32)
    hf0, hb0 = lax.fori_loop(0, CTX_LEN, ctx_step, (zero_row, zero_row), unroll=8)

    def col_step(i, carry):
        hf, pf, hb, pb = carry
        rf = pl.ds(pl.multiple_of(CTX_LEN + i * GRID_W, GRID_W), GRID_W)
        rb = pl.ds(pl.multiple_of(CTX_LEN + (GRID_H - 1 - i) * GRID_W, GRID_W), GRID_W)
        af, ab = a_s[0, rf, :], a_s[1, rb, :]
        hf = af * hf + u_s[0, rf, :]
        pf = af * pf
        hb = ab * hb + u_s[1, rb, :]
        pb = ab * pb
        h_s[0, rf, :] = hf
        p_s[0, pl.ds(pl.multiple_of(i * GRID_W, GRID_W), GRID_W), :] = pf
        h_s[1, rb, :] = hb
        p_s[1, pl.ds(pl.multiple_of((GRID_H - 1 - i) * GRID_W, GRID_W), GRID_W), :] = pb
        return hf, pf, hb, pb

    zeros = jnp.zeros((GRID_W, LANE), F32)
    ones = jnp.ones((GRID_W, LANE), F32)
    hf, pf, hb, pb = lax.fori_loop(0, GRID_H, col_step, (zeros, ones, zeros, ones))
    end_s[0], end_s[1], end_s[2], end_s[3] = hf, pf, hb, pb

    def chain_step(w, carry):
        cf, cb = carry
        wb = GRID_W - 1 - w
        cin_s[0, pl.ds(w, 1), :] = cf
        cin_s[1, pl.ds(wb, 1), :] = cb
        cf = end_s[0, pl.ds(w, 1), :] + end_s[1, pl.ds(w, 1), :] * cf
        cb = end_s[2, pl.ds(wb, 1), :] + end_s[3, pl.ds(wb, 1), :] * cb
        return cf, cb

    lax.fori_loop(0, GRID_W, chain_step, (hf0, hb0), unroll=8)

    o_ref[0, :CTX_LEN, :] = ((h_s[0, :CTX_LEN, :] + h_s[1, :CTX_LEN, :])
                             * _gelu(y_ref[0, :CTX_LEN, :].astype(F32))).astype(o_ref.dtype)

    def out_step(i, carry):
        rows = pl.ds(pl.multiple_of(CTX_LEN + i * GRID_W, GRID_W), GRID_W)
        prow = pl.ds(pl.multiple_of(i * GRID_W, GRID_W), GRID_W)
        h = (h_s[0, rows, :] + p_s[0, prow, :] * cin_s[0] + h_s[1, rows, :] + p_s[1, prow, :] * cin_s[1])
        o_ref[0, rows, :] = (h * _gelu(y_ref[0, rows, :].astype(F32))).astype(o_ref.dtype)
        return carry

    lax.fori_loop(0, GRID_H, out_step, 0, unroll=2)


def lru_mixer(big, conv_w, conv_b, gate_w, gate_b, lam):
    bsz = big.shape[0]
    gw = gate_w.transpose(2, 3, 0, 1, 4).reshape(R_BLOCKS, R_BDIM, 4 * R_BDIM).astype(BF16)
    gb = gate_b.reshape(2, 2, R_BLOCKS, R_BDIM).transpose(2, 0, 1, 3).reshape(R_BLOCKS, 1, 4 * R_BDIM)
    cw = jnp.pad(conv_w, ((0, 8 - CONV_W), (0, 0)))
    xb, yb = C_RX // R_BDIM, C_RY // R_BDIM
    return pl.pallas_call(
        _lru_kernel,
        out_shape=jax.ShapeDtypeStruct((bsz, S_ALL, BRANCH_W), BF16),
        grid=(bsz, R_BLOCKS),
        in_specs=[pl.BlockSpec((1, S_ALL, R_BDIM), lambda b, n: (b, 0, xb + n)),
                  pl.BlockSpec((1, S_ALL, R_BDIM), lambda b, n: (b, 0, yb + n)),
                  pl.BlockSpec((8, R_BDIM), lambda b, n: (0, n)),
                  pl.BlockSpec((1, R_BDIM), lambda b, n: (0, n)),
                  pl.BlockSpec((1, R_BDIM, 4 * R_BDIM), lambda b, n: (n, 0, 0)),
                  pl.BlockSpec((1, 1, 4 * R_BDIM), lambda b, n: (n, 0, 0)),
                  pl.BlockSpec((2, R_BDIM), lambda b, n: (0, n))],
        out_specs=pl.BlockSpec((1, S_ALL, R_BDIM), lambda b, n: (b, 0, n)),
        scratch_shapes=[pltpu.VMEM((S_ALL, R_BDIM), F32), pltpu.VMEM((2, S_ALL, R_BDIM), F32),
                        pltpu.VMEM((2, S_ALL, R_BDIM), F32), pltpu.VMEM((2, S_ALL, R_BDIM), F32),
                        pltpu.VMEM((2, SEQ, R_BDIM), F32), pltpu.VMEM((4, GRID_W, R_BDIM), F32),
                        pltpu.VMEM((2, GRID_W, R_BDIM), F32)],
        compiler_params=_cparams(("arbitrary", "arbitrary")),
        name="lru_mixer",
    )(big, big, cw, conv_b.reshape(1, BRANCH_W), gw, gb, lam)


def _pack_in_weights(w, b):
    segs_w = [w[:, IN_OFFS[i]:IN_OFFS[i + 1]] for i in range(len(IN_COLS))]
    segs_b = [b[IN_OFFS[i]:IN_OFFS[i + 1]] for i in range(len(IN_COLS))]
    w_big = jnp.concatenate([segs_w[i] for i in BIG_SEGS], axis=1).astype(BF16)
    b_big = jnp.concatenate([segs_b[i] for i in BIG_SEGS]).reshape(1, N_BIG)
    pad = N_SMALL - sum(IN_COLS[i] for i in SMALL_SEGS)
    w_small = jnp.pad(jnp.concatenate([segs_w[i] for i in SMALL_SEGS], axis=1), ((0, 0), (0, pad))).astype(BF16)
    b_small = jnp.pad(jnp.concatenate([segs_b[i] for i in SMALL_SEGS]), (0, pad)).reshape(1, N_SMALL)
    return w_big, b_big, w_small, b_small


def kernel(x, c, ctx, c_ctx, mod_w, mod_b, norm1_g, norm2_g, final_g, w_in, b_in, m_fbias, m_norm_g,
           g_conv, g_a_log, g_dt_bias, g_norm_g, r_conv, r_conv_b, r_gate_w, r_gate_b, r_lambda,
           w_branch, w_out, ffn_w1, ffn_w3, ffn_w2, router_w, router_b, moe_w1, moe_w3, moe_w2):
    bsz = x.shape[0]
    xs = jnp.concatenate([ctx, x], axis=1)
    src = jnp.concatenate([c, c_ctx[None]], axis=0)
    mods = modulation(src, mod_w, mod_b)
    for layer in range(DEPTH):
        mod = mods[layer]
        w_big, b_big, w_small, b_small = _pack_in_weights(w_in[layer], b_in[layer])
        big, small = in_projection(xs, norm1_g[layer], mod, w_big, b_big, w_small, b_small)
        ym = mlstm_mixer(big, small, m_fbias[layer], m_norm_g[layer])
        yg = gdn_mixer(big, small, g_conv[layer], g_a_log[layer], g_dt_bias[layer], g_norm_g[layer])
        yr = lru_mixer(big, r_conv[layer], r_conv_b[layer], r_gate_w[layer], r_gate_b[layer], r_lambda[layer])
        xs = merge_branches(ym, yg, yr, big, xs, mod, w_branch[layer].astype(BF16), w_out[layer].astype(BF16))
        j = layer // 2
        if layer % 2 == 0:
            xs = dense_ffn(xs, norm2_g[layer], mod, ffn_w1[j].astype(BF16), ffn_w3[j].astype(BF16),
                           ffn_w2[j].astype(BF16))
        else:
            gt2 = jnp.concatenate([jnp.broadcast_to(mod[bsz:, None, 5 * D_MODEL:], (bsz, CTX_LEN, D_MODEL)),
                                   jnp.broadcast_to(mod[:bsz, None, 5 * D_MODEL:], (bsz, SEQ, D_MODEL))], axis=1)
            xs = moe_ffn(xs, norm2_g[layer], mod, gt2, router_w[j], router_b[j], moe_w1[j].astype(BF16),
                         moe_w3[j].astype(BF16), moe_w2[j].astype(BF16))
    return final_norm(xs, final_g)
```

```python
import jax
import jax.numpy as jnp
import numpy as np
from jax import lax
from jax.experimental import pallas as pl
from jax.experimental.pallas import tpu as pltpu

D_MODEL = 1024
SEQ = 2048
DEPTH = 4
GRID_W = 64
CTX_LEN = 256
S_ALL = CTX_LEN + SEQ
BRANCH_W = D_MODEL
N_BRANCH = 3
CHUNK = 64
CONV_W = 4
M_HEADS = 4
M_DQK = 128
M_DV = BRANCH_W // M_HEADS
G_HEADS = 8
G_DK = 128
G_DV = BRANCH_W // G_HEADS
R_BLOCKS = 8
R_BDIM = BRANCH_W // R_BLOCKS
LRU_C = 8.0
D_FF = 2816
N_EXPERTS = 8
TOP_K = 2
MOE_BLOCK = 512
EPS = 1e-6
IN_COLS = (M_HEADS * M_DQK, M_HEADS * M_DQK, BRANCH_W, BRANCH_W, 4 * M_HEADS,
           2 * G_HEADS * G_DK + BRANCH_W, BRANCH_W, 2 * G_HEADS, 2 * G_HEADS,
           BRANCH_W, BRANCH_W, N_BRANCH * D_MODEL)
IN_OFFS = tuple(int(v) for v in np.cumsum((0,) + IN_COLS))
SMALL_SEGS = (4, 7, 8)
BIG_SEGS = tuple(i for i in range(len(IN_COLS)) if i not in SMALL_SEGS)
N_BIG = sum(IN_COLS[i] for i in BIG_SEGS)
N_SMALL = 128
C_MQ, C_MK, C_MV, C_MO = 0, 512, 1024, 2048
C_GQ, C_GK, C_GV, C_GZ = 3072, 4096, 5120, 6144
C_RX, C_RY, C_MG = 7168, 8192, 9216

LANE = 128
VMEM_LIMIT = 56 * 1024 * 1024
F32 = jnp.float32
BF16 = jnp.bfloat16


def _cparams(sem):
    return pltpu.CompilerParams(dimension_semantics=sem, vmem_limit_bytes=VMEM_LIMIT)


def _sigmoid(v):
    return 0.5 * jnp.tanh(0.5 * v) + 0.5


def _ada_norm(x, g, mod_l, mod_c, first_row, which):
    y = x * lax.rsqrt(jnp.mean(x * x, axis=-1, keepdims=True) + EPS) * g
    sh_l, sc_l = _mod_slice(mod_l, 3 * which), _mod_slice(mod_l, 3 * which + 1)
    sh_c, sc_c = _mod_slice(mod_c, 3 * which), _mod_slice(mod_c, 3 * which + 1)
    is_ctx = _is_ctx(x.shape[0], first_row)
    return y * (1.0 + jnp.where(is_ctx, sc_c, sc_l)) + jnp.where(is_ctx, sh_c, sh_l)


def _mod_slice(mod, k):
    return mod[:, k * D_MODEL:(k + 1) * D_MODEL]


def _is_ctx(rows, first_row):
    return (lax.broadcasted_iota(jnp.int32, (rows, 1), 0) + first_row) < CTX_LEN


def _mod_kernel(src_ref, w_ref, b_ref, o_ref):
    s = src_ref[...]
    s = (s * _sigmoid(s)).astype(BF16)
    o_ref[0] = jnp.dot(s, w_ref[0].astype(BF16), preferred_element_type=F32) + b_ref[0]


def modulation(src, mod_w, mod_b):
    rows = src.shape[0]
    tn = 1536
    return pl.pallas_call(
        _mod_kernel,
        out_shape=jax.ShapeDtypeStruct((DEPTH, rows, 6 * D_MODEL), F32),
        grid=(DEPTH, 6 * D_MODEL // tn),
        in_specs=[pl.BlockSpec((rows, D_MODEL), lambda l, n: (0, 0)),
                  pl.BlockSpec((1, D_MODEL, tn), lambda l, n: (l, 0, n)),
                  pl.BlockSpec((1, 1, tn), lambda l, n: (l, 0, n))],
        out_specs=pl.BlockSpec((1, rows, tn), lambda l, n: (l, 0, n)),
        compiler_params=_cparams(("arbitrary", "arbitrary")),
        name="modulation",
    )(src, mod_w, mod_b.reshape(DEPTH, 1, 6 * D_MODEL))


IN_TM = 1152
IN_TN = 1024


def _in_kernel(x_ref, g_ref, ml_ref, mc_ref, w_ref, b_ref, ws_ref, bs_ref, o_ref, os_ref, h_ref):
    n = pl.program_id(2)

    @pl.when(n == 0)
    def _():
        h = _ada_norm(x_ref[0], g_ref[...], ml_ref[0], mc_ref[0], pl.program_id(1) * IN_TM, 0)
        h_ref[...] = h.astype(BF16)
        os_ref[0] = jnp.dot(h_ref[...], ws_ref[...], preferred_element_type=F32) + bs_ref[...]

    o_ref[0] = (jnp.dot(h_ref[...], w_ref[...], preferred_element_type=F32) + b_ref[...]).astype(o_ref.dtype)


def in_projection(xs, g, mod, w_big, b_big, w_small, b_small):
    bsz = xs.shape[0]
    mod3 = mod.reshape(bsz + 1, 1, 6 * D_MODEL)
    return pl.pallas_call(
        _in_kernel,
        out_shape=(jax.ShapeDtypeStruct((bsz, S_ALL, N_BIG), BF16),
                   jax.ShapeDtypeStruct((bsz, S_ALL, N_SMALL), F32)),
        grid=(bsz, S_ALL // IN_TM, N_BIG // IN_TN),
        in_specs=[pl.BlockSpec((1, IN_TM, D_MODEL), lambda b, s, n: (b, s, 0)),
                  pl.BlockSpec((1, D_MODEL), lambda b, s, n: (0, 0)),
                  pl.BlockSpec((1, 1, 6 * D_MODEL), lambda b, s, n: (b, 0, 0)),
                  pl.BlockSpec((1, 1, 6 * D_MODEL), lambda b, s, n: (bsz, 0, 0)),
                  pl.BlockSpec((D_MODEL, IN_TN), lambda b, s, n: (0, n)),
                  pl.BlockSpec((1, IN_TN), lambda b, s, n: (0, n)),
                  pl.BlockSpec((D_MODEL, N_SMALL), lambda b, s, n: (0, 0)),
                  pl.BlockSpec((1, N_SMALL), lambda b, s, n: (0, 0))],
        out_specs=(pl.BlockSpec((1, IN_TM, IN_TN), lambda b, s, n: (b, s, n)),
                   pl.BlockSpec((1, IN_TM, N_SMALL), lambda b, s, n: (b, s, 0))),
        scratch_shapes=[pltpu.VMEM((IN_TM, D_MODEL), BF16)],
        compiler_params=_cparams(("arbitrary", "arbitrary", "arbitrary")),
        name="in_projection",
    )(xs, g.reshape(1, D_MODEL), mod3, mod3, w_big, b_big, w_small, b_small)


MERGE_TM = 768


def _merge_kernel(ym_ref, yg_ref, yr_ref, g0_ref, g1_ref, g2_ref, x_ref, ml_ref, mc_ref, wb_ref, wo_ref, o_ref):
    acc = None
    for n, (y_ref, gp_ref) in enumerate(((ym_ref, g0_ref), (yg_ref, g1_ref), (yr_ref, g2_ref))):
        p = jnp.dot(y_ref[0].astype(BF16), wb_ref[n], preferred_element_type=F32)
        p = p * _sigmoid(gp_ref[0].astype(F32))
        acc = p if acc is None else acc + p
    out = jnp.dot(acc.astype(BF16), wo_ref[...], preferred_element_type=F32)
    is_ctx = _is_ctx(MERGE_TM, pl.program_id(1) * MERGE_TM)
    gt = jnp.where(is_ctx, _mod_slice(mc_ref[0], 2), _mod_slice(ml_ref[0], 2))
    o_ref[0] = x_ref[0] + gt * out


def merge_branches(ym, yg, yr, big, xs, mod, w_branch, w_out):
    bsz = xs.shape[0]
    mod3 = mod.reshape(bsz + 1, 1, 6 * D_MODEL)
    row = lambda b, s: (b, s, 0)
    gate_blk = C_MG // D_MODEL
    const = pl.Buffered(1)
    return pl.pallas_call(
        _merge_kernel,
        out_shape=jax.ShapeDtypeStruct(xs.shape, F32),
        grid=(bsz, S_ALL // MERGE_TM),
        in_specs=[pl.BlockSpec((1, MERGE_TM, D_MODEL), row),
                  pl.BlockSpec((1, MERGE_TM, D_MODEL), row),
                  pl.BlockSpec((1, MERGE_TM, D_MODEL), row),
                  pl.BlockSpec((1, MERGE_TM, D_MODEL), lambda b, s: (b, s, gate_blk)),
                  pl.BlockSpec((1, MERGE_TM, D_MODEL), lambda b, s: (b, s, gate_blk + 1)),
                  pl.BlockSpec((1, MERGE_TM, D_MODEL), lambda b, s: (b, s, gate_blk + 2)),
                  pl.BlockSpec((1, MERGE_TM, D_MODEL), row),
                  pl.BlockSpec((1, 1, 6 * D_MODEL), lambda b, s: (b, 0, 0)),
                  pl.BlockSpec((1, 1, 6 * D_MODEL), lambda b, s: (bsz, 0, 0)),
                  pl.BlockSpec((N_BRANCH, BRANCH_W, D_MODEL), lambda b, s: (0, 0, 0), pipeline_mode=const),
                  pl.BlockSpec((D_MODEL, D_MODEL), lambda b, s: (0, 0), pipeline_mode=const)],
        out_specs=pl.BlockSpec((1, MERGE_TM, D_MODEL), row),
        compiler_params=_cparams(("arbitrary", "arbitrary")),
        name="merge_branches",
    )(ym, yg, yr, big, big, big, xs, mod3, mod3, w_branch, w_out)


FFN_TM = 768
FFN_TF = 256


def _swiglu_chunk(h, w1, w3, w2):
    a = jnp.dot(h, w1, preferred_element_type=F32)
    b = jnp.dot(h, w3, preferred_element_type=F32)
    return jnp.dot((a * _sigmoid(a) * b).astype(BF16), w2, preferred_element_type=F32)


def _ffn_kernel(x_ref, g_ref, ml_ref, mc_ref, w1_ref, w3_ref, w2_ref, o_ref):
    first_row = pl.program_id(1) * FFN_TM
    h = _ada_norm(x_ref[0], g_ref[...], ml_ref[0], mc_ref[0], first_row, 1).astype(BF16)
    for f in range(D_FF // FFN_TF):
        cols = slice(f * FFN_TF, (f + 1) * FFN_TF)
        y = _swiglu_chunk(h, w1_ref[:, cols], w3_ref[:, cols], w2_ref[cols, :])
        if f == 0:
            o_ref[0] = y
        else:
            o_ref[0] += y
    gt = jnp.where(_is_ctx(FFN_TM, first_row), _mod_slice(mc_ref[0], 5), _mod_slice(ml_ref[0], 5))
    o_ref[0] = x_ref[0] + gt * o_ref[0]


def dense_ffn(xs, g, mod, w1, w3, w2):
    bsz = xs.shape[0]
    mod3 = mod.reshape(bsz + 1, 1, 6 * D_MODEL)
    const = pl.Buffered(1)
    return pl.pallas_call(
        _ffn_kernel,
        out_shape=jax.ShapeDtypeStruct(xs.shape, F32),
        grid=(bsz, S_ALL // FFN_TM),
        in_specs=[pl.BlockSpec((1, FFN_TM, D_MODEL), lambda b, s: (b, s, 0)),
                  pl.BlockSpec((1, D_MODEL), lambda b, s: (0, 0)),
                  pl.BlockSpec((1, 1, 6 * D_MODEL), lambda b, s: (b, 0, 0)),
                  pl.BlockSpec((1, 1, 6 * D_MODEL), lambda b, s: (bsz, 0, 0)),
                  pl.BlockSpec((D_MODEL, D_FF), lambda b, s: (0, 0), pipeline_mode=const),
                  pl.BlockSpec((D_MODEL, D_FF), lambda b, s: (0, 0), pipeline_mode=const),
                  pl.BlockSpec((D_FF, D_MODEL), lambda b, s: (0, 0), pipeline_mode=const)],
        out_specs=pl.BlockSpec((1, FFN_TM, D_MODEL), lambda b, s: (b, s, 0)),
        compiler_params=_cparams(("arbitrary", "arbitrary")),
        name="dense_ffn",
    )(xs, g.reshape(1, D_MODEL), mod3, mod3, w1, w3, w2)


NORM_TM = 768


def _norm_router_kernel(x_ref, g_ref, ml_ref, mc_ref, rw_ref, rb_ref, h_ref, lg_ref):
    h = _ada_norm(x_ref[0], g_ref[...], ml_ref[0], mc_ref[0], pl.program_id(1) * NORM_TM, 1)
    h_ref[0] = h.astype(h_ref.dtype)
    lg_ref[0] = jnp.dot(h, rw_ref[...], preferred_element_type=F32, precision=lax.Precision.HIGHEST) + rb_ref[...]


def norm_router(xs, g, mod, router_w, router_b):
    bsz = xs.shape[0]
    mod3 = mod.reshape(bsz + 1, 1, 6 * D_MODEL)
    rw = jnp.zeros((D_MODEL, LANE), F32).at[:, :N_EXPERTS].set(router_w)
    rb = jnp.zeros((1, LANE), F32).at[0, :N_EXPERTS].set(router_b)
    return pl.pallas_call(
        _norm_router_kernel,
        out_shape=(jax.ShapeDtypeStruct(xs.shape, BF16), jax.ShapeDtypeStruct((bsz, S_ALL, LANE), F32)),
        grid=(bsz, S_ALL // NORM_TM),
        in_specs=[pl.BlockSpec((1, NORM_TM, D_MODEL), lambda b, s: (b, s, 0)),
                  pl.BlockSpec((1, D_MODEL), lambda b, s: (0, 0)),
                  pl.BlockSpec((1, 1, 6 * D_MODEL), lambda b, s: (b, 0, 0)),
                  pl.BlockSpec((1, 1, 6 * D_MODEL), lambda b, s: (bsz, 0, 0)),
                  pl.BlockSpec((D_MODEL, LANE), lambda b, s: (0, 0)),
                  pl.BlockSpec((1, LANE), lambda b, s: (0, 0))],
        out_specs=(pl.BlockSpec((1, NORM_TM, D_MODEL), lambda b, s: (b, s, 0)),
                   pl.BlockSpec((1, NORM_TM, LANE), lambda b, s: (b, s, 0))),
        compiler_params=_cparams(("arbitrary", "arbitrary")),
        name="norm_router",
    )(xs, g.reshape(1, D_MODEL), mod3, mod3, rw, rb)


def _expert_kernel(be_ref, x_ref, w1_ref, w3_ref, w2_ref, o_ref):
    del be_ref
    x = x_ref[...]
    for f in range(D_FF // FFN_TF):
        cols = slice(f * FFN_TF, (f + 1) * FFN_TF)
        y = _swiglu_chunk(x, w1_ref[0, :, cols], w3_ref[0, :, cols], w2_ref[0, cols, :])
        if f == 0:
            o_ref[...] = y
        else:
            o_ref[...] += y


def grouped_experts(xb, block_e, w1, w3, w2):
    n_rows = xb.shape[0]
    const = pl.Buffered(1)
    grid_spec = pltpu.PrefetchScalarGridSpec(
        num_scalar_prefetch=1,
        grid=(n_rows // MOE_BLOCK,),
        in_specs=[pl.BlockSpec((MOE_BLOCK, D_MODEL), lambda i, be: (i, 0)),
                  pl.BlockSpec((1, D_MODEL, D_FF), lambda i, be: (be[i], 0, 0), pipeline_mode=const),
                  pl.BlockSpec((1, D_MODEL, D_FF), lambda i, be: (be[i], 0, 0), pipeline_mode=const),
                  pl.BlockSpec((1, D_FF, D_MODEL), lambda i, be: (be[i], 0, 0), pipeline_mode=const)],
        out_specs=pl.BlockSpec((MOE_BLOCK, D_MODEL), lambda i, be: (i, 0)),
    )
    return pl.pallas_call(
        _expert_kernel,
        out_shape=jax.ShapeDtypeStruct((n_rows, D_MODEL), F32),
        grid_spec=grid_spec,
        compiler_params=_cparams(("arbitrary",)),
        name="grouped_experts",
    )(block_e, xb, w1, w3, w2)


def moe_ffn(xs, g, mod, gt2, router_w, router_b, w1, w3, w2):
    bsz = xs.shape[0]
    h, logits = norm_router(xs, g, mod, router_w, router_b)
    h = h.reshape(-1, D_MODEL)
    n_tok = h.shape[0]
    top_logit, top_e = lax.top_k(logits.reshape(n_tok, LANE)[:, :N_EXPERTS], TOP_K)
    gate = jax.nn.softmax(top_logit, axis=-1)
    n_assign = n_tok * TOP_K
    flat_e = top_e.reshape(-1)
    flat_tok = jnp.repeat(jnp.arange(n_tok, dtype=jnp.int32), TOP_K)
    onehot = (flat_e[:, None] == jnp.arange(N_EXPERTS, dtype=flat_e.dtype)[None, :]).astype(jnp.int32)
    seen = jnp.cumsum(onehot, axis=0)
    padded = (seen[-1] + MOE_BLOCK - 1) // MOE_BLOCK * MOE_BLOCK
    p_end = jnp.cumsum(padded)
    slot = jnp.sum(onehot * ((p_end - padded)[None, :] + seen - 1), axis=1)
    n_blocks = -(-n_assign // MOE_BLOCK) + N_EXPERTS
    n_rows = n_blocks * MOE_BLOCK
    row_tok = jnp.zeros((n_rows,), jnp.int32).at[slot].set(flat_tok)
    slot = slot.reshape(n_tok, TOP_K)
    block_e = jnp.minimum(jnp.searchsorted(p_end, jnp.arange(n_blocks) * MOE_BLOCK, side='right'),
                          N_EXPERTS - 1).astype(jnp.int32)
    yb = grouped_experts(h[row_tok], block_e, w1, w3, w2)
    y = sum(yb[slot[:, k]] * gate[:, k:k + 1] for k in range(TOP_K))
    return xs + gt2 * y.reshape(bsz, S_ALL, D_MODEL)


FINAL_TM = 256


def _final_kernel(x_ref, g_ref, o_ref):
    x = x_ref[0]
    o_ref[0] = x * lax.rsqrt(jnp.mean(x * x, axis=-1, keepdims=True) + EPS) * g_ref[...]


def final_norm(xs, g):
    bsz = xs.shape[0]
    skip = CTX_LEN // FINAL_TM
    return pl.pallas_call(
        _final_kernel,
        out_shape=jax.ShapeDtypeStruct((bsz, SEQ, D_MODEL), F32),
        grid=(bsz, SEQ // FINAL_TM),
        in_specs=[pl.BlockSpec((1, FINAL_TM, D_MODEL), lambda b, s: (b, s + skip, 0)),
                  pl.BlockSpec((1, D_MODEL), lambda b, s: (0, 0))],
        out_specs=pl.BlockSpec((1, FINAL_TM, D_MODEL), lambda b, s: (b, s, 0)),
        compiler_params=_cparams(("arbitrary", "arbitrary")),
        name="final_norm",
    )(xs, g.reshape(1, D_MODEL))


N_CHUNKS = S_ALL // CHUNK
CTX_CHUNKS = CTX_LEN // CHUNK
GATE_ROWS = 8
NEG_INF = float("-inf")


def _softplus(v):
    return jnp.maximum(v, 0.0) + jnp.log(1.0 + jnp.exp(-jnp.abs(v)))


def _scan_constants():
    s = np.arange(CHUNK)[:, None]
    t = np.arange(CHUNK)[None, :]
    tri = np.concatenate([(s <= t), (s >= t), np.ones((CHUNK, CHUNK), bool)], axis=1).astype(np.float32)
    return jnp.asarray(tri), jnp.asarray(np.eye(CHUNK, dtype=np.float32))


def _rev_chunk(j):
    return jnp.where(j < CTX_CHUNKS, CTX_CHUNKS - 1 - j, N_CHUNKS + CTX_CHUNKS - 1 - j)


def _head_gate_rows(small, c0, heads, kind_major):
    bsz = small.shape[0]
    a = small[:, :, c0:c0 + 4 * heads].reshape(bsz, N_CHUNKS, CHUNK, 2, 2, heads)
    a = a.transpose((0, 5, 1, 4, 3, 2) if kind_major else (0, 5, 1, 3, 4, 2))
    a = a.reshape(bsz, heads, N_CHUNKS, 4, CHUNK)
    return jnp.pad(a, ((0, 0), (0, 0), (0, 0), (0, GATE_ROWS - 4), (0, 0)))


def _tn_dot(a, b):
    return lax.dot_general(a, b, (((0,), (0,)), ((), ())), preferred_element_type=F32)


def _nt_dot(a, b, precision=None):
    return lax.dot_general(a, b, (((1,), (1,)), ((), ())), preferred_element_type=F32, precision=precision)


def _dir_mask(d):
    t = lax.broadcasted_iota(jnp.int32, (CHUNK, CHUNK), 0)
    s = lax.broadcasted_iota(jnp.int32, (CHUNK, CHUNK), 1)
    return (s <= t, s < t) if d == 0 else (s >= t, s > t)


def _split_dot(a, ones, dot_fn):
    b = ones.astype(BF16)
    hi = a.astype(BF16)
    lo = (a - hi.astype(F32)).astype(BF16)
    return dot_fn(hi, b, preferred_element_type=F32) + dot_fn(lo, b, preferred_element_type=F32)


def _gate_forms(vals, tri, eye):
    cum = _split_dot(vals, tri, jnp.dot)
    kind = lax.broadcasted_iota(jnp.int32, vals.shape, 0) & (GATE_ROWS - 1)
    rows = jnp.where(kind == 1, cum[:, :CHUNK], jnp.where(kind == 3, cum[:, CHUNK:2 * CHUNK], vals))
    cols = _split_dot(rows, eye, lambda a, b, **kw: _nt_dot(b, a))
    return rows, cols, cum[:, 2 * CHUNK:2 * CHUNK + 1]


def _x_row(rows, d, g=0):
    return rows[GATE_ROWS * g + 2 * d:GATE_ROWS * g + 2 * d + 1]


def _run_row(rows, d, g=0):
    return rows[GATE_ROWS * g + 2 * d + 1:GATE_ROWS * g + 2 * d + 2]


def _x_col(cols, d, g=0):
    return cols[:, GATE_ROWS * g + 2 * d:GATE_ROWS * g + 2 * d + 1]


def _run_col(cols, d, g=0):
    return cols[:, GATE_ROWS * g + 2 * d + 1:GATE_ROWS * g + 2 * d + 2]


PAIRS_PER_STEP = 6
OUT_PAIRS_PER_STEP = 1


def _mlstm_kernel(q_ref, k_ref, v_ref, o_ref, g_ref, fb_ref, ng_ref, tri_ref, eye_ref, y_ref,
                  c_s, st_s, gr_s, gc_s):
    tri, eye = tri_ref[...], eye_ref[...]
    fbias = fb_ref[0]
    scale = M_DQK ** -0.5
    kind = lax.broadcasted_iota(jnp.int32, (2 * GATE_ROWS, CHUNK), 0) & 1
    fbias2 = jnp.concatenate([fbias, fbias], axis=0)
    st_row = lax.broadcasted_iota(jnp.int32, (GATE_ROWS, M_DQK), 0)

    def pair_rows(c0):
        return pl.ds(pl.multiple_of(c0 * CHUNK, 2 * CHUNK), 2 * CHUNK)

    def state_rows(c):
        return pl.ds(pl.multiple_of(c * M_DQK, M_DQK), M_DQK)

    def pair_state_rows(c0):
        return pl.ds(pl.multiple_of(c0 * M_DQK, 2 * M_DQK), 2 * M_DQK)

    def local(it, carry):
        pairs = [it * PAIRS_PER_STEP + p for p in range(PAIRS_PER_STEP)]
        forms = []
        for g in pairs:
            raw = jnp.concatenate([g_ref[0, 0, 2 * g], g_ref[0, 0, 2 * g + 1]], axis=0)
            vals = jnp.where(kind == 1, -_softplus(-(raw + fbias2)), raw)
            forms.append(_gate_forms(vals, tri, eye))
        zero = jnp.zeros((CHUNK, 2 * M_DQK), F32)
        lhs_all = []
        for g, (rows, cols, totals) in zip(pairs, forms):
            gr_s[g] = rows
            gc_s[g] = cols
            kf2 = k_ref[0, pair_rows(2 * g), :].astype(F32)
            lhs = []
            for i in range(2):
                kf = kf2[i * CHUNK:(i + 1) * CHUNK]
                kws = []
                for d in range(2):
                    tot = totals[GATE_ROWS * i + 2 * d + 1:GATE_ROWS * i + 2 * d + 2]
                    m_loc = jnp.max(tot - _run_row(rows, d, i) + _x_row(rows, d, i), axis=-1, keepdims=True)
                    kw = kf * (jnp.exp(tot - _run_col(cols, d, i) + _x_col(cols, d, i) - m_loc) * scale)
                    kws.append(kw)
                    st_s[d, 2 * g + i] = jnp.where(st_row == 0, jnp.sum(kw, axis=0, keepdims=True),
                                                   jnp.where(st_row == 1, m_loc, tot))
                lhs.append(jnp.concatenate(kws + [zero] if i == 0 else [zero] + kws, axis=1))
            lhs_all.append(jnp.concatenate(lhs, axis=0).astype(BF16))
        for g, lhs in zip(pairs, lhs_all):
            c_loc = _tn_dot(lhs, v_ref[0, pair_rows(2 * g), :])
            for i in range(2):
                for d in range(2):
                    c_s[d, state_rows(2 * g + i), :] = c_loc[(2 * i + d) * M_DQK:(2 * i + d + 1) * M_DQK]
        return carry

    lax.fori_loop(0, N_CHUNKS // (2 * PAIRS_PER_STEP), local, 0)

    def scan(d):
        def body(j, carry):
            c_prev, n_prev, m_prev = carry
            c = j if d == 0 else _rev_chunk(j)
            st = st_s[d, c]
            n_loc, m_loc, tot = st[0:1], st[1:2, :1], st[2:3, :1]
            c_loc = c_s[d, state_rows(c), :]
            c_s[d, state_rows(c), :] = c_prev
            st_s[d, c] = jnp.where(st_row == 0, n_prev, m_prev)
            m_new = jnp.maximum(tot + m_prev, m_loc)
            s_prev = jnp.exp(tot + m_prev - m_new)
            s_loc = jnp.exp(m_loc - m_new)
            return s_prev * c_prev + s_loc * c_loc, s_prev * n_prev + s_loc * n_loc, m_new

        lax.fori_loop(0, N_CHUNKS, body, (jnp.zeros((M_DQK, M_DV), F32), jnp.zeros((1, M_DQK), F32),
                                          jnp.zeros((1, 1), F32)))

    scan(0)
    scan(1)

    def output(it, carry):
        pairs = [it * OUT_PAIRS_PER_STEP + p for p in range(OUT_PAIRS_PER_STEP)]
        qk_all = [_nt_dot(q_ref[0, pair_rows(2 * g), :], k_ref[0, pair_rows(2 * g), :]) * scale for g in pairs]
        z_intra = jnp.zeros((CHUNK, 2 * CHUNK), F32)
        z_inter = jnp.zeros((CHUNK, M_DQK), F32)
        operands = []
        for g, qk2 in zip(pairs, qk_all):
            c0 = 2 * g
            qf2, v2 = q_ref[0, pair_rows(c0), :].astype(F32), v_ref[0, pair_rows(c0), :]
            rows, cols = gr_s[g], gc_s[g]
            lhs_intra, lhs_inter, rhs_v = [], [], []
            for i in range(2):
                qk = qk2[i * CHUNK:(i + 1) * CHUNK, i * CHUNK:(i + 1) * CHUNK]
                qf = qf2[i * CHUNK:(i + 1) * CHUNK]
                intra, inter = [], []
                for d in range(2):
                    st = st_s[d, c0 + i]
                    b_c = _run_col(cols, d, i)
                    n0, m0 = st[0:1], st[1:2, :1]
                    incl, _ = _dir_mask(d)
                    d_log = jnp.where(incl, b_c - _run_row(rows, d, i) + _x_row(rows, d, i), NEG_INF)
                    m_inter = b_c + m0
                    m_comb = jnp.maximum(m_inter, jnp.max(d_log, axis=-1, keepdims=True))
                    s = qk * jnp.exp(d_log - m_comb)
                    e_inter = jnp.exp(m_inter - m_comb)
                    den = (jnp.sum(s, axis=-1, keepdims=True)
                           + e_inter * jnp.sum(qf * n0, axis=-1, keepdims=True))
                    inv = 1.0 / jnp.maximum(jnp.abs(den), jnp.exp(-m_comb))
                    intra.append(s * inv)
                    inter.append(qf * (e_inter * inv))
                pair = jnp.concatenate(intra, axis=1)
                lhs_intra.append(jnp.concatenate([pair, z_intra] if i == 0 else [z_intra, pair], axis=1))
                lhs_inter.append(jnp.concatenate([inter[0], z_inter, inter[1], z_inter] if i == 0
                                                 else [z_inter, inter[0], z_inter, inter[1]], axis=1))
                v = v2[i * CHUNK:(i + 1) * CHUNK]
                rhs_v += [v, v]
            c_in = jnp.concatenate([c_s[0, pair_state_rows(c0), :], c_s[1, pair_state_rows(c0), :]], axis=0)
            operands.append((jnp.concatenate(lhs_intra, axis=0).astype(BF16), jnp.concatenate(rhs_v, axis=0),
                             jnp.concatenate(lhs_inter, axis=0).astype(BF16), c_in.astype(BF16)))
        hs = [jnp.dot(li, rv, preferred_element_type=F32) + jnp.dot(le, ci, preferred_element_type=F32)
              for li, rv, le, ci in operands]
        for g, h in zip(pairs, hs):
            h = h * lax.rsqrt(jnp.mean(h * h, axis=-1, keepdims=True) + EPS) * ng_ref[...]
            y_ref[0, pair_rows(2 * g), :] = (h * _sigmoid(o_ref[0, pair_rows(2 * g), :].astype(F32))
                                             ).astype(y_ref.dtype)
        return carry

    lax.fori_loop(0, N_CHUNKS // (2 * OUT_PAIRS_PER_STEP), output, 0)


def mlstm_mixer(big, small, f_bias, norm_g):
    bsz = big.shape[0]
    gates = _head_gate_rows(small, 0, M_HEADS, False)
    fb = jnp.zeros((M_HEADS, GATE_ROWS, CHUNK), F32)
    fb = fb.at[:, 1].set(f_bias[0][:, None]).at[:, 3].set(f_bias[1][:, None])
    tri, eye = _scan_constants()
    qb, kb, vb, ob = C_MQ // M_DQK, C_MK // M_DQK, C_MV // M_DV, C_MO // M_DV
    return pl.pallas_call(
        _mlstm_kernel,
        out_shape=jax.ShapeDtypeStruct((bsz, S_ALL, BRANCH_W), BF16),
        grid=(bsz, M_HEADS),
        in_specs=[pl.BlockSpec((1, S_ALL, M_DQK), lambda b, h: (b, 0, qb + h)),
                  pl.BlockSpec((1, S_ALL, M_DQK), lambda b, h: (b, 0, kb + h)),
                  pl.BlockSpec((1, S_ALL, M_DV), lambda b, h: (b, 0, vb + h)),
                  pl.BlockSpec((1, S_ALL, M_DV), lambda b, h: (b, 0, ob + h)),
                  pl.BlockSpec((1, 1, N_CHUNKS, GATE_ROWS, CHUNK), lambda b, h: (b, h, 0, 0, 0)),
                  pl.BlockSpec((1, GATE_ROWS, CHUNK), lambda b, h: (h, 0, 0)),
                  pl.BlockSpec((1, M_DV), lambda b, h: (0, h)),
                  pl.BlockSpec((CHUNK, 3 * CHUNK), lambda b, h: (0, 0)),
                  pl.BlockSpec((CHUNK, CHUNK), lambda b, h: (0, 0))],
        out_specs=pl.BlockSpec((1, S_ALL, M_DV), lambda b, h: (b, 0, h)),
        scratch_shapes=[pltpu.VMEM((2, N_CHUNKS * M_DQK, M_DV), F32),
                        pltpu.VMEM((2, N_CHUNKS, GATE_ROWS, M_DQK), F32),
                        pltpu.VMEM((N_CHUNKS // 2, 2 * GATE_ROWS, CHUNK), F32),
                        pltpu.VMEM((N_CHUNKS // 2, CHUNK, 2 * GATE_ROWS), F32)],
        compiler_params=_cparams(("arbitrary", "arbitrary")),
        name="mlstm_mixer",
    )(big, big, big, big, gates, fb, norm_g.reshape(1, BRANCH_W), tri, eye)


CONV_BLK = 256
CONV_HALO = 16
SOLVE_BLK = 2
SOLVE_GROUP = 4
PREP_GROUPS = 3


def _conv_silu(src_ref, w, dst_ref, l2_scale):
    n_blk = S_ALL // CONV_BLK
    zeros = jnp.zeros((CONV_HALO, src_ref.shape[-1]), F32)
    for i in range(n_blk):
        lo = i * CONV_BLK
        first = i == 0 or lo == CTX_LEN
        last = lo + CONV_BLK in (CTX_LEN, S_ALL)
        body = src_ref[0, lo - (0 if first else CONV_HALO):lo + CONV_BLK + (0 if last else CONV_HALO), :]
        win = jnp.concatenate(([zeros] if first else []) + [body.astype(F32)] + ([zeros] if last else []), axis=0)
        n = win.shape[0]
        y = None
        for j in range(CONV_W):
            tap = win if j == CONV_W // 2 else pltpu.roll(win, (CONV_W // 2 - j) % n, 0)
            term = tap[CONV_HALO:CONV_HALO + CONV_BLK] * w[j:j + 1, :]
            y = term if y is None else y + term
        y = y * _sigmoid(y)
        if l2_scale is not None:
            y = y * (lax.rsqrt(jnp.sum(y * y, axis=-1, keepdims=True) + EPS) * l2_scale)
        dst_ref[lo:lo + CONV_BLK, :] = y.astype(dst_ref.dtype)


def _block_masks():
    n = SOLVE_GROUP * CHUNK
    t = np.arange(n)[:, None]
    s = np.arange(n)[None, :]
    masks = [(t // SOLVE_BLK) == (s // SOLVE_BLK)]
    k = SOLVE_BLK
    while k < CHUNK:
        masks.append(((t // (2 * k)) == (s // (2 * k))) & ((t // k) != (s // k)))
        k *= 2
    return jnp.asarray(np.stack(masks).astype(np.float32), BF16)


def _unit_triangular_inverse(systems, eye, masks):
    abs_ = [a.astype(BF16) for a in systems]
    ts = [eye - (ab * masks[0]).astype(F32) for ab in abs_]
    for lvl in range(1, masks.shape[0]):
        tbs = [t.astype(BF16) for t in ts]
        tls = [jnp.dot(tb, ab * masks[lvl], preferred_element_type=F32) for tb, ab in zip(tbs, abs_)]
        ts = [t - jnp.dot(tl.astype(BF16), tb, preferred_element_type=F32) for t, tl, tb in zip(ts, tls, tbs)]
    return ts


def _block_diag(blocks):
    n = len(blocks)
    z_half = jnp.zeros((CHUNK, CHUNK), F32)
    z_tile = jnp.zeros((CHUNK, 2 * CHUNK), F32)
    out = []
    for p, blk in enumerate(blocks):
        pair = jnp.concatenate([blk, z_half] if p % 2 == 0 else [z_half, blk], axis=1)
        out.append(jnp.concatenate([pair if t == p // 2 else z_tile for t in range(n // 2)], axis=1))
    return jnp.concatenate(out, axis=0)


def _gdn_kernel(xq_ref, xk_ref, xv_ref, z_ref, g_ref, gp_ref, cw_ref, ng_ref, tri_ref, eye_ref, bm_ref, y_ref,
                q_s, k_s, v_s, u_s, wq_s, at_s, kt_s, cd_s, acc_ref):
    tri, eye, masks = tri_ref[...], eye_ref[...], bm_ref[...]
    cw = cw_ref[0]
    _conv_silu(xq_ref, cw[:, :G_DK], q_s, G_DK ** -0.5)
    _conv_silu(xk_ref, cw[:, G_DK:2 * G_DK], k_s, 1.0)
    _conv_silu(xv_ref, cw[:, 2 * G_DK:], v_s, None)
    n_pair = SOLVE_GROUP // 2
    kind = lax.broadcasted_iota(jnp.int32, (n_pair * GATE_ROWS, CHUNK), 0) & 1
    neg_a = jnp.concatenate([gp_ref[0, 0]] * n_pair, axis=0)
    dt_bias = jnp.concatenate([gp_ref[0, 1]] * n_pair, axis=0)
    n_bd = SOLVE_GROUP * CHUNK
    eye_bd = (lax.broadcasted_iota(jnp.int32, (n_bd, n_bd), 0)
              == lax.broadcasted_iota(jnp.int32, (n_bd, n_bd), 1)).astype(F32)
    z_half = jnp.zeros((CHUNK, CHUNK), F32)

    def chunk_rows(c):
        return pl.ds(pl.multiple_of(c * CHUNK, CHUNK), CHUNK)

    def wq_rows(c):
        return pl.ds(pl.multiple_of(c * 2 * CHUNK, 2 * CHUNK), 2 * CHUNK)

    def prep(g, carry):
        systems, rhs_all, q_dec_all = [], [], []
        for sgrp in range(PREP_GROUPS):
            c0 = (g * PREP_GROUPS + sgrp) * n_pair
            raw = jnp.concatenate([g_ref[0, 0, c0 + i] for i in range(n_pair)], axis=0)
            vals = jnp.where(kind == 1, neg_a * _softplus(raw + dt_bias), _sigmoid(raw))
            g_rows, g_cols, totals = _gate_forms(vals, tri, eye)
            blocks, rhs, q_dec = [], [], []
            for i in range(n_pair):
                c = c0 + i
                q, k, v = q_s[chunk_rows(c), :], k_s[chunk_rows(c), :], v_s[chunk_rows(c), :]
                kf, vf, qf = k.astype(F32), v.astype(F32), q.astype(F32)
                kq = _nt_dot(jnp.concatenate([k, q], axis=0), k)
                kk, qk = kq[:CHUNK], kq[CHUNK:]
                for d in range(2):
                    gam_r, beta_c, gam_c = _run_row(g_rows, d, i), _x_col(g_cols, d, i), _run_col(g_cols, d, i)
                    tot = totals[GATE_ROWS * i + 2 * d + 1:GATE_ROWS * i + 2 * d + 2]
                    incl, strict = _dir_mask(d)
                    decay = jnp.exp(jnp.where(incl, gam_c - gam_r, NEG_INF))
                    blocks.append(jnp.where(strict, kk * beta_c * decay, 0.0))
                    e_gam = jnp.exp(gam_c)
                    rhs.append(jnp.concatenate([vf * beta_c, kf * (beta_c * e_gam)], axis=-1))
                    q_dec.append(qf * e_gam)
                    attn = qk * decay
                    at_s[d, chunk_rows(c), :] = jnp.concatenate([attn, z_half] if d == 0 else [z_half, attn],
                                                                axis=1).astype(BF16)
                    kt_s[d, chunk_rows(c), :] = (kf * jnp.exp(tot - gam_c)).astype(BF16)
                    cd_s[d, c] = jnp.broadcast_to(jnp.exp(tot), cd_s.shape[2:])
            systems.append(_block_diag(blocks))
            rhs_all.append(jnp.concatenate(rhs, axis=0).astype(BF16))
            q_dec_all.append(q_dec)
        t_invs = _unit_triangular_inverse(systems, eye_bd, masks)
        for sgrp in range(PREP_GROUPS):
            c0 = (g * PREP_GROUPS + sgrp) * n_pair
            x = jnp.dot(t_invs[sgrp].astype(BF16), rhs_all[sgrp], preferred_element_type=F32)
            for i in range(n_pair):
                for d in range(2):
                    p = 2 * i + d
                    xp = x[p * CHUNK:(p + 1) * CHUNK]
                    u_s[d, chunk_rows(c0 + i), :] = xp[:, :G_DV]
                    wq_s[d, wq_rows(c0 + i), :] = jnp.concatenate([xp[:, G_DV:], q_dec_all[sgrp][p]],
                                                                  axis=0).astype(BF16)
        return carry

    lax.fori_loop(0, N_CHUNKS // (n_pair * PREP_GROUPS), prep, 0)

    z_v = jnp.zeros((CHUNK, G_DV), BF16)

    def step(j, s_prev):
        cs = (j, _rev_chunk(j))
        ws_qs = [jnp.dot(wq_s[d, wq_rows(cs[d]), :], s_prev[d].astype(BF16), preferred_element_type=F32)
                 for d in range(2)]
        v_new = [(u_s[d, chunk_rows(cs[d]), :] - ws_qs[d][:CHUNK]).astype(BF16) for d in range(2)]
        o = [jnp.dot(at_s[d, chunk_rows(cs[d]), :],
                     jnp.concatenate([v_new[d], z_v] if d == 0 else [z_v, v_new[d]], axis=0),
                     preferred_element_type=F32) for d in range(2)]
        ktv = [_tn_dot(kt_s[d, chunk_rows(cs[d]), :], v_new[d]) for d in range(2)]
        for d in range(2):
            acc_ref[d, chunk_rows(cs[d]), :] = ws_qs[d][CHUNK:] + o[d]
        return tuple(cd_s[d, cs[d]][:1, :] * s_prev[d] + ktv[d] for d in range(2))

    s_zero = jnp.zeros((G_DK, G_DV), F32)
    lax.fori_loop(0, N_CHUNKS, step, (s_zero, s_zero))

    def finish(c, carry):
        rows = pl.ds(pl.multiple_of(c * CHUNK, CHUNK), CHUNK)
        h = acc_ref[0, rows, :] + acc_ref[1, rows, :]
        h = h * lax.rsqrt(jnp.mean(h * h, axis=-1, keepdims=True) + EPS) * ng_ref[...]
        z = z_ref[0, rows, :].astype(F32)
        y_ref[0, rows, :] = (h * (z * _sigmoid(z))).astype(y_ref.dtype)
        return carry

    lax.fori_loop(0, N_CHUNKS, finish, 0, unroll=4)


def gdn_mixer(big, small, conv_w, a_log, dt_bias, norm_g):
    bsz = big.shape[0]
    gates = _head_gate_rows(small, 4 * M_HEADS, G_HEADS, True)
    gp = jnp.zeros((G_HEADS, 2, GATE_ROWS, CHUNK), F32)
    for d in range(2):
        gp = gp.at[:, 0, 2 * d + 1].set(-jnp.exp(a_log[d])[:, None]).at[:, 1, 2 * d + 1].set(dt_bias[d][:, None])
    cw = conv_w.reshape(CONV_W, 3, G_HEADS, G_DK).transpose(2, 0, 1, 3).reshape(G_HEADS, CONV_W, 3 * G_DK)
    cw = jnp.pad(cw, ((0, 0), (0, 8 - CONV_W), (0, 0)))
    tri, eye = _scan_constants()
    masks = _block_masks()
    qb, kb, vb, zb = C_GQ // G_DK, C_GK // G_DK, C_GV // G_DV, C_GZ // G_DV
    return pl.pallas_call(
        _gdn_kernel,
        out_shape=jax.ShapeDtypeStruct((bsz, S_ALL, BRANCH_W), BF16),
        grid=(bsz, G_HEADS),
        in_specs=[pl.BlockSpec((1, S_ALL, G_DK), lambda b, h: (b, 0, qb + h)),
                  pl.BlockSpec((1, S_ALL, G_DK), lambda b, h: (b, 0, kb + h)),
                  pl.BlockSpec((1, S_ALL, G_DV), lambda b, h: (b, 0, vb + h)),
                  pl.BlockSpec((1, S_ALL, G_DV), lambda b, h: (b, 0, zb + h)),
                  pl.BlockSpec((1, 1, N_CHUNKS, GATE_ROWS, CHUNK), lambda b, h: (b, h, 0, 0, 0)),
                  pl.BlockSpec((1, 2, GATE_ROWS, CHUNK), lambda b, h: (h, 0, 0, 0)),
                  pl.BlockSpec((1, 8, 3 * G_DK), lambda b, h: (h, 0, 0)),
                  pl.BlockSpec((1, G_DV), lambda b, h: (0, 0)),
                  pl.BlockSpec((CHUNK, 3 * CHUNK), lambda b, h: (0, 0)),
                  pl.BlockSpec((CHUNK, CHUNK), lambda b, h: (0, 0)),
                  pl.BlockSpec(masks.shape, lambda b, h: (0, 0, 0))],
        out_specs=pl.BlockSpec((1, S_ALL, G_DV), lambda b, h: (b, 0, h)),
        scratch_shapes=[pltpu.VMEM((S_ALL, G_DK), BF16), pltpu.VMEM((S_ALL, G_DK), BF16),
                        pltpu.VMEM((S_ALL, G_DV), BF16),
                        pltpu.VMEM((2, S_ALL, G_DV), F32),
                        pltpu.VMEM((2, 2 * S_ALL, G_DK), BF16),
                        pltpu.VMEM((2, S_ALL, 2 * CHUNK), BF16),
                        pltpu.VMEM((2, S_ALL, G_DK), BF16),
                        pltpu.VMEM((2, N_CHUNKS, 8, G_DV), F32),
                        pltpu.VMEM((2, S_ALL, G_DV), F32)],
        compiler_params=_cparams(("arbitrary", "arbitrary")),
        name="gdn_mixer",
    )(big, big, big, big, gates, gp, cw, norm_g.reshape(1, G_DV), tri, eye, masks)


GRID_H = SEQ // GRID_W
LRU_BLK = 256


def _shift_rows(a, s):
    n = a.shape[0]
    r = lax.broadcasted_iota(jnp.int32, a.shape, 0)
    return jnp.where((r >= s) & (r < n + s), pltpu.roll(a, s % n, 0), 0.0)


def _gelu(v):
    return 0.5 * v * (1.0 + jnp.tanh(0.7978845608028654 * (v + 0.044715 * v * v * v)))


def _lru_kernel(x_ref, y_ref, cw_ref, cb_ref, gw_ref, gb_ref, lam_ref, o_ref,
                xb_s, a_s, u_s, h_s, p_s, end_s, cin_s):
    cw = cw_ref[...]
    xf = x_ref[0].astype(F32)
    xc = xf[:CTX_LEN]
    lat = lambda r0, r1: xf[CTX_LEN + r0 * GRID_W:CTX_LEN + r1 * GRID_W]
    prev1 = jnp.concatenate([_shift_rows(lat(GRID_H - 1, GRID_H), 1), lat(0, GRID_H - 1)], axis=0)
    prev2 = jnp.concatenate([_shift_rows(lat(GRID_H - 2, GRID_H - 1), 1), _shift_rows(lat(GRID_H - 1, GRID_H), 1),
                             lat(0, GRID_H - 2)], axis=0)
    next1 = jnp.concatenate([lat(1, GRID_H), _shift_rows(lat(0, 1), -1)], axis=0)
    taps_c = (_shift_rows(xc, 2), _shift_rows(xc, 1), xc, _shift_rows(xc, -1))
    taps_l = (prev2, prev1, lat(0, GRID_H), next1)
    xb_s[:CTX_LEN, :] = sum(t * cw[j:j + 1, :] for j, t in enumerate(taps_c)) + cb_ref[...]
    xb_s[CTX_LEN:, :] = sum(t * cw[j:j + 1, :] for j, t in enumerate(taps_l)) + cb_ref[...]

    neg_c_sp = -LRU_C * _softplus(-lam_ref[...])

    def gates(i, carry):
        rows = pl.ds(pl.multiple_of(i * LRU_BLK, LRU_BLK), LRU_BLK)
        xb = xb_s[rows, :]
        pre = jnp.dot(xb.astype(BF16), gw_ref[0], preferred_element_type=F32) + gb_ref[0]
        for d in range(2):
            r = _sigmoid(pre[:, (2 * d) * LANE:(2 * d + 1) * LANE])
            g_in = _sigmoid(pre[:, (2 * d + 1) * LANE:(2 * d + 2) * LANE])
            log_a = neg_c_sp[d:d + 1, :] * r
            a = jnp.exp(log_a)
            a_s[d, rows, :] = a
            u_s[d, rows, :] = jnp.sqrt(jnp.maximum(1.0 - a * a, 0.0)) * g_in * xb
        return carry

    lax.fori_loop(0, S_ALL // LRU_BLK, gates, 0)

    def ctx_step(t, carry):
        hf, hb = carry
        tb = CTX_LEN - 1 - t
        hf = a_s[0, pl.ds(t, 1), :] * hf + u_s[0, pl.ds(t, 1), :]
        hb = a_s[1, pl.ds(tb, 1), :] * hb + u_s[1, pl.ds(tb, 1), :]
        h_s[0, pl.ds(t, 1), :] = hf
        h_s[1, pl.ds(tb, 1), :] = hb
        return hf, hb

    zero_row = jnp.zeros((1, LANE), F32)
    hf0, hb0 = lax.fori_loop(0, CTX_LEN, ctx_step, (zero_row, zero_row), unroll=8)

    def col_step(i, carry):
        hf, pf, hb, pb = carry
        rf = pl.ds(pl.multiple_of(CTX_LEN + i * GRID_W, GRID_W), GRID_W)
        rb = pl.ds(pl.multiple_of(CTX_LEN + (GRID_H - 1 - i) * GRID_W, GRID_W), GRID_W)
        af, ab = a_s[0, rf, :], a_s[1, rb, :]
        hf = af * hf + u_s[0, rf, :]
        pf = af * pf
        hb = ab * hb + u_s[1, rb, :]
        pb = ab * pb
        h_s[0, rf, :] = hf
        p_s[0, pl.ds(pl.multiple_of(i * GRID_W, GRID_W), GRID_W), :] = pf
        h_s[1, rb, :] = hb
        p_s[1, pl.ds(pl.multiple_of((GRID_H - 1 - i) * GRID_W, GRID_W), GRID_W), :] = pb
        return hf, pf, hb, pb

    zeros = jnp.zeros((GRID_W, LANE), F32)
    ones = jnp.ones((GRID_W, LANE), F32)
    hf, pf, hb, pb = lax.fori_loop(0, GRID_H, col_step, (zeros, ones, zeros, ones))
    end_s[0], end_s[1], end_s[2], end_s[3] = hf, pf, hb, pb

    def chain_step(w, carry):
        cf, cb = carry
        wb = GRID_W - 1 - w
        cin_s[0, pl.ds(w, 1), :] = cf
        cin_s[1, pl.ds(wb, 1), :] = cb
        cf = end_s[0, pl.ds(w, 1), :] + end_s[1, pl.ds(w, 1), :] * cf
        cb = end_s[2, pl.ds(wb, 1), :] + end_s[3, pl.ds(wb, 1), :] * cb
        return cf, cb

    lax.fori_loop(0, GRID_W, chain_step, (hf0, hb0), unroll=8)

    o_ref[0, :CTX_LEN, :] = ((h_s[0, :CTX_LEN, :] + h_s[1, :CTX_LEN, :])
                             * _gelu(y_ref[0, :CTX_LEN, :].astype(F32))).astype(o_ref.dtype)

    def out_step(i, carry):
        rows = pl.ds(pl.multiple_of(CTX_LEN + i * GRID_W, GRID_W), GRID_W)
        prow = pl.ds(pl.multiple_of(i * GRID_W, GRID_W), GRID_W)
        h = (h_s[0, rows, :] + p_s[0, prow, :] * cin_s[0] + h_s[1, rows, :] + p_s[1, prow, :] * cin_s[1])
        o_ref[0, rows, :] = (h * _gelu(y_ref[0, rows, :].astype(F32))).astype(o_ref.dtype)
        return carry

    lax.fori_loop(0, GRID_H, out_step, 0, unroll=2)


def lru_mixer(big, conv_w, conv_b, gate_w, gate_b, lam):
    bsz = big.shape[0]
    gw = gate_w.transpose(2, 3, 0, 1, 4).reshape(R_BLOCKS, R_BDIM, 4 * R_BDIM).astype(BF16)
    gb = gate_b.reshape(2, 2, R_BLOCKS, R_BDIM).transpose(2, 0, 1, 3).reshape(R_BLOCKS, 1, 4 * R_BDIM)
    cw = jnp.pad(conv_w, ((0, 8 - CONV_W), (0, 0)))
    xb, yb = C_RX // R_BDIM, C_RY // R_BDIM
    return pl.pallas_call(
        _lru_kernel,
        out_shape=jax.ShapeDtypeStruct((bsz, S_ALL, BRANCH_W), BF16),
        grid=(bsz, R_BLOCKS),
        in_specs=[pl.BlockSpec((1, S_ALL, R_BDIM), lambda b, n: (b, 0, xb + n)),
                  pl.BlockSpec((1, S_ALL, R_BDIM), lambda b, n: (b, 0, yb + n)),
                  pl.BlockSpec((8, R_BDIM), lambda b, n: (0, n)),
                  pl.BlockSpec((1, R_BDIM), lambda b, n: (0, n)),
                  pl.BlockSpec((1, R_BDIM, 4 * R_BDIM), lambda b, n: (n, 0, 0)),
                  pl.BlockSpec((1, 1, 4 * R_BDIM), lambda b, n: (n, 0, 0)),
                  pl.BlockSpec((2, R_BDIM), lambda b, n: (0, n))],
        out_specs=pl.BlockSpec((1, S_ALL, R_BDIM), lambda b, n: (b, 0, n)),
        scratch_shapes=[pltpu.VMEM((S_ALL, R_BDIM), F32), pltpu.VMEM((2, S_ALL, R_BDIM), F32),
                        pltpu.VMEM((2, S_ALL, R_BDIM), F32), pltpu.VMEM((2, S_ALL, R_BDIM), F32),
                        pltpu.VMEM((2, SEQ, R_BDIM), F32), pltpu.VMEM((4, GRID_W, R_BDIM), F32),
                        pltpu.VMEM((2, GRID_W, R_BDIM), F32)],
        compiler_params=_cparams(("arbitrary", "arbitrary")),
        name="lru_mixer",
    )(big, big, cw, conv_b.reshape(1, BRANCH_W), gw, gb, lam)


def _pack_in_weights(w, b):
    segs_w = [w[:, IN_OFFS[i]:IN_OFFS[i + 1]] for i in range(len(IN_COLS))]
    segs_b = [b[IN_OFFS[i]:IN_OFFS[i + 1]] for i in range(len(IN_COLS))]
    w_big = jnp.concatenate([segs_w[i] for i in BIG_SEGS], axis=1).astype(BF16)
    b_big = jnp.concatenate([segs_b[i] for i in BIG_SEGS]).reshape(1, N_BIG)
    pad = N_SMALL - sum(IN_COLS[i] for i in SMALL_SEGS)
    w_small = jnp.pad(jnp.concatenate([segs_w[i] for i in SMALL_SEGS], axis=1), ((0, 0), (0, pad))).astype(BF16)
    b_small = jnp.pad(jnp.concatenate([segs_b[i] for i in SMALL_SEGS]), (0, pad)).reshape(1, N_SMALL)
    return w_big, b_big, w_small, b_small


def kernel(x, c, ctx, c_ctx, mod_w, mod_b, norm1_g, norm2_g, final_g, w_in, b_in, m_fbias, m_norm_g,
           g_conv, g_a_log, g_dt_bias, g_norm_g, r_conv, r_conv_b, r_gate_w, r_gate_b, r_lambda,
           w_branch, w_out, ffn_w1, ffn_w3, ffn_w2, router_w, router_b, moe_w1, moe_w3, moe_w2):
    bsz = x.shape[0]
    xs = jnp.concatenate([ctx, x], axis=1)
    src = jnp.concatenate([c, c_ctx[None]], axis=0)
    mods = modulation(src, mod_w, mod_b)
    for layer in range(DEPTH):
        mod = mods[layer]
        w_big, b_big, w_small, b_small = _pack_in_weights(w_in[layer], b_in[layer])
        big, small = in_projection(xs, norm1_g[layer], mod, w_big, b_big, w_small, b_small)
        ym = mlstm_mixer(big, small, m_fbias[layer], m_norm_g[layer])
        yg = gdn_mixer(big, small, g_conv[layer], g_a_log[layer], g_dt_bias[layer], g_norm_g[layer])
        yr = lru_mixer(big, r_conv[layer], r_conv_b[layer], r_gate_w[layer], r_gate_b[layer], r_lambda[layer])
        xs = merge_branches(ym, yg, yr, big, xs, mod, w_branch[layer].astype(BF16), w_out[layer].astype(BF16))
        j = layer // 2
        if layer % 2 == 0:
            xs = dense_ffn(xs, norm2_g[layer], mod, ffn_w1[j].astype(BF16), ffn_w3[j].astype(BF16),
                           ffn_w2[j].astype(BF16))
        else:
            gt2 = jnp.concatenate([jnp.broadcast_to(mod[bsz:, None, 5 * D_MODEL:], (bsz, CTX_LEN, D_MODEL)),
                                   jnp.broadcast_to(mod[:bsz, None, 5 * D_MODEL:], (bsz, SEQ, D_MODEL))], axis=1)
            xs = moe_ffn(xs, norm2_g[layer], mod, gt2, router_w[j], router_b[j], moe_w1[j].astype(BF16),
                         moe_w3[j].astype(BF16), moe_w2[j].astype(BF16))
    return final_norm(xs, final_g)
```
